```python
import math
import jax
import jax.numpy as jnp
from jax import lax
import numpy as np


D_MODEL = 1024
BATCH = 1
SEQ = 16384
DEPTH = 4
DEC_BATCH = 8
DEC_SEQ = 8192
PAST_LEN = 128

HEAD_DIM = 64
BR_HEADS = 4
BR_W = BR_HEADS * HEAD_DIM
N_BRANCH = 5
HG_CHUNK = 64
DF_DK = HEAD_DIM // 2
Q_BLOCK = 128
WIN = 128
WIN_BLOCK = 128
WIN_KV_HEADS = 2
WIN_KV_W = WIN_KV_HEADS * HEAD_DIM
GRID_W = 64
NA_KH = 8
NA_KW = 16
MEM_LEN = 256
N_BUCKETS = 32
MAX_DIST = 128
EPS = 1e-6
NEG = -1e30
LB_FLOOR = 1e-30
F32 = jnp.float32

A_Q = 0
A_I = A_Q + BR_W
A_FF = A_I + BR_W
A_FB = A_FF + BR_W
A_G = A_FB + BR_W
B_Q = A_G + BR_W
B_K = B_Q + BR_W
B_V = B_K + BR_W
B_G = B_V + BR_W
C_Q = B_G + BR_W
C_K = C_Q + BR_W
C_V = C_K + WIN_KV_W
C_G = C_V + WIN_KV_W
D_Q = C_G + BR_W
D_K = D_Q + BR_W
D_V = D_K + BR_W
D_G = D_V + BR_W
E_Q = D_G + BR_W
E_G = E_Q + BR_W
M_G = E_G + BR_W
IN_COLS = M_G + N_BRANCH * D_MODEL

kernel_name = 'bidir_hybrid_gated_encoder'


def _rms(x, g):
    x32 = x.astype(F32)
    y = x32 * lax.rsqrt(jnp.mean(x32 * x32, axis=-1, keepdims=True) + EPS)
    return (y * g.astype(F32)).astype(x.dtype)


def _t5_bucket(rel):
    half = N_BUCKETS // 2
    exact = half // 2
    n = jnp.abs(rel)
    nf = jnp.maximum(n, 1).astype(F32)
    large = exact + (jnp.log(nf / exact) / math.log(MAX_DIST / exact) * (half - exact)).astype(jnp.int32)
    large = jnp.clip(large, 0, half - 1)
    return jnp.where(rel > 0, half, 0) + jnp.where(n < exact, n, large)


def _hgrn2(q, i, f_fwd, f_bwd, lb, norm_g):
    bsz, n, nh, dk = q.shape
    dv = i.shape[-1]
    qs = jax.nn.silu(q.astype(F32))
    v = i.astype(F32)
    z = jnp.stack([f_fwd, f_bwd[:, ::-1]]).astype(F32)
    lb = lb.astype(F32)[:, None, None]
    log_lb = jnp.log(jnp.maximum(lb, LB_FLOOR))
    logf = jnp.logaddexp(log_lb, jnp.log1p(-lb) + jax.nn.log_sigmoid(z))
    k = (1.0 - lb) * jax.nn.sigmoid(-z)
    qq = jnp.stack([qs, qs[:, ::-1]])
    vv = jnp.stack([v, v[:, ::-1]])
    nc = n // HG_CHUNK

    def chunks(a):
        return a.reshape(2, bsz, nc, HG_CHUNK, nh, a.shape[-1]).transpose(2, 0, 1, 4, 3, 5)

    tri = jnp.tril(jnp.ones((HG_CHUNK, HG_CHUNK), dtype=bool))[:, :, None]

    def step(S, inp):
        qc, kc, vc, gc = inp
        b = jnp.cumsum(gc, axis=-2)
        b_last = b[..., -1:, :]
        inter = jnp.einsum('zbhtk,zbhkv->zbhtv', qc * jnp.exp(b), S)
        diff = b[..., :, None, :] - b[..., None, :, :]
        dec = jnp.where(tri, jnp.exp(jnp.where(tri, diff, 0.0)), 0.0)
        a = jnp.einsum('zbhtk,zbhsk,zbhtsk->zbhts', qc, kc, dec)
        o = inter + jnp.einsum('zbhts,zbhsv->zbhtv', a, vc)
        S = S * jnp.exp(b_last)[..., 0, :, None] + jnp.einsum('zbhsk,zbhsv->zbhkv', kc * jnp.exp(b_last - b), vc)
        return S, o

    S0 = jnp.zeros((2, bsz, nh, dk, dv), F32)
    _, outs = lax.scan(step, S0, (chunks(qq), chunks(k), chunks(vv), chunks(logf)))
    outs = outs.transpose(1, 2, 0, 4, 3, 5).reshape(2, bsz, n, nh, dv)
    o = outs[0] + outs[1][:, ::-1]
    o = _rms(o, norm_g.reshape(nh, dv))
    return o.reshape(bsz, n, nh * dv)


def _diff_attn(q, k, v, qk_g, lam, lam_init, subln_g, rel_bias):
    bsz, n, nh = q.shape[:3]
    q = _rms(q, qk_g[0])
    k = _rms(k, qk_g[1])
    lam = lam.astype(F32)
    lmb = jnp.exp(jnp.sum(lam[0] * lam[1])) - jnp.exp(jnp.sum(lam[2] * lam[3])) + lam_init
    scale = DF_DK ** -0.5
    nb = n // Q_BLOCK
    qb = q.reshape(bsz, nb, Q_BLOCK, nh, 2, DF_DK).swapaxes(0, 1)
    kpos = jnp.arange(n)
    table = rel_bias[:, :BR_HEADS].astype(F32)

    def block(args):
        qj, j = args
        qpos = j * Q_BLOCK + jnp.arange(Q_BLOCK)
        bias = table[_t5_bucket(kpos[None, :] - qpos[:, None])].transpose(2, 0, 1)
        s = jnp.einsum('bqhcd,bkhcd->bchqk', qj, k, preferred_element_type=F32) * scale + bias
        p = jax.nn.softmax(s, axis=-1)
        a = p[:, 0] - lmb * p[:, 1]
        return jnp.einsum('bhqk,bkhd->bqhd', a.astype(v.dtype), v, preferred_element_type=F32)

    o = lax.map(block, (qb, jnp.arange(nb)))
    o = o.swapaxes(0, 1).reshape(bsz, n, nh, 2 * DF_DK)
    o = _rms(o, subln_g) * (1.0 - lam_init)
    return o.reshape(bsz, n, nh * 2 * DF_DK)


def _window_attn(q, k, v, qk_g, sink, rel_bias):
    bsz, n, nh, d = q.shape
    kvh = k.shape[2]
    rep = nh // kvh
    q = _rms(q, qk_g[0])
    k = _rms(k, qk_g[1])
    nb = n // WIN_BLOCK
    qb = q.reshape(bsz, nb, WIN_BLOCK, kvh, rep, d)

    def band(a):
        ap = jnp.pad(a, ((0, 0), (WIN_BLOCK, WIN_BLOCK), (0, 0), (0, 0))).reshape(bsz, nb + 2, WIN_BLOCK, kvh, d)
        return jnp.concatenate([ap[:, :-2], ap[:, 1:-1], ap[:, 2:]], axis=2)

    kb = band(k)
    vb = band(v)
    rel = jnp.arange(3 * WIN_BLOCK)[None, :] - WIN_BLOCK - jnp.arange(WIN_BLOCK)[:, None]
    kpos = jnp.arange(nb)[:, None] * WIN_BLOCK - WIN_BLOCK + jnp.arange(3 * WIN_BLOCK)[None, :]
    valid = (jnp.abs(rel) <= WIN)[None] & ((kpos >= 0) & (kpos < n))[:, None, :]
    bias = rel_bias[:, BR_HEADS:].astype(F32)[_t5_bucket(rel)].transpose(2, 0, 1)
    bias = bias.reshape(kvh, rep, WIN_BLOCK, 3 * WIN_BLOCK)
    s = jnp.einsum('bjqgrd,bjkgd->bjgrqk', qb, kb, preferred_element_type=F32) * d ** -0.5 + bias
    s = jnp.where(valid[:, None, None], s, NEG)
    sk = sink.astype(F32).reshape(kvh, rep)[..., None, None]
    m = jnp.maximum(jnp.max(s, axis=-1, keepdims=True), sk)
    p = jnp.where(valid[:, None, None], jnp.exp(s - m), 0.0)
    den = jnp.sum(p, axis=-1, keepdims=True) + jnp.exp(sk - m)
    o = jnp.einsum('bjgrqk,bjkgd->bjqgrd', (p / den).astype(v.dtype), vb, preferred_element_type=F32)
    return o.reshape(bsz, n, nh * d)


def _neigh_attn(q, k, v, qk_g, rpb):
    bsz, n, nh, d = q.shape
    rows = n // GRID_W
    kh = min(NA_KH, rows)
    kw = NA_KW
    q = _rms(q, qk_g[0])
    k = _rms(k, qk_g[1])
    qg = q.reshape(bsz, rows, GRID_W, nh, d).swapaxes(0, 1)
    kg = k.reshape(bsz, rows, GRID_W, nh, d)
    vg = v.reshape(bsz, rows, GRID_W, nh, d)
    col = jnp.arange(GRID_W)
    cols = jnp.clip(col - kw // 2, 0, GRID_W - kw)[:, None] + jnp.arange(kw)[None, :]
    dc = cols - col[:, None] + (NA_KW - 1)
    rpb = rpb.astype(F32)

    def row(args):
        qr, r = args
        rs = jnp.clip(r - kh // 2, 0, rows - kh)
        kr = lax.dynamic_slice_in_dim(kg, rs, kh, axis=1)[:, :, cols]
        vr = lax.dynamic_slice_in_dim(vg, rs, kh, axis=1)[:, :, cols]
        dr = rs + jnp.arange(kh) - r + (NA_KH - 1)
        bias = rpb[:, dr[:, None, None], dc[None]].transpose(0, 2, 1, 3)
        s = jnp.einsum('bqhd,biqjhd->bhqij', qr, kr, preferred_element_type=F32) * d ** -0.5 + bias
        p = jax.nn.softmax(s.reshape(bsz, nh, GRID_W, kh * kw), axis=-1).reshape(s.shape)
        return jnp.einsum('bhqij,biqjhd->bqhd', p.astype(v.dtype), vr, preferred_element_type=F32)

    o = lax.map(row, (qg, jnp.arange(rows)))
    return o.swapaxes(0, 1).reshape(bsz, n, nh * d)


def _mem_attn(q, mk, mv, qk_g):
    bsz, n, nh, d = q.shape
    q = _rms(q, qk_g[0])
    mk = _rms(mk, qk_g[1])
    s = jnp.einsum('bqhd,bkhd->bhqk', q, mk, preferred_element_type=F32) * d ** -0.5
    p = jax.nn.softmax(s, axis=-1)
    o = jnp.einsum('bhqk,bkhd->bqhd', p.astype(mv.dtype), mv, preferred_element_type=F32)
    return o.reshape(bsz, n, nh * d)


def _layer(x, mem, w_in, norm_g, mem_norm_g, w_mem_kv, rel_bias, lb, hg_norm_g,
           df_qk_g, df_lam, lam_init, df_subln_g, win_qk_g, win_sink, na_qk_g, na_rpb,
           mem_qk_g, w_branch, w_out):
    bsz, n, _ = x.shape
    dt = x.dtype
    h = _rms(x, norm_g)

    def proj(off, width, nh):
        return (h @ w_in[:, off:off + width]).reshape(bsz, n, nh, width // nh)

    def gate(off):
        return jax.nn.silu(h @ w_in[:, off:off + BR_W])

    o_a = _hgrn2(proj(A_Q, BR_W, BR_HEADS), proj(A_I, BR_W, BR_HEADS), proj(A_FF, BR_W, BR_HEADS),
                 proj(A_FB, BR_W, BR_HEADS), lb, hg_norm_g)
    qd = proj(B_Q, BR_W, BR_HEADS).reshape(bsz, n, BR_HEADS, 2, DF_DK)
    kd = proj(B_K, BR_W, BR_HEADS).reshape(bsz, n, BR_HEADS, 2, DF_DK)
    o_b = _diff_attn(qd, kd, proj(B_V, BR_W, BR_HEADS), df_qk_g, df_lam, lam_init, df_subln_g, rel_bias)
    o_c = _window_attn(proj(C_Q, BR_W, BR_HEADS), proj(C_K, WIN_KV_W, WIN_KV_HEADS),
                       proj(C_V, WIN_KV_W, WIN_KV_HEADS), win_qk_g, win_sink, rel_bias)
    o_d = _neigh_attn(proj(D_Q, BR_W, BR_HEADS), proj(D_K, BR_W, BR_HEADS), proj(D_V, BR_W, BR_HEADS),
                      na_qk_g, na_rpb)
    mh = _rms(mem, mem_norm_g)
    mkv = (mh @ w_mem_kv).reshape(bsz, mem.shape[1], 2, BR_HEADS, HEAD_DIM)
    o_e = _mem_attn(proj(E_Q, BR_W, BR_HEADS), mkv[:, :, 0], mkv[:, :, 1], mem_qk_g)

    branches = (o_a.astype(dt) * gate(A_G), o_b.astype(dt) * gate(B_G), o_c.astype(dt) * gate(C_G),
                o_d.astype(dt) * gate(D_G), o_e.astype(dt) * gate(E_G))
    merged = sum(jax.nn.sigmoid(h @ w_in[:, M_G + kb * D_MODEL:M_G + (kb + 1) * D_MODEL]) * (ob @ w_branch[kb])
                 for kb, ob in enumerate(branches))
    return x + (merged @ w_out).astype(dt)


def setup_inputs(seed: int = 0) -> dict:
    key = jax.random.key(seed)
    ks = jax.random.split(key, 21)
    nrm = jax.random.normal
    return {
        'x_prompt': nrm(ks[0], (BATCH, SEQ, D_MODEL), F32),
        'x_sample': nrm(ks[1], (DEC_BATCH, DEC_SEQ, D_MODEL), F32),
        'mem_prompt': nrm(ks[2], (BATCH, MEM_LEN, D_MODEL), F32),
        'mem_sample': nrm(ks[3], (DEC_BATCH, MEM_LEN, D_MODEL), F32),
        'norm_g': 1.0 + 0.1 * nrm(ks[4], (DEPTH, D_MODEL), F32),
        'mem_norm_g': 1.0 + 0.1 * nrm(ks[5], (DEPTH, D_MODEL), F32),
        'w_in': nrm(ks[6], (DEPTH, D_MODEL, IN_COLS), F32) * D_MODEL ** -0.5,
        'w_mem_kv': nrm(ks[7], (DEPTH, D_MODEL, 2 * BR_W), F32) * D_MODEL ** -0.5,
        'rel_bias': 0.5 * nrm(ks[8], (N_BUCKETS, 2 * BR_HEADS), F32),
        'hgrn_lb': nrm(ks[9], (2, DEPTH, BR_W), F32),
        'hgrn_norm_g': 1.0 + 0.1 * nrm(ks[10], (DEPTH, BR_W), F32),
        'diff_qk_g': 1.0 + 0.1 * nrm(ks[11], (DEPTH, 2, DF_DK), F32),
        'diff_lambda': 0.1 * nrm(ks[12], (DEPTH, 4, DF_DK), F32),
        'diff_subln_g': 1.0 + 0.1 * nrm(ks[13], (DEPTH, 2 * DF_DK), F32),
        'win_qk_g': 1.0 + 0.1 * nrm(ks[14], (DEPTH, 2, HEAD_DIM), F32),
        'win_sink': 0.5 * nrm(ks[15], (DEPTH, BR_HEADS), F32),
        'na_qk_g': 1.0 + 0.1 * nrm(ks[16], (DEPTH, 2, HEAD_DIM), F32),
        'na_rpb': 0.5 * nrm(ks[17], (DEPTH, BR_HEADS, 2 * NA_KH - 1, 2 * NA_KW - 1), F32),
        'mem_qk_g': 1.0 + 0.1 * nrm(ks[18], (DEPTH, 2, HEAD_DIM), F32),
        'w_branch': nrm(ks[19], (DEPTH, N_BRANCH, BR_W, D_MODEL), F32) * BR_W ** -0.5,
        'w_out': nrm(ks[20], (DEPTH, D_MODEL, D_MODEL), F32) * D_MODEL ** -0.5,
    }


def reference(x_prompt, x_sample, mem_prompt, mem_sample, norm_g, mem_norm_g, w_in, w_mem_kv, rel_bias,
              hgrn_lb, hgrn_norm_g, diff_qk_g, diff_lambda, diff_subln_g, win_qk_g, win_sink, na_qk_g,
              na_rpb, mem_qk_g, w_branch, w_out):
    sm = jax.nn.softmax(hgrn_lb.astype(F32), axis=1)
    lb_all = jnp.clip(jnp.cumsum(sm, axis=1) - sm[:, :1], 0.0, 1.0 - 1e-6)

    def trunk(x, mem):
        for l in range(DEPTH):
            x = _layer(x, mem, w_in[l], norm_g[l], mem_norm_g[l], w_mem_kv[l], rel_bias,
                       lb_all[:, l].reshape(2, BR_HEADS, HEAD_DIM), hgrn_norm_g[l],
                       diff_qk_g[l], diff_lambda[l], 0.8 - 0.6 * math.exp(-0.3 * l), diff_subln_g[l],
                       win_qk_g[l], win_sink[l], na_qk_g[l], na_rpb[l], mem_qk_g[l], w_branch[l], w_out[l])
        return x

    y_prompt = trunk(x_prompt, mem_prompt)
    y_sample = trunk(x_sample, mem_sample)
    return (y_prompt, y_sample)
```

```python
import functools
import math

import numpy as np
import jax
import jax.numpy as jnp
from jax import lax
from jax.experimental import pallas as pl
from jax.experimental.pallas import tpu as pltpu

F32 = jnp.float32
BF16 = jnp.bfloat16

D_MODEL = 1024
HEAD_DIM = 64
BR_HEADS = 4
BR_W = BR_HEADS * HEAD_DIM
N_BRANCH = 5
DF_DK = HEAD_DIM // 2
WIN = 128
WIN_BLOCK = 128
WIN_KV_HEADS = 2
GRID_W = 64
NA_KH = 8
NA_KW = 16
N_BUCKETS = 32
MAX_DIST = 128
EPS = 1e-6
NEG = -1e30
LB_FLOOR = 1e-30
LOG2E = 1.4426950408889634

A_Q = 0
A_G = 4 * BR_W
B_Q = A_G + BR_W
B_G = B_Q + 3 * BR_W
C_Q = B_G + BR_W
C_K = C_Q + BR_W
C_V = C_K + WIN_KV_HEADS * HEAD_DIM
C_G = C_V + WIN_KV_HEADS * HEAD_DIM
D_Q = C_G + BR_W
D_G = D_Q + 3 * BR_W
E_Q = D_G + BR_W
E_G = E_Q + BR_W
M_G = E_G + BR_W

VMEM_LIMIT_BYTES = 56 * 1024 * 1024

PROJ_TM = 512
MERGE_TM = 256
MEM_TM = 256
HG_C = 128
DF_T = 256
NA_RB = 8
N_PROJ_SEG = 14
DF_VROWS = 80


def _params(sem):
    return pltpu.CompilerParams(dimension_semantics=sem, vmem_limit_bytes=VMEM_LIMIT_BYTES)


def _const_spec(shape):
    nd = len(shape)
    return pl.BlockSpec(shape, lambda *_: (0,) * nd, pipeline_mode=pl.Buffered(1))


def _dot(a, b):
    return jnp.dot(a, b, preferred_element_type=F32)


def _dot_nt(a, b):
    return lax.dot_general(a, b, (((1,), (1,)), ((), ())), preferred_element_type=F32)


def _rms_rows(x, g):
    ms = jnp.mean(x * x, axis=-1, keepdims=True)
    return x * lax.rsqrt(ms + EPS) * g


def _group_rms(x, gmat):
    x2 = x * x
    hi = x2.astype(BF16)
    lo = (x2 - hi.astype(F32)).astype(BF16)
    ms = _dot(hi, gmat) + _dot(lo, gmat)
    return x * lax.rsqrt(ms + EPS)


def _head_mask(width=BR_W):
    lane = lax.broadcasted_iota(jnp.int32, (1, width), 1)
    return [(lane >= h * HEAD_DIM) & (lane < (h + 1) * HEAD_DIM) for h in range(BR_HEADS)]


def _stack_heads(q, masks):
    zero = jnp.zeros_like(q)
    return jnp.concatenate([jnp.where(m, q, zero) for m in masks], axis=0)


def _unstack_heads(o_all, masks, m):
    out = jnp.zeros((m, BR_W), F32)
    for h, mk in enumerate(masks):
        out = out + jnp.where(mk, o_all[h * m:(h + 1) * m, :], 0.0)
    return out


class _Layout:
    def __init__(self, n_prompt, len_prompt, n_sample, len_sample):
        self.classes = ((0, n_prompt, len_prompt), (n_prompt * len_prompt, n_sample, len_sample))
        self.off1 = n_prompt * len_prompt
        self.lp = len_prompt
        self.ls = len_sample
        self.total = self.off1 + n_sample * len_sample
        self.nseq = n_prompt + n_sample
        self.n_prompt = n_prompt
        assert self.off1 % len_sample == 0

    def seq_bounds(self, tok):
        in_p = tok < self.off1
        start_p = (tok // self.lp) * self.lp
        start_s = self.off1 + ((tok - self.off1) // self.ls) * self.ls
        return jnp.where(in_p, start_p, start_s), jnp.where(in_p, self.lp, self.ls)

    def seq_index(self, tok):
        return jnp.where(tok < self.off1, tok // self.lp, self.n_prompt + (tok - self.off1) // self.ls)


def _proj_body(x_ref, ng_ref, w_ref, ep_ref, g32_ref, g64_ref,
               aq_ref, av_ref, alf_ref, akk_ref, bq_ref, bk_ref, bv_ref,
               cq_ref, ck_ref, cv_ref, dq_ref, dk_ref, dv_ref, eq_ref):
    h = _rms_rows(x_ref[...], ng_ref[...]).astype(BF16)
    ep = ep_ref[...]

    def seg(i):
        return _dot(h, w_ref[:, i * BR_W:(i + 1) * BR_W])

    def row(r):
        return ep[r:r + 1, :]

    aq = seg(0)
    aq_ref[...] = (aq * jax.nn.sigmoid(aq)).astype(BF16)
    av_ref[...] = seg(1).astype(BF16)
    for d in range(2):
        z = seg(2 + d)
        lb = row(7 + d)
        e = jnp.exp(-jnp.abs(z))
        log_sig = jnp.minimum(z, 0.0) - jnp.log1p(e)
        t1 = jnp.log(jnp.maximum(lb, LB_FLOOR))
        t2 = jnp.log1p(-lb) + log_sig
        logf = jnp.maximum(t1, t2) + jnp.log1p(jnp.exp(-jnp.abs(t1 - t2)))
        sig_neg = jnp.where(z >= 0.0, e, 1.0) / (1.0 + e)
        alf_ref[:, d * BR_W:(d + 1) * BR_W] = logf
        akk_ref[:, d * BR_W:(d + 1) * BR_W] = ((1.0 - lb) * sig_neg).astype(BF16)
    g32 = g32_ref[...]
    g64 = g64_ref[...]
    bq_ref[...] = (_group_rms(seg(4), g32) * row(0)).astype(BF16)
    bk_ref[...] = (_group_rms(seg(5), g32) * row(1)).astype(BF16)
    bv_ref[...] = seg(6).astype(BF16)
    cq_ref[...] = (_group_rms(seg(7), g64) * row(2)).astype(BF16)
    ck_ref[...] = (_group_rms(seg(8), g64) * row(3)).astype(BF16)
    cv_ref[...] = seg(9).astype(BF16)
    dq_ref[...] = (_group_rms(seg(10), g64) * row(4)).astype(BF16)
    dk_ref[...] = (_group_rms(seg(11), g64) * row(5)).astype(BF16)
    dv_ref[...] = seg(12).astype(BF16)
    eq_ref[...] = (_group_rms(seg(13), g64) * row(6)).astype(BF16)


def _project(x, ng, w1, ep, g32, g64):
    t = x.shape[0]
    tm = PROJ_TM
    tile = lambda w: pl.BlockSpec((tm, w), lambda i: (i, 0))
    widths = [BR_W, BR_W, 2 * BR_W, 2 * BR_W] + [BR_W] * 10
    dtypes = [BF16, BF16, F32, BF16] + [BF16] * 10
    return pl.pallas_call(
        _proj_body,
        grid=(t // tm,),
        in_specs=[tile(D_MODEL), _const_spec((1, D_MODEL)), _const_spec(w1.shape), _const_spec(ep.shape),
                  _const_spec(g32.shape), _const_spec(g64.shape)],
        out_specs=[tile(w) for w in widths],
        out_shape=[jax.ShapeDtypeStruct((t, w), dt) for w, dt in zip(widths, dtypes)],
        compiler_params=_params(("parallel",)),
        name="proj",
    )(x, ng, w1, ep, g32, g64)


def _memkv_body(m_ref, g_ref, w_ref, gk_ref, g64_ref, mk_ref, mv_ref):
    mh = _rms_rows(m_ref[...], g_ref[...]).astype(BF16)
    kv = _dot(mh, w_ref[...])
    mk_ref[...] = (_group_rms(kv[:, :BR_W], g64_ref[...]) * gk_ref[...]).astype(BF16)
    mv_ref[...] = kv[:, BR_W:].astype(BF16)


def _mem_kv(mem, g, w, gk, g64):
    t = mem.shape[0]
    tm = 256
    return pl.pallas_call(
        _memkv_body,
        grid=(t // tm,),
        in_specs=[pl.BlockSpec((tm, D_MODEL), lambda i: (i, 0)), _const_spec((1, D_MODEL)), _const_spec(w.shape),
                  _const_spec((1, BR_W)), _const_spec(g64.shape)],
        out_specs=[pl.BlockSpec((tm, BR_W), lambda i: (i, 0))] * 2,
        out_shape=[jax.ShapeDtypeStruct((t, BR_W), BF16)] * 2,
        compiler_params=_params(("parallel",)),
        name="memkv",
    )(mem, g, w, gk, g64)


def _hgrn_constants(c, reverse):
    nl = int(math.log2(c))
    idx = np.arange(c)
    t = idx[:, None]
    u = idx[None, :]
    incl = (u <= t).astype(np.float32)
    rest = (u > t).astype(np.float32)
    tot = np.ones((8, c), np.float32)
    mds, mes, lms = [], [], [np.eye(c, dtype=np.float32)]
    for lev in range(nl):
        w = 1 << lev
        blk = idx // w
        odd = (blk % 2 == 1)
        md = (odd[:, None] & (u >= (blk * w)[:, None]) & (u <= t)).astype(np.float32)
        me = ((~odd)[:, None] & (u > t) & (u <= ((blk + 1) * w - 1)[:, None])).astype(np.float32)
        lm = (odd[:, None] & (blk[None, :] == (blk - 1)[:, None])).astype(np.float32)
        mds.append(md)
        mes.append(me)
        lms.append(lm)
    mats = [incl, rest, tot] + mds + mes
    if reverse:
        mats = [m[::-1, ::-1] for m in mats]
        lms = [m[::-1, ::-1] for m in lms]
    mall = np.concatenate(mats, axis=0)
    lmst = np.stack([np.tile(m, (BR_HEADS, 1)) for m in lms])
    bd = np.kron(np.eye(BR_HEADS, dtype=np.float32), np.ones((HEAD_DIM, HEAD_DIM), np.float32))
    return mall, lmst, bd


def _hgrn_body(layout, nct, reverse, final, q_ref, k_ref, v_ref, lf_ref, mall_ref, lm_ref, bd_ref, *rest):
    if final:
        of_ref, g64_ref, gn_ref, o_ref, st_ref = rest
    else:
        o_ref, st_ref = rest
    c = HG_C
    nl = int(math.log2(c))
    i = pl.program_id(0)
    chunk = (nct - 1 - i) if reverse else i
    tok = chunk * c
    start, length = layout.seq_bounds(tok)
    fresh = (tok + c == start + length) if reverse else (tok == start)

    @pl.when(fresh)
    def _():
        st_ref[...] = jnp.zeros_like(st_ref)

    masks = _head_mask()
    lf = lf_ref[...]
    hi = lf.astype(BF16)
    r1 = lf - hi.astype(F32)
    mid = r1.astype(BF16)
    lo = (r1 - mid.astype(F32)).astype(BF16)
    mall = mall_ref[...]
    cums = _dot(mall, hi) + _dot(mall, mid) + _dot(mall, lo)

    def blockrows(j):
        return cums[j * c:(j + 1) * c, :]

    b = blockrows(0)
    b_rest = blockrows(1)
    b_tot = cums[2 * c:2 * c + 1, :]
    base = 2 * c + 8

    qf = q_ref[...].astype(F32)
    kf = k_ref[...].astype(F32)
    v = v_ref[...]
    st = st_ref[...]

    o = _dot_nt((qf * jnp.exp(b)).astype(BF16), st.astype(BF16))
    a_all = _dot_nt(_stack_heads(q_ref[...], masks), k_ref[...]) * lm_ref[0]
    for lev in range(nl):
        dq = cums[base + lev * c:base + (lev + 1) * c, :]
        dk = cums[base + (nl + lev) * c:base + (nl + lev + 1) * c, :]
        ql = (qf * jnp.exp(dq)).astype(BF16)
        kl = (kf * jnp.exp(dk)).astype(BF16)
        a_all = a_all + _dot_nt(_stack_heads(ql, masks), kl) * lm_ref[lev + 1]
    a_bf = a_all.astype(BF16)
    zero = jnp.zeros_like(v)
    for h, mk in enumerate(masks):
        o = o + _dot(a_bf[h * c:(h + 1) * c, :], jnp.where(mk, v, zero))

    kst = (kf * jnp.exp(b_rest)).astype(BF16)
    vt = v.astype(F32).T.astype(BF16)
    st_ref[...] = (st * jnp.exp(b_tot) + _dot(vt, kst)) * bd_ref[...]

    if final:
        tot = of_ref[...] + o
        o_ref[...] = (_group_rms(tot, g64_ref[...]) * gn_ref[...]).astype(BF16)
    else:
        o_ref[...] = o


def _hgrn(layout, qs, v, kk, logf, consts_f, consts_b, g64, gn):
    t = qs.shape[0]
    c = HG_C
    nct = t // c

    def run(reverse, consts, extra_in, extra_specs, out_dtype):
        mall, lmst, bd = consts
        cm = (lambda i: nct - 1 - i) if reverse else (lambda i: i)
        d = 1 if reverse else 0
        tile = pl.BlockSpec((c, BR_W), lambda i: (cm(i), 0))
        half = pl.BlockSpec((c, BR_W), lambda i: (cm(i), d))
        return pl.pallas_call(
            functools.partial(_hgrn_body, layout, nct, reverse, reverse),
            grid=(nct,),
            in_specs=[tile, half, tile, half, _const_spec(mall.shape), _const_spec(lmst.shape),
                      _const_spec(bd.shape)] + extra_specs(tile),
            out_specs=tile,
            out_shape=jax.ShapeDtypeStruct((t, BR_W), out_dtype),
            scratch_shapes=[pltpu.VMEM((BR_W, BR_W), F32)],
            compiler_params=_params(("arbitrary",)),
            name="hgrn_bwd" if reverse else "hgrn_fwd",
        )(qs, kk, v, logf, mall, lmst, bd, *extra_in)

    o_f = run(False, consts_f, [], lambda tile: [], F32)
    return run(True, consts_b, [o_f, g64, gn],
               lambda tile: [tile, _const_spec(g64.shape), _const_spec((1, BR_W))], BF16)


def _diff_body(nkt, qt_ref, k_ref, vt_ref, bias_ref, sc_ref, gcol_ref, o_ref, qm_ref, m_ref, acc_ref):
    tq = DF_T
    qi = pl.program_id(1)
    qt = qt_ref[...]
    rowid = lax.broadcasted_iota(jnp.int32, (BR_W, tq), 0)
    for hc in range(2 * BR_HEADS):
        sel = (rowid >= hc * DF_DK) & (rowid < (hc + 1) * DF_DK)
        qm_ref[hc] = jnp.where(sel, qt, jnp.zeros_like(qt))
    m_ref[...] = jnp.full(m_ref.shape, NEG, F32)
    acc_ref[...] = jnp.zeros_like(acc_ref)

    def update(kj, tile_bias, const_bias):
        kt = k_ref[kj]
        for h in range(BR_HEADS):
            vt = vt_ref[kj, h]
            for cmap in range(2):
                hc = 2 * h + cmap
                s = _dot(kt, qm_ref[hc])
                m_old = m_ref[hc]
                if tile_bias is not None:
                    s = s + tile_bias(h)
                    m_new = jnp.maximum(m_old, jnp.max(s, axis=0, keepdims=True))
                    shift = m_new
                else:
                    c = const_bias(h)
                    m_new = jnp.maximum(m_old, jnp.max(s, axis=0, keepdims=True) + c)
                    shift = m_new - c
                p = jnp.exp2(s - shift).astype(BF16)
                acc_ref[hc] = acc_ref[hc] * jnp.exp2(m_old - m_new) + _dot(vt, p)
                m_ref[hc] = m_new

    def far(row):
        def body(kj, carry):
            update(kj, None, lambda h: sc_ref[row, h])
            return carry
        return body

    lax.fori_loop(0, jnp.maximum(qi - 1, 0), far(0), 0)
    for d in range(3):
        kj = qi + d - 1

        @pl.when((kj >= 0) & (kj < nkt))
        def _():
            update(kj, lambda h: bias_ref[d, h], None)
    lax.fori_loop(jnp.minimum(qi + 2, nkt), nkt, far(1), 0)

    lmb = sc_ref[2, 0]
    outs = []
    for h in range(BR_HEADS):
        a0 = acc_ref[2 * h]
        a1 = acc_ref[2 * h + 1]
        o0 = a0[:HEAD_DIM, :] / a0[HEAD_DIM:HEAD_DIM + 1, :]
        o1 = a1[:HEAD_DIM, :] / a1[HEAD_DIM:HEAD_DIM + 1, :]
        o = o0 - lmb * o1
        ms = jnp.mean(o * o, axis=0, keepdims=True)
        outs.append(o * lax.rsqrt(ms + EPS))
    ot = jnp.concatenate(outs, axis=0) * gcol_ref[...]
    o_ref[...] = ot.T.astype(BF16)


def _diff_attn_class(off, nseq, n, qt, k3, vt4, bias, sc, gcol):
    t = DF_T
    nkt = n // t
    qb = off // t
    sb = off // n
    return pl.pallas_call(
        functools.partial(_diff_body, nkt),
        grid=(nseq, nkt),
        in_specs=[pl.BlockSpec((BR_W, t), lambda s, i: (0, qb + s * nkt + i)),
                  pl.BlockSpec((nkt, t, BR_W), lambda s, i: (sb + s, 0, 0), pipeline_mode=pl.Buffered(1)),
                  pl.BlockSpec((nkt, BR_HEADS, DF_VROWS, t), lambda s, i: (sb + s, 0, 0, 0),
                               pipeline_mode=pl.Buffered(1)),
                  _const_spec(bias.shape),
                  pl.BlockSpec(memory_space=pltpu.SMEM),
                  _const_spec(gcol.shape)],
        out_specs=pl.BlockSpec((t, BR_W), lambda s, i: (s * nkt + i, 0)),
        out_shape=jax.ShapeDtypeStruct((nseq * n, BR_W), BF16),
        scratch_shapes=[pltpu.VMEM((2 * BR_HEADS, BR_W, t), BF16),
                        pltpu.VMEM((2 * BR_HEADS, 1, t), F32),
                        pltpu.VMEM((2 * BR_HEADS, DF_VROWS, t), F32)],
        compiler_params=_params(("parallel", "parallel")),
        name="diff_attn",
    )(qt, k3, vt4, bias, sc, gcol)


def _diff_attn(layout, q, k, v, bias, sc, gcol):
    t = q.shape[0]
    qt = q.T
    ones = jnp.ones((t, BR_HEADS, 1), BF16)
    zeros = jnp.zeros((t, BR_HEADS, DF_VROWS - HEAD_DIM - 1), BF16)
    vaug = jnp.concatenate([v.reshape(t, BR_HEADS, HEAD_DIM), ones, zeros], axis=-1)
    k3 = k.reshape(t // DF_T, DF_T, BR_W)
    vt4 = vaug.reshape(t // DF_T, DF_T, BR_HEADS, DF_VROWS).transpose(0, 2, 3, 1)
    outs = [_diff_attn_class(off, nseq, n, qt, k3, vt4, bias, sc, gcol) for off, nseq, n in layout.classes]
    return jnp.concatenate(outs, axis=0)


def _win_body(layout, q_ref, kp_ref, kc_ref, kn_ref, vp_ref, vc_ref, vn_ref, bias_ref, sink_ref, o_ref):
    blk = WIN_BLOCK
    i = pl.program_id(0)
    tok = i * blk
    start, length = layout.seq_bounds(tok)
    has_prev = tok > start
    has_next = tok + blk < start + length
    masks = _head_mask()
    qst = _stack_heads(q_ref[...], masks)
    kcat = jnp.concatenate([kp_ref[...], kc_ref[...], kn_ref[...]], axis=0)
    vcat = jnp.concatenate([vp_ref[...], vc_ref[...], vn_ref[...]], axis=0)
    col = lax.broadcasted_iota(jnp.int32, (1, 3 * blk), 1)
    dead = ((col < blk) & jnp.logical_not(has_prev)) | ((col >= 2 * blk) & jnp.logical_not(has_next))
    s = _dot_nt(qst, kcat) + bias_ref[...] + jnp.where(dead, NEG, 0.0)
    sink = sink_ref[...]
    m = jnp.maximum(jnp.max(s, axis=-1, keepdims=True), sink)
    p = jnp.exp2(s - m)
    den = jnp.sum(p, axis=-1, keepdims=True) + jnp.exp2(sink - m)
    pn = (p * (1.0 / den)).astype(BF16)
    o_ref[...] = _unstack_heads(_dot(pn, vcat), masks, blk).astype(BF16)


def _win_attn(layout, q, k, v, bias, sink):
    t = q.shape[0]
    blk = WIN_BLOCK
    nb = t // blk
    cur = pl.BlockSpec((blk, BR_W), lambda i: (i, 0))
    prev = pl.BlockSpec((blk, BR_W), lambda i: (jnp.maximum(i - 1, 0), 0))
    nxt = pl.BlockSpec((blk, BR_W), lambda i: (jnp.minimum(i + 1, nb - 1), 0))
    return pl.pallas_call(
        functools.partial(_win_body, layout),
        grid=(nb,),
        in_specs=[cur, prev, cur, nxt, prev, cur, nxt, _const_spec(bias.shape), _const_spec(sink.shape)],
        out_specs=cur,
        out_shape=jax.ShapeDtypeStruct((t, BR_W), BF16),
        compiler_params=_params(("parallel",)),
        name="win_attn",
    )(q, k, k, k, v, v, v, bias, sink)


def _na_body(rows, q_ref, k_ref, v_ref, bias_ref, o_ref):
    j = pl.program_id(1)
    masks = _head_mask()
    nk = NA_KH * GRID_W
    for i in range(NA_RB):
        r = j * NA_RB + i
        rs = jnp.clip(r - NA_KH // 2, 0, rows - NA_KH)
        koff = pl.multiple_of(rs * GRID_W, GRID_W)
        q = q_ref[i * GRID_W:(i + 1) * GRID_W, :]
        ks = k_ref[pl.ds(koff, nk), :]
        vs = v_ref[pl.ds(koff, nk), :]
        s = _dot_nt(_stack_heads(q, masks), ks) + bias_ref[r - rs]
        m = jnp.max(s, axis=-1, keepdims=True)
        p = jnp.exp2(s - m)
        pn = (p * (1.0 / jnp.sum(p, axis=-1, keepdims=True))).astype(BF16)
        o_ref[i * GRID_W:(i + 1) * GRID_W, :] = _unstack_heads(_dot(pn, vs), masks, GRID_W).astype(BF16)


def _na_class(off, nseq, n, q, k, v, bias):
    rows = n // GRID_W
    qt = NA_RB * GRID_W
    nj = n // qt
    qb = off // qt
    sb = off // n
    seq = pl.BlockSpec((n, BR_W), lambda s, j: (sb + s, 0), pipeline_mode=pl.Buffered(1))
    return pl.pallas_call(
        functools.partial(_na_body, rows),
        grid=(nseq, nj),
        in_specs=[pl.BlockSpec((qt, BR_W), lambda s, j: (qb + s * nj + j, 0)), seq, seq, _const_spec(bias.shape)],
        out_specs=pl.BlockSpec((qt, BR_W), lambda s, j: (s * nj + j, 0)),
        out_shape=jax.ShapeDtypeStruct((nseq * n, BR_W), BF16),
        compiler_params=_params(("parallel", "parallel")),
        name="na_attn",
    )(q, k, v, bias)


def _na_attn(layout, q, k, v, bias):
    return jnp.concatenate([_na_class(off, nseq, n, q, k, v, bias) for off, nseq, n in layout.classes], axis=0)


def _mem_body(q_ref, mk_ref, mv_ref, o_ref):
    masks = _head_mask()
    s = _dot_nt(_stack_heads(q_ref[...], masks), mk_ref[0])
    m = jnp.max(s, axis=-1, keepdims=True)
    p = jnp.exp2(s - m)
    pn = (p * (1.0 / jnp.sum(p, axis=-1, keepdims=True))).astype(BF16)
    o_ref[...] = _unstack_heads(_dot(pn, mv_ref[0]), masks, MEM_TM).astype(BF16)


def _mem_attn(layout, q, mk, mv):
    t = q.shape[0]
    tm = MEM_TM
    mem_len = mk.shape[1]
    tile = pl.BlockSpec((tm, BR_W), lambda i: (i, 0))
    mem = pl.BlockSpec((1, mem_len, BR_W), lambda i: (layout.seq_index(i * tm), 0, 0))
    return pl.pallas_call(
        _mem_body,
        grid=(t // tm,),
        in_specs=[tile, mem, mem],
        out_specs=tile,
        out_shape=jax.ShapeDtypeStruct((t, BR_W), BF16),
        compiler_params=_params(("parallel",)),
        name="mem_attn",
    )(q, mk, mv)


def _merge_body(x_ref, ng_ref, oa_ref, ob_ref, oc_ref, od_ref, oe_ref, wg_ref, wm_ref, wb_ref, wo_ref, y_ref):
    x = x_ref[...]
    h = _rms_rows(x, ng_ref[...]).astype(BF16)
    merged = jnp.zeros((x.shape[0], D_MODEL), F32)
    for kb, o_ref in enumerate((oa_ref, ob_ref, oc_ref, od_ref, oe_ref)):
        g = _dot(h, wg_ref[:, kb * BR_W:(kb + 1) * BR_W])
        br = (o_ref[...].astype(F32) * (g * jax.nn.sigmoid(g))).astype(BF16)
        mg = jax.nn.sigmoid(_dot(h, wm_ref[:, kb * D_MODEL:(kb + 1) * D_MODEL]))
        merged = merged + mg * _dot(br, wb_ref[kb])
    y_ref[...] = x + _dot(merged.astype(BF16), wo_ref[...])


def _merge(x, ng, branches, wg, wm, wb, wo):
    t = x.shape[0]
    tm = MERGE_TM
    xt = pl.BlockSpec((tm, D_MODEL), lambda i: (i, 0))
    bt = pl.BlockSpec((tm, BR_W), lambda i: (i, 0))
    return pl.pallas_call(
        _merge_body,
        grid=(t // tm,),
        in_specs=[xt, _const_spec((1, D_MODEL))] + [bt] * N_BRANCH
        + [_const_spec(wg.shape), _const_spec(wm.shape), _const_spec(wb.shape), _const_spec(wo.shape)],
        out_specs=xt,
        out_shape=jax.ShapeDtypeStruct((t, D_MODEL), F32),
        compiler_params=_params(("parallel",)),
        name="merge",
    )(x, ng, *branches, wg, wm, wb, wo)


def _t5_bucket(rel):
    half = N_BUCKETS // 2
    exact = half // 2
    n = jnp.abs(rel)
    nf = jnp.maximum(n, 1).astype(F32)
    large = exact + (jnp.log(nf / exact) / math.log(MAX_DIST / exact) * (half - exact)).astype(jnp.int32)
    large = jnp.clip(large, 0, half - 1)
    return jnp.where(rel > 0, half, 0) + jnp.where(n < exact, n, large)


def _group_matrix(group):
    return jnp.asarray(np.kron(np.eye(BR_W // group), np.full((group, group), 1.0 / group)), BF16)


def _diff_bias_tables(rel_bias):
    t = DF_T
    table = rel_bias[:, :BR_HEADS].astype(F32) * LOG2E
    kl = jnp.arange(t)[:, None]
    ql = jnp.arange(t)[None, :]
    tiles = [table[_t5_bucket(kl - ql + d * t)].transpose(2, 0, 1) for d in (-1, 0, 1)]
    far = jnp.stack([table[_t5_bucket(jnp.int32(-2 * t))], table[_t5_bucket(jnp.int32(2 * t))]])
    return jnp.stack(tiles), far


def _win_bias_table(rel_bias):
    rel = jnp.arange(3 * WIN_BLOCK)[None, :] - WIN_BLOCK - jnp.arange(WIN_BLOCK)[:, None]
    bias = rel_bias[:, BR_HEADS:].astype(F32)[_t5_bucket(rel)].transpose(2, 0, 1) * LOG2E
    bias = jnp.where((jnp.abs(rel) <= WIN)[None], bias, NEG)
    return bias.reshape(BR_HEADS * WIN_BLOCK, 3 * WIN_BLOCK)


def _na_bias_table(rpb):
    col = jnp.arange(GRID_W)
    cs = jnp.clip(col - NA_KW // 2, 0, GRID_W - NA_KW)
    kc = jnp.arange(GRID_W)
    inwin = (kc[None, :] >= cs[:, None]) & (kc[None, :] < cs[:, None] + NA_KW)
    dc = jnp.clip(kc[None, :] - col[:, None] + (NA_KW - 1), 0, 2 * NA_KW - 2)
    i = jnp.arange(NA_KH)
    out = []
    for d in range(NA_KH):
        dr = i - d + (NA_KH - 1)
        b = rpb.astype(F32)[:, dr[:, None, None], dc[None]] * LOG2E
        b = jnp.where(inwin[None, None], b, NEG).transpose(0, 2, 1, 3)
        out.append(b.reshape(BR_HEADS * GRID_W, NA_KH * GRID_W))
    return jnp.stack(out)


def _tile_gain(g, reps, scale=1.0):
    return jnp.tile(g.astype(F32), reps)[None, :] * scale


def kernel(x_prompt, x_sample, mem_prompt, mem_sample, norm_g, mem_norm_g, w_in, w_mem_kv, rel_bias, hgrn_lb,
           hgrn_norm_g, diff_qk_g, diff_lambda, diff_subln_g, win_qk_g, win_sink, na_qk_g, na_rpb, mem_qk_g,
           w_branch, w_out):
    depth = w_in.shape[0]
    bp, lp, _ = x_prompt.shape
    bs, ls, _ = x_sample.shape
    mem_len = mem_prompt.shape[1]
    layout = _Layout(bp, lp, bs, ls)
    t = layout.total
    assert t % PROJ_TM == 0 and lp % DF_T == 0 and ls % DF_T == 0
    assert lp % (NA_RB * GRID_W) == 0 and ls % (NA_RB * GRID_W) == 0 and mem_len % MEM_TM == 0

    x = jnp.concatenate([x_prompt.reshape(bp * lp, D_MODEL), x_sample.reshape(bs * ls, D_MODEL)], axis=0)
    mem = jnp.concatenate([mem_prompt.reshape(bp * mem_len, D_MODEL), mem_sample.reshape(bs * mem_len, D_MODEL)], axis=0)

    sm = jax.nn.softmax(hgrn_lb.astype(F32), axis=1)
    lb_all = jnp.clip(jnp.cumsum(sm, axis=1) - sm[:, :1], 0.0, 1.0 - 1e-6)
    lam_init = jnp.asarray([0.8 - 0.6 * math.exp(-0.3 * l) for l in range(depth)], F32)
    lam = diff_lambda.astype(F32)
    lmb = jnp.exp(jnp.sum(lam[:, 0] * lam[:, 1], axis=-1)) - jnp.exp(jnp.sum(lam[:, 2] * lam[:, 3], axis=-1)) + lam_init

    w_in_b = w_in.astype(BF16)

    def expand_kv(w):
        w = w.reshape(depth, D_MODEL, WIN_KV_HEADS, HEAD_DIM)
        return jnp.repeat(w, BR_HEADS // WIN_KV_HEADS, axis=2).reshape(depth, D_MODEL, BR_W)

    w1 = jnp.concatenate([w_in_b[:, :, A_Q:A_G], w_in_b[:, :, B_Q:B_G], w_in_b[:, :, C_Q:C_K],
                          expand_kv(w_in_b[:, :, C_K:C_V]), expand_kv(w_in_b[:, :, C_V:C_G]),
                          w_in_b[:, :, D_Q:D_G], w_in_b[:, :, E_Q:E_G]], axis=-1)
    wg = jnp.concatenate([w_in_b[:, :, A_G:B_Q], w_in_b[:, :, B_G:C_Q], w_in_b[:, :, C_G:D_Q],
                          w_in_b[:, :, D_G:E_Q], w_in_b[:, :, E_G:M_G]], axis=-1)
    wm = w_in_b[:, :, M_G:]
    wb = w_branch.astype(BF16)
    wo = w_out.astype(BF16)
    wmem = w_mem_kv.astype(BF16)

    sc_b = DF_DK ** -0.5 * LOG2E
    sc_h = HEAD_DIM ** -0.5 * LOG2E
    zrow = jnp.zeros((depth, 1, BR_W), F32)

    def per_layer(fn):
        return jnp.stack([fn(l) for l in range(depth)])

    ep = jnp.concatenate([
        per_layer(lambda l: _tile_gain(diff_qk_g[l, 0], 8, sc_b)),
        per_layer(lambda l: _tile_gain(diff_qk_g[l, 1], 8)),
        per_layer(lambda l: _tile_gain(win_qk_g[l, 0], 4, sc_h)),
        per_layer(lambda l: _tile_gain(win_qk_g[l, 1], 4)),
        per_layer(lambda l: _tile_gain(na_qk_g[l, 0], 4, sc_h)),
        per_layer(lambda l: _tile_gain(na_qk_g[l, 1], 4)),
        per_layer(lambda l: _tile_gain(mem_qk_g[l, 0], 4, sc_h)),
        lb_all[0][:, None, :], lb_all[1][:, None, :]] + [zrow] * 7, axis=1)
    gk_mem = per_layer(lambda l: _tile_gain(mem_qk_g[l, 1], 4))
    gn_hg = hgrn_norm_g.astype(F32)[:, None, :]
    gcol = per_layer(lambda l: (jnp.tile(diff_subln_g[l].astype(F32), BR_HEADS) * (1.0 - lam_init[l]))[:, None])
    sink = per_layer(lambda l: jnp.repeat(win_sink[l].astype(F32) * LOG2E, WIN_BLOCK)[:, None])
    na_bias = per_layer(lambda l: _na_bias_table(na_rpb[l]))

    diff_bias, diff_far = _diff_bias_tables(rel_bias)
    win_bias = _win_bias_table(rel_bias)
    sc = jnp.concatenate([jnp.broadcast_to(diff_far[None], (depth, 2, BR_HEADS)),
                          jnp.broadcast_to(lmb[:, None, None], (depth, 1, BR_HEADS))], axis=1)

    g32 = _group_matrix(DF_DK)
    g64 = _group_matrix(HEAD_DIM)
    consts_f = tuple(jnp.asarray(a, dt) for a, dt in zip(_hgrn_constants(HG_C, False), (BF16, F32, F32)))
    consts_b = tuple(jnp.asarray(a, dt) for a, dt in zip(_hgrn_constants(HG_C, True), (BF16, F32, F32)))

    def layer(x, p):
        ng = p["ng"]
        (a_q, a_v, a_lf, a_kk, b_q, b_k, b_v, c_q, c_k, c_v, d_q, d_k, d_v, e_q) = _project(
            x, ng, p["w1"], p["ep"], g32, g64)
        mk, mv = _mem_kv(mem, p["mng"], p["wmem"], p["gk_mem"], g64)
        o_a = _hgrn(layout, a_q, a_v, a_kk, a_lf, consts_f, consts_b, g64, p["gn_hg"])
        o_b = _diff_attn(layout, b_q, b_k, b_v, diff_bias, p["sc"], p["gcol"])
        o_c = _win_attn(layout, c_q, c_k, c_v, win_bias, p["sink"])
        o_d = _na_attn(layout, d_q, d_k, d_v, p["na_bias"])
        o_e = _mem_attn(layout, e_q, mk.reshape(layout.nseq, mem_len, BR_W), mv.reshape(layout.nseq, mem_len, BR_W))
        y = _merge(x, ng, (o_a, o_b, o_c, o_d, o_e), p["wg"], p["wm"], p["wb"], p["wo"])
        return y, None

    params = dict(ng=norm_g.astype(F32)[:, None, :], mng=mem_norm_g.astype(F32)[:, None, :], w1=w1, ep=ep,
                  wmem=wmem, gk_mem=gk_mem, gn_hg=gn_hg, sc=sc, gcol=gcol, sink=sink, na_bias=na_bias,
                  wg=wg, wm=wm, wb=wb, wo=wo)
    x, _ = lax.scan(layer, x, params)
    y_prompt = x[:layout.off1].reshape(bp, lp, D_MODEL)
    y_sample = x[layout.off1:].reshape(bs, ls, D_MODEL)
    return (y_prompt, y_sample)
```

```python
import functools
import math

import numpy as np
import jax
import jax.numpy as jnp
from jax import lax
from jax.experimental import pallas as pl
from jax.experimental.pallas import tpu as pltpu

F32 = jnp.float32
BF16 = jnp.bfloat16

D_MODEL = 1024
HEAD_DIM = 64
BR_HEADS = 4
BR_W = BR_HEADS * HEAD_DIM
N_BRANCH = 5
DF_DK = HEAD_DIM // 2
WIN = 128
WIN_BLOCK = 128
WIN_KV_HEADS = 2
GRID_W = 64
NA_KH = 8
NA_KW = 16
N_BUCKETS = 32
MAX_DIST = 128
EPS = 1e-6
NEG = -1e30
LB_FLOOR = 1e-30
LOG2E = 1.4426950408889634

A_Q = 0
A_G = 4 * BR_W
B_Q = A_G + BR_W
B_G = B_Q + 3 * BR_W
C_Q = B_G + BR_W
C_K = C_Q + BR_W
C_V = C_K + WIN_KV_HEADS * HEAD_DIM
C_G = C_V + WIN_KV_HEADS * HEAD_DIM
D_Q = C_G + BR_W
D_G = D_Q + 3 * BR_W
E_Q = D_G + BR_W
E_G = E_Q + BR_W
M_G = E_G + BR_W

VMEM_LIMIT_BYTES = 56 * 1024 * 1024

PROJ_TM = 512
MERGE_TM = 256
MEM_TM = 256
HG_C = 128
DF_T = 256
NA_RB = 8
WIN_NB = 4
N_PROJ_SEG = 14
DF_VROWS = 80
DF_RC = 64


def _params(sem):
    return pltpu.CompilerParams(dimension_semantics=sem, vmem_limit_bytes=VMEM_LIMIT_BYTES)


def _const_spec(shape):
    nd = len(shape)
    return pl.BlockSpec(shape, lambda *_: (0,) * nd, pipeline_mode=pl.Buffered(1))


def _dot(a, b):
    return jnp.dot(a, b, preferred_element_type=F32)


def _dot_nt(a, b):
    return lax.dot_general(a, b, (((1,), (1,)), ((), ())), preferred_element_type=F32)


def _rms_rows(x, g):
    ms = jnp.mean(x * x, axis=-1, keepdims=True)
    return x * lax.rsqrt(ms + EPS) * g


def _group_rms(x, gmat):
    x2 = x * x
    hi = x2.astype(BF16)
    lo = (x2 - hi.astype(F32)).astype(BF16)
    ms = _dot(hi, gmat) + _dot(lo, gmat)
    return x * lax.rsqrt(ms + EPS)


def _head_mask(width=BR_W):
    lane = lax.broadcasted_iota(jnp.int32, (1, width), 1)
    return [(lane >= h * HEAD_DIM) & (lane < (h + 1) * HEAD_DIM) for h in range(BR_HEADS)]


def _stack_heads(q, masks):
    zero = jnp.zeros_like(q)
    return jnp.concatenate([jnp.where(m, q, zero) for m in masks], axis=0)


def _unstack_heads(o_all, masks, m):
    out = jnp.zeros((m, BR_W), F32)
    for h, mk in enumerate(masks):
        out = out + jnp.where(mk, o_all[h * m:(h + 1) * m, :], 0.0)
    return out


class _Layout:
    def __init__(self, n_prompt, len_prompt, n_sample, len_sample):
        self.classes = ((0, n_prompt, len_prompt), (n_prompt * len_prompt, n_sample, len_sample))
        self.off1 = n_prompt * len_prompt
        self.lp = len_prompt
        self.ls = len_sample
        self.total = self.off1 + n_sample * len_sample
        self.nseq = n_prompt + n_sample
        self.n_prompt = n_prompt
        assert self.off1 % len_sample == 0

    def seq_bounds(self, tok):
        in_p = tok < self.off1
        start_p = (tok // self.lp) * self.lp
        start_s = self.off1 + ((tok - self.off1) // self.ls) * self.ls
        return jnp.where(in_p, start_p, start_s), jnp.where(in_p, self.lp, self.ls)

    def seq_index(self, tok):
        return jnp.where(tok < self.off1, tok // self.lp, self.n_prompt + (tok - self.off1) // self.ls)


def _proj_body(x_ref, ng_ref, w_ref, ep_ref, g32_ref, g64_ref,
               aq_ref, av_ref, alf_ref, akk_ref, bq_ref, bk_ref, bv_ref,
               cq_ref, ck_ref, cv_ref, dq_ref, dk_ref, dv_ref, eq_ref):
    h = _rms_rows(x_ref[...], ng_ref[...]).astype(BF16)
    ep = ep_ref[...]

    def seg(i):
        return _dot(h, w_ref[:, i * BR_W:(i + 1) * BR_W])

    def row(r):
        return ep[r:r + 1, :]

    aq = seg(0)
    aq_ref[...] = (aq * jax.nn.sigmoid(aq)).astype(BF16)
    av_ref[...] = seg(1).astype(BF16)
    for d in range(2):
        z = seg(2 + d)
        lb = row(7 + d)
        e = jnp.exp(-jnp.abs(z))
        log_sig = jnp.minimum(z, 0.0) - jnp.log1p(e)
        t1 = jnp.log(jnp.maximum(lb, LB_FLOOR))
        t2 = jnp.log1p(-lb) + log_sig
        logf = jnp.maximum(t1, t2) + jnp.log1p(jnp.exp(-jnp.abs(t1 - t2)))
        sig_neg = jnp.where(z >= 0.0, e, 1.0) / (1.0 + e)
        alf_ref[:, d * BR_W:(d + 1) * BR_W] = logf
        akk_ref[:, d * BR_W:(d + 1) * BR_W] = ((1.0 - lb) * sig_neg).astype(BF16)
    g32 = g32_ref[...]
    g64 = g64_ref[...]
    bq_ref[...] = (_group_rms(seg(4), g32) * row(0)).astype(BF16)
    bk_ref[...] = (_group_rms(seg(5), g32) * row(1)).astype(BF16)
    bv_ref[...] = seg(6).astype(BF16)
    cq_ref[...] = (_group_rms(seg(7), g64) * row(2)).astype(BF16)
    ck_ref[...] = (_group_rms(seg(8), g64) * row(3)).astype(BF16)
    cv_ref[...] = seg(9).astype(BF16)
    dq_ref[...] = (_group_rms(seg(10), g64) * row(4)).astype(BF16)
    dk_ref[...] = (_group_rms(seg(11), g64) * row(5)).astype(BF16)
    dv_ref[...] = seg(12).astype(BF16)
    eq_ref[...] = (_group_rms(seg(13), g64) * row(6)).astype(BF16)


def _project(x, ng, w1, ep, g32, g64):
    t = x.shape[0]
    tm = PROJ_TM
    tile = lambda w: pl.BlockSpec((tm, w), lambda i: (i, 0))
    widths = [BR_W, BR_W, 2 * BR_W, 2 * BR_W] + [BR_W] * 10
    dtypes = [BF16, BF16, F32, BF16] + [BF16] * 10
    return pl.pallas_call(
        _proj_body,
        grid=(t // tm,),
        in_specs=[tile(D_MODEL), _const_spec((1, D_MODEL)), _const_spec(w1.shape), _const_spec(ep.shape),
                  _const_spec(g32.shape), _const_spec(g64.shape)],
        out_specs=[tile(w) for w in widths],
        out_shape=[jax.ShapeDtypeStruct((t, w), dt) for w, dt in zip(widths, dtypes)],
        compiler_params=_params(("parallel",)),
        name="proj",
    )(x, ng, w1, ep, g32, g64)


def _memkv_body(m_ref, g_ref, w_ref, gk_ref, g64_ref, mk_ref, mv_ref):
    mh = _rms_rows(m_ref[...], g_ref[...]).astype(BF16)
    kv = _dot(mh, w_ref[...])
    mk_ref[...] = (_group_rms(kv[:, :BR_W], g64_ref[...]) * gk_ref[...]).astype(BF16)
    mv_ref[...] = kv[:, BR_W:].astype(BF16)


def _mem_kv(mem, g, w, gk, g64):
    t = mem.shape[0]
    tm = 256
    return pl.pallas_call(
        _memkv_body,
        grid=(t // tm,),
        in_specs=[pl.BlockSpec((tm, D_MODEL), lambda i: (i, 0)), _const_spec((1, D_MODEL)), _const_spec(w.shape),
                  _const_spec((1, BR_W)), _const_spec(g64.shape)],
        out_specs=[pl.BlockSpec((tm, BR_W), lambda i: (i, 0))] * 2,
        out_shape=[jax.ShapeDtypeStruct((t, BR_W), BF16)] * 2,
        compiler_params=_params(("parallel",)),
        name="memkv",
    )(mem, g, w, gk, g64)


def _hgrn_constants(c, reverse):
    nl = int(math.log2(c))
    idx = np.arange(c)
    t = idx[:, None]
    u = idx[None, :]
    incl = (u <= t).astype(np.float32)
    rest = (u > t).astype(np.float32)
    tot = np.ones((8, c), np.float32)
    mds, mes, lms = [], [], [np.eye(c, dtype=np.float32)]
    for lev in range(nl):
        w = 1 << lev
        blk = idx // w
        odd = (blk % 2 == 1)
        md = (odd[:, None] & (u >= (blk * w)[:, None]) & (u <= t)).astype(np.float32)
        me = ((~odd)[:, None] & (u > t) & (u <= ((blk + 1) * w - 1)[:, None])).astype(np.float32)
        lm = (odd[:, None] & (blk[None, :] == (blk - 1)[:, None])).astype(np.float32)
        mds.append(md)
        mes.append(me)
        lms.append(lm)
    mats = [incl, rest, tot] + mds + mes
    if reverse:
        mats = [m[::-1, ::-1] for m in mats]
        lms = [m[::-1, ::-1] for m in lms]
    mall = np.concatenate(mats, axis=0)
    lmst = np.stack([np.tile(m, (BR_HEADS, 1)) for m in lms])
    bd = np.kron(np.eye(BR_HEADS, dtype=np.float32), np.ones((HEAD_DIM, HEAD_DIM), np.float32))
    return mall, lmst, bd


def _hgrn_body(layout, nct, reverse, final, q_ref, k_ref, v_ref, lf_ref, mall_ref, lm_ref, bd_ref, *rest):
    if final:
        of_ref, g64_ref, gn_ref, o_ref, st_ref = rest
    else:
        o_ref, st_ref = rest
    c = HG_C
    nl = int(math.log2(c))
    i = pl.program_id(0)
    chunk = (nct - 1 - i) if reverse else i
    tok = chunk * c
    start, length = layout.seq_bounds(tok)
    fresh = (tok + c == start + length) if reverse else (tok == start)

    @pl.when(fresh)
    def _():
        st_ref[...] = jnp.zeros_like(st_ref)

    masks = _head_mask()
    lf = lf_ref[...]
    hi = lf.astype(BF16)
    mid = (lf - hi.astype(F32)).astype(BF16)
    mall = mall_ref[...]
    cums = _dot(mall, hi) + _dot(mall, mid)

    def blockrows(j):
        return cums[j * c:(j + 1) * c, :]

    b = blockrows(0)
    b_rest = blockrows(1)
    b_tot = cums[2 * c:2 * c + 1, :]
    base = 2 * c + 8

    qf = q_ref[...].astype(F32)
    kf = k_ref[...].astype(F32)
    v = v_ref[...]
    st = st_ref[...]

    o = _dot_nt((qf * jnp.exp(b)).astype(BF16), st.astype(BF16))
    a_all = _dot_nt(_stack_heads(q_ref[...], masks), k_ref[...]) * lm_ref[0]
    for lev in range(nl):
        dq = cums[base + lev * c:base + (lev + 1) * c, :]
        dk = cums[base + (nl + lev) * c:base + (nl + lev + 1) * c, :]
        ql = (qf * jnp.exp(dq)).astype(BF16)
        kl = (kf * jnp.exp(dk)).astype(BF16)
        a_all = a_all + _dot_nt(_stack_heads(ql, masks), kl) * lm_ref[lev + 1]
    a_bf = a_all.astype(BF16)
    zero = jnp.zeros_like(v)
    for h, mk in enumerate(masks):
        o = o + _dot(a_bf[h * c:(h + 1) * c, :], jnp.where(mk, v, zero))

    kst = (kf * jnp.exp(b_rest)).astype(BF16)
    vt = v.astype(F32).T.astype(BF16)
    st_ref[...] = (st * jnp.exp(b_tot) + _dot(vt, kst)) * bd_ref[...]

    if final:
        tot = of_ref[...] + o
        o_ref[...] = (_group_rms(tot, g64_ref[...]) * gn_ref[...]).astype(BF16)
    else:
        o_ref[...] = o


def _hgrn(layout, qs, v, kk, logf, consts_f, consts_b, g64, gn):
    t = qs.shape[0]
    c = HG_C
    nct = t // c

    def run(reverse, consts, extra_in, extra_specs, out_dtype):
        mall, lmst, bd = consts
        cm = (lambda i: nct - 1 - i) if reverse else (lambda i: i)
        d = 1 if reverse else 0
        tile = pl.BlockSpec((c, BR_W), lambda i: (cm(i), 0))
        half = pl.BlockSpec((c, BR_W), lambda i: (cm(i), d))
        return pl.pallas_call(
            functools.partial(_hgrn_body, layout, nct, reverse, reverse),
            grid=(nct,),
            in_specs=[tile, half, tile, half, _const_spec(mall.shape), _const_spec(lmst.shape),
                      _const_spec(bd.shape)] + extra_specs(tile),
            out_specs=tile,
            out_shape=jax.ShapeDtypeStruct((t, BR_W), out_dtype),
            scratch_shapes=[pltpu.VMEM((BR_W, BR_W), F32)],
            compiler_params=_params(("arbitrary",)),
            name="hgrn_bwd" if reverse else "hgrn_fwd",
        )(qs, kk, v, logf, mall, lmst, bd, *extra_in)

    o_f = run(False, consts_f, [], lambda tile: [], F32)
    return run(True, consts_b, [o_f, g64, gn],
               lambda tile: [tile, _const_spec(g64.shape), _const_spec((1, BR_W))], BF16)


def _diff_body(nkt, qt_ref, k_ref, vt_ref, bias_ref, sc_ref, gcol_ref, o_ref, qm_ref, m_ref, acc_ref,
               s0_ref, s1_ref, p_ref, al_ref):
    tq = DF_T
    qi = pl.program_id(1)
    qt = qt_ref[...]
    rowid = lax.broadcasted_iota(jnp.int32, (BR_W, tq), 0)
    for hc in range(2 * BR_HEADS):
        sel = (rowid >= hc * DF_DK) & (rowid < (hc + 1) * DF_DK)
        qm_ref[hc] = jnp.where(sel, qt, jnp.zeros_like(qt))
    m_ref[...] = jnp.full(m_ref.shape, NEG, F32)
    acc_ref[...] = jnp.zeros_like(acc_ref)

    nu = 2 * BR_HEADS
    rows = [(r, r + DF_RC) for r in range(0, tq, DF_RC)]

    def qk(kj, sbuf):
        kt = k_ref[kj]
        for hc in range(nu):
            sbuf[hc] = _dot(kt, qm_ref[hc])

    def softmax_pv(kj, sbuf, tile_bias, const_bias):
        def scores(hc, r0, r1):
            s = sbuf[hc, r0:r1, :]
            return s if tile_bias is None else s + tile_bias(hc // 2, r0, r1)

        for hc in range(nu):
            mx = None
            for r0, r1 in rows:
                part = jnp.max(scores(hc, r0, r1).reshape(DF_RC // 8, 8, tq), axis=0)
                mx = part if mx is None else jnp.maximum(mx, part)
            smax = jnp.max(mx, axis=0, keepdims=True)
            m_old = m_ref[hc]
            if tile_bias is not None:
                m_new = jnp.maximum(m_old, smax)
                shift = m_new
            else:
                c = const_bias(hc // 2)
                m_new = jnp.maximum(m_old, smax + c)
                shift = m_new - c
            for r0, r1 in rows:
                p_ref[hc, r0:r1, :] = jnp.exp2(scores(hc, r0, r1) - shift).astype(BF16)
            al_ref[hc] = jnp.exp2(m_old - m_new)
            m_ref[hc] = m_new
        for hc in range(nu):
            acc_ref[hc] = acc_ref[hc] * al_ref[hc] + _dot(vt_ref[kj, hc // 2], p_ref[hc])

    def far_loop(lo, hi, row):
        last = nkt - 1

        @pl.when(hi > lo)
        def _():
            qk(jnp.minimum(lo, last), s0_ref)

        def body(i, carry):
            kj = lo + 2 * i
            live = kj + 1 < hi
            qk(jnp.minimum(kj + 1, last), s1_ref)
            softmax_pv(kj, s0_ref, None, lambda h: sc_ref[row, h])
            qk(jnp.minimum(kj + 2, last), s0_ref)
            softmax_pv(jnp.minimum(kj + 1, last), s1_ref, None,
                       lambda h: jnp.where(live, sc_ref[row, h], 2.0 * NEG))
            return carry
        lax.fori_loop(0, (hi - lo + 1) // 2, body, 0)

    far_loop(0, jnp.maximum(qi - 1, 0), 0)
    for d in range(3):
        kj = qi + d - 1

        @pl.when((kj >= 0) & (kj < nkt))
        def _():
            qk(kj, s0_ref)
            softmax_pv(kj, s0_ref, lambda h, r0, r1: bias_ref[d, h, r0:r1, :], None)
    far_loop(jnp.minimum(qi + 2, nkt), nkt, 1)

    lmb = sc_ref[2, 0]
    outs = []
    for h in range(BR_HEADS):
        a0 = acc_ref[2 * h]
        a1 = acc_ref[2 * h + 1]
        o0 = a0[:HEAD_DIM, :] / a0[HEAD_DIM:HEAD_DIM + 1, :]
        o1 = a1[:HEAD_DIM, :] / a1[HEAD_DIM:HEAD_DIM + 1, :]
        o = o0 - lmb * o1
        ms = jnp.mean(o * o, axis=0, keepdims=True)
        outs.append(o * lax.rsqrt(ms + EPS))
    ot = jnp.concatenate(outs, axis=0) * gcol_ref[...]
    o_ref[...] = ot.T.astype(BF16)


def _diff_attn_class(off, nseq, n, qt, k3, vt4, bias, sc, gcol):
    t = DF_T
    nkt = n // t
    qb = off // t
    sb = off // n
    return pl.pallas_call(
        functools.partial(_diff_body, nkt),
        grid=(nseq, nkt),
        in_specs=[pl.BlockSpec((BR_W, t), lambda s, i: (0, qb + s * nkt + i)),
                  pl.BlockSpec((nkt, t, BR_W), lambda s, i: (sb + s, 0, 0), pipeline_mode=pl.Buffered(1)),
                  pl.BlockSpec((nkt, BR_HEADS, DF_VROWS, t), lambda s, i: (sb + s, 0, 0, 0),
                               pipeline_mode=pl.Buffered(1)),
                  _const_spec(bias.shape),
                  pl.BlockSpec(memory_space=pltpu.SMEM),
                  _const_spec(gcol.shape)],
        out_specs=pl.BlockSpec((t, BR_W), lambda s, i: (s * nkt + i, 0)),
        out_shape=jax.ShapeDtypeStruct((nseq * n, BR_W), BF16),
        scratch_shapes=[pltpu.VMEM((2 * BR_HEADS, BR_W, t), BF16),
                        pltpu.VMEM((2 * BR_HEADS, 1, t), F32),
                        pltpu.VMEM((2 * BR_HEADS, DF_VROWS, t), F32),
                        pltpu.VMEM((2 * BR_HEADS, t, t), F32),
                        pltpu.VMEM((2 * BR_HEADS, t, t), F32),
                        pltpu.VMEM((2 * BR_HEADS, t, t), BF16),
                        pltpu.VMEM((2 * BR_HEADS, 1, t), F32)],
        compiler_params=_params(("parallel", "parallel")),
        name="diff_attn",
    )(qt, k3, vt4, bias, sc, gcol)


def _diff_attn(layout, q, k, v, bias, sc, gcol):
    t = q.shape[0]
    qt = q.T
    ones = jnp.ones((t, BR_HEADS, 1), BF16)
    zeros = jnp.zeros((t, BR_HEADS, DF_VROWS - HEAD_DIM - 1), BF16)
    vaug = jnp.concatenate([v.reshape(t, BR_HEADS, HEAD_DIM), ones, zeros], axis=-1)
    k3 = k.reshape(t // DF_T, DF_T, BR_W)
    vt4 = vaug.reshape(t // DF_T, DF_T, BR_HEADS, DF_VROWS).transpose(0, 2, 3, 1)
    outs = [_diff_attn_class(off, nseq, n, qt, k3, vt4, bias, sc, gcol) for off, nseq, n in layout.classes]
    return jnp.concatenate(outs, axis=0)


def _win_body(layout, q_ref, kp_ref, kc_ref, kn_ref, vp_ref, vc_ref, vn_ref, bias_ref, sink_ref, o_ref):
    blk = WIN_BLOCK
    i = pl.program_id(0)
    tok = i * (WIN_NB * blk)
    start, length = layout.seq_bounds(tok)
    masks = _head_mask()
    kcat = jnp.concatenate([kp_ref[...], kc_ref[...], kn_ref[...]], axis=0)
    vcat = jnp.concatenate([vp_ref[...], vc_ref[...], vn_ref[...]], axis=0)
    col = lax.broadcasted_iota(jnp.int32, (1, 3 * blk), 1)
    sink = sink_ref[...]
    bias = bias_ref[...]
    for j in range(WIN_NB):
        has_prev = tok + j * blk > start
        has_next = tok + (j + 1) * blk < start + length
        dead = ((col < blk) & jnp.logical_not(has_prev)) | ((col >= 2 * blk) & jnp.logical_not(has_next))
        qst = _stack_heads(q_ref[j * blk:(j + 1) * blk, :], masks)
        s = _dot_nt(qst, kcat[j * blk:(j + 3) * blk, :]) + bias + jnp.where(dead, NEG, 0.0)
        m = jnp.maximum(jnp.max(s, axis=-1, keepdims=True), sink)
        p = jnp.exp2(s - m)
        den = jnp.sum(p, axis=-1, keepdims=True) + jnp.exp2(sink - m)
        pn = (p * (1.0 / den)).astype(BF16)
        o_all = _dot(pn, vcat[j * blk:(j + 3) * blk, :])
        o_ref[j * blk:(j + 1) * blk, :] = _unstack_heads(o_all, masks, blk).astype(BF16)


def _win_attn(layout, q, k, v, bias, sink):
    t = q.shape[0]
    blk = WIN_BLOCK
    nb = t // blk
    cur = pl.BlockSpec((WIN_NB * blk, BR_W), lambda i: (i, 0))
    prev = pl.BlockSpec((blk, BR_W), lambda i: (jnp.maximum(i * WIN_NB - 1, 0), 0))
    nxt = pl.BlockSpec((blk, BR_W), lambda i: (jnp.minimum((i + 1) * WIN_NB, nb - 1), 0))
    return pl.pallas_call(
        functools.partial(_win_body, layout),
        grid=(nb // WIN_NB,),
        in_specs=[cur, prev, cur, nxt, prev, cur, nxt, _const_spec(bias.shape), _const_spec(sink.shape)],
        out_specs=cur,
        out_shape=jax.ShapeDtypeStruct((t, BR_W), BF16),
        compiler_params=_params(("parallel",)),
        name="win_attn",
    )(q, k, k, k, v, v, v, bias, sink)


def _na_body(rows, q_ref, k_ref, v_ref, bias_ref, o_ref):
    j = pl.program_id(1)
    masks = _head_mask()
    nk = NA_KH * GRID_W
    for i in range(NA_RB):
        r = j * NA_RB + i
        rs = jnp.clip(r - NA_KH // 2, 0, rows - NA_KH)
        koff = pl.multiple_of(rs * GRID_W, GRID_W)
        q = q_ref[i * GRID_W:(i + 1) * GRID_W, :]
        ks = k_ref[pl.ds(koff, nk), :]
        vs = v_ref[pl.ds(koff, nk), :]
        s = _dot_nt(_stack_heads(q, masks), ks) + bias_ref[r - rs]
        m = jnp.max(s, axis=-1, keepdims=True)
        p = jnp.exp2(s - m)
        pn = (p * (1.0 / jnp.sum(p, axis=-1, keepdims=True))).astype(BF16)
        o_ref[i * GRID_W:(i + 1) * GRID_W, :] = _unstack_heads(_dot(pn, vs), masks, GRID_W).astype(BF16)


def _na_class(off, nseq, n, q, k, v, bias):
    rows = n // GRID_W
    qt = NA_RB * GRID_W
    nj = n // qt
    qb = off // qt
    sb = off // n
    seq = pl.BlockSpec((n, BR_W), lambda s, j: (sb + s, 0), pipeline_mode=pl.Buffered(1))
    return pl.pallas_call(
        functools.partial(_na_body, rows),
        grid=(nseq, nj),
        in_specs=[pl.BlockSpec((qt, BR_W), lambda s, j: (qb + s * nj + j, 0)), seq, seq, _const_spec(bias.shape)],
        out_specs=pl.BlockSpec((qt, BR_W), lambda s, j: (s * nj + j, 0)),
        out_shape=jax.ShapeDtypeStruct((nseq * n, BR_W), BF16),
        compiler_params=_params(("parallel", "parallel")),
        name="na_attn",
    )(q, k, v, bias)


def _na_attn(layout, q, k, v, bias):
    return jnp.concatenate([_na_class(off, nseq, n, q, k, v, bias) for off, nseq, n in layout.classes], axis=0)


def _mem_body(q_ref, mk_ref, mv_ref, o_ref):
    masks = _head_mask()
    s = _dot_nt(_stack_heads(q_ref[...], masks), mk_ref[0])
    m = jnp.max(s, axis=-1, keepdims=True)
    p = jnp.exp2(s - m)
    pn = (p * (1.0 / jnp.sum(p, axis=-1, keepdims=True))).astype(BF16)
    o_ref[...] = _unstack_heads(_dot(pn, mv_ref[0]), masks, MEM_TM).astype(BF16)


def _mem_attn(layout, q, mk, mv):
    t = q.shape[0]
    tm = MEM_TM
    mem_len = mk.shape[1]
    tile = pl.BlockSpec((tm, BR_W), lambda i: (i, 0))
    mem = pl.BlockSpec((1, mem_len, BR_W), lambda i: (layout.seq_index(i * tm), 0, 0))
    return pl.pallas_call(
        _mem_body,
        grid=(t // tm,),
        in_specs=[tile, mem, mem],
        out_specs=tile,
        out_shape=jax.ShapeDtypeStruct((t, BR_W), BF16),
        compiler_params=_params(("parallel",)),
        name="mem_attn",
    )(q, mk, mv)


def _merge_body(x_ref, ng_ref, oa_ref, ob_ref, oc_ref, od_ref, oe_ref, wg_ref, wm_ref, wb_ref, wo_ref, y_ref):
    x = x_ref[...]
    h = _rms_rows(x, ng_ref[...]).astype(BF16)
    merged = jnp.zeros((x.shape[0], D_MODEL), F32)
    for kb, o_ref in enumerate((oa_ref, ob_ref, oc_ref, od_ref, oe_ref)):
        g = _dot(h, wg_ref[:, kb * BR_W:(kb + 1) * BR_W])
        br = (o_ref[...].astype(F32) * (g * jax.nn.sigmoid(g))).astype(BF16)
        mg = jax.nn.sigmoid(_dot(h, wm_ref[:, kb * D_MODEL:(kb + 1) * D_MODEL]))
        merged = merged + mg * _dot(br, wb_ref[kb])
    y_ref[...] = x + _dot(merged.astype(BF16), wo_ref[...])


def _merge(x, ng, branches, wg, wm, wb, wo):
    t = x.shape[0]
    tm = MERGE_TM
    xt = pl.BlockSpec((tm, D_MODEL), lambda i: (i, 0))
    bt = pl.BlockSpec((tm, BR_W), lambda i: (i, 0))
    return pl.pallas_call(
        _merge_body,
        grid=(t // tm,),
        in_specs=[xt, _const_spec((1, D_MODEL))] + [bt] * N_BRANCH
        + [_const_spec(wg.shape), _const_spec(wm.shape), _const_spec(wb.shape), _const_spec(wo.shape)],
        out_specs=xt,
        out_shape=jax.ShapeDtypeStruct((t, D_MODEL), F32),
        compiler_params=_params(("parallel",)),
        name="merge",
    )(x, ng, *branches, wg, wm, wb, wo)


def _t5_bucket(rel):
    half = N_BUCKETS // 2
    exact = half // 2
    n = jnp.abs(rel)
    nf = jnp.maximum(n, 1).astype(F32)
    large = exact + (jnp.log(nf / exact) / math.log(MAX_DIST / exact) * (half - exact)).astype(jnp.int32)
    large = jnp.clip(large, 0, half - 1)
    return jnp.where(rel > 0, half, 0) + jnp.where(n < exact, n, large)


def _group_matrix(group):
    return jnp.asarray(np.kron(np.eye(BR_W // group), np.full((group, group), 1.0 / group)), BF16)


def _lookup(table, idx):
    onehot = (idx[..., None] == jnp.arange(table.shape[0])).astype(F32)
    return jnp.dot(onehot, table, precision=lax.Precision.HIGHEST)


def _diff_bias_tables(rel_bias):
    t = DF_T
    table = rel_bias[:, :BR_HEADS].astype(F32) * LOG2E
    kl = jnp.arange(t)[:, None]
    ql = jnp.arange(t)[None, :]
    rel = jnp.stack([kl - ql + d * t for d in (-1, 0, 1)])
    tiles = _lookup(table, _t5_bucket(rel)).transpose(0, 3, 1, 2)
    far = _lookup(table, _t5_bucket(jnp.asarray([-2 * t, 2 * t], jnp.int32)))
    return tiles, far


def _win_bias_table(rel_bias):
    rel = jnp.arange(3 * WIN_BLOCK)[None, :] - WIN_BLOCK - jnp.arange(WIN_BLOCK)[:, None]
    bias = _lookup(rel_bias[:, BR_HEADS:].astype(F32) * LOG2E, _t5_bucket(rel)).transpose(2, 0, 1)
    bias = jnp.where((jnp.abs(rel) <= WIN)[None], bias, NEG)
    return bias.reshape(BR_HEADS * WIN_BLOCK, 3 * WIN_BLOCK)


def _na_bias_tables(rpb):
    depth = rpb.shape[0]
    col = np.arange(GRID_W)
    cs = np.clip(col - NA_KW // 2, 0, GRID_W - NA_KW)
    inwin = (col[None, :] >= cs[:, None]) & (col[None, :] < cs[:, None] + NA_KW)
    dc = col[None, :] - col[:, None] + (NA_KW - 1)
    onehot = jnp.asarray(dc[None] == np.arange(2 * NA_KW - 1)[:, None, None], F32)
    toep = jnp.einsum("lhrc,cqk->lhrqk", rpb.astype(F32) * LOG2E, onehot, precision=lax.Precision.HIGHEST)
    toep = jnp.where(jnp.asarray(inwin)[None, None, None], toep, NEG)
    out = []
    for d in range(NA_KH):
        lo = NA_KH - 1 - d
        b = toep[:, :, lo:lo + NA_KH].transpose(0, 1, 3, 2, 4)
        out.append(b.reshape(depth, BR_HEADS * GRID_W, NA_KH * GRID_W))
    return jnp.stack(out, axis=1)


def _tile_gain(g, reps, scale=1.0):
    return jnp.tile(g.astype(F32), reps)[None, :] * scale


def kernel(x_prompt, x_sample, mem_prompt, mem_sample, norm_g, mem_norm_g, w_in, w_mem_kv, rel_bias, hgrn_lb,
           hgrn_norm_g, diff_qk_g, diff_lambda, diff_subln_g, win_qk_g, win_sink, na_qk_g, na_rpb, mem_qk_g,
           w_branch, w_out):
    depth = w_in.shape[0]
    bp, lp, _ = x_prompt.shape
    bs, ls, _ = x_sample.shape
    mem_len = mem_prompt.shape[1]
    layout = _Layout(bp, lp, bs, ls)
    t = layout.total
    assert t % PROJ_TM == 0 and lp % DF_T == 0 and ls % DF_T == 0
    assert lp % (NA_RB * GRID_W) == 0 and ls % (NA_RB * GRID_W) == 0 and mem_len % MEM_TM == 0
    assert lp % (WIN_NB * WIN_BLOCK) == 0 and ls % (WIN_NB * WIN_BLOCK) == 0

    x = jnp.concatenate([x_prompt.reshape(bp * lp, D_MODEL), x_sample.reshape(bs * ls, D_MODEL)], axis=0)
    mem = jnp.concatenate([mem_prompt.reshape(bp * mem_len, D_MODEL), mem_sample.reshape(bs * mem_len, D_MODEL)], axis=0)

    sm = jax.nn.softmax(hgrn_lb.astype(F32), axis=1)
    lb_all = jnp.clip(jnp.cumsum(sm, axis=1) - sm[:, :1], 0.0, 1.0 - 1e-6)
    lam_init = jnp.asarray([0.8 - 0.6 * math.exp(-0.3 * l) for l in range(depth)], F32)
    lam = diff_lambda.astype(F32)
    lmb = jnp.exp(jnp.sum(lam[:, 0] * lam[:, 1], axis=-1)) - jnp.exp(jnp.sum(lam[:, 2] * lam[:, 3], axis=-1)) + lam_init

    w_in_b = w_in.astype(BF16)

    def expand_kv(w):
        w = w.reshape(depth, D_MODEL, WIN_KV_HEADS, HEAD_DIM)
        return jnp.repeat(w, BR_HEADS // WIN_KV_HEADS, axis=2).reshape(depth, D_MODEL, BR_W)

    w1 = jnp.concatenate([w_in_b[:, :, A_Q:A_G], w_in_b[:, :, B_Q:B_G], w_in_b[:, :, C_Q:C_K],
                          expand_kv(w_in_b[:, :, C_K:C_V]), expand_kv(w_in_b[:, :, C_V:C_G]),
                          w_in_b[:, :, D_Q:D_G], w_in_b[:, :, E_Q:E_G]], axis=-1)
    wg = jnp.concatenate([w_in_b[:, :, A_G:B_Q], w_in_b[:, :, B_G:C_Q], w_in_b[:, :, C_G:D_Q],
                          w_in_b[:, :, D_G:E_Q], w_in_b[:, :, E_G:M_G]], axis=-1)
    wm = w_in_b[:, :, M_G:]
    wb = w_branch.astype(BF16)
    wo = w_out.astype(BF16)
    wmem = w_mem_kv.astype(BF16)

    sc_b = DF_DK ** -0.5 * LOG2E
    sc_h = HEAD_DIM ** -0.5 * LOG2E
    zrow = jnp.zeros((depth, 1, BR_W), F32)

    def per_layer(fn):
        return jnp.stack([fn(l) for l in range(depth)])

    ep = jnp.concatenate([
        per_layer(lambda l: _tile_gain(diff_qk_g[l, 0], 8, sc_b)),
        per_layer(lambda l: _tile_gain(diff_qk_g[l, 1], 8)),
        per_layer(lambda l: _tile_gain(win_qk_g[l, 0], 4, sc_h)),
        per_layer(lambda l: _tile_gain(win_qk_g[l, 1], 4)),
        per_layer(lambda l: _tile_gain(na_qk_g[l, 0], 4, sc_h)),
        per_layer(lambda l: _tile_gain(na_qk_g[l, 1], 4)),
        per_layer(lambda l: _tile_gain(mem_qk_g[l, 0], 4, sc_h)),
        lb_all[0][:, None, :], lb_all[1][:, None, :]] + [zrow] * 7, axis=1)
    gk_mem = per_layer(lambda l: _tile_gain(mem_qk_g[l, 1], 4))
    gn_hg = hgrn_norm_g.astype(F32)[:, None, :]
    gcol = per_layer(lambda l: (jnp.tile(diff_subln_g[l].astype(F32), BR_HEADS) * (1.0 - lam_init[l]))[:, None])
    sink = per_layer(lambda l: jnp.repeat(win_sink[l].astype(F32) * LOG2E, WIN_BLOCK)[:, None])
    na_bias = _na_bias_tables(na_rpb)

    diff_bias, diff_far = _diff_bias_tables(rel_bias)
    win_bias = _win_bias_table(rel_bias)
    sc = jnp.concatenate([jnp.broadcast_to(diff_far[None], (depth, 2, BR_HEADS)),
                          jnp.broadcast_to(lmb[:, None, None], (depth, 1, BR_HEADS))], axis=1)

    g32 = _group_matrix(DF_DK)
    g64 = _group_matrix(HEAD_DIM)
    consts_f = tuple(jnp.asarray(a, dt) for a, dt in zip(_hgrn_constants(HG_C, False), (BF16, F32, F32)))
    consts_b = tuple(jnp.asarray(a, dt) for a, dt in zip(_hgrn_constants(HG_C, True), (BF16, F32, F32)))

    def layer(x, p):
        ng = p["ng"]
        (a_q, a_v, a_lf, a_kk, b_q, b_k, b_v, c_q, c_k, c_v, d_q, d_k, d_v, e_q) = _project(
            x, ng, p["w1"], p["ep"], g32, g64)
        mk, mv = _mem_kv(mem, p["mng"], p["wmem"], p["gk_mem"], g64)
        o_a = _hgrn(layout, a_q, a_v, a_kk, a_lf, consts_f, consts_b, g64, p["gn_hg"])
        o_b = _diff_attn(layout, b_q, b_k, b_v, diff_bias, p["sc"], p["gcol"])
        o_c = _win_attn(layout, c_q, c_k, c_v, win_bias, p["sink"])
        o_d = _na_attn(layout, d_q, d_k, d_v, p["na_bias"])
        o_e = _mem_attn(layout, e_q, mk.reshape(layout.nseq, mem_len, BR_W), mv.reshape(layout.nseq, mem_len, BR_W))
        y = _merge(x, ng, (o_a, o_b, o_c, o_d, o_e), p["wg"], p["wm"], p["wb"], p["wo"])
        return y, None

    params = dict(ng=norm_g.astype(F32)[:, None, :], mng=mem_norm_g.astype(F32)[:, None, :], w1=w1, ep=ep,
                  wmem=wmem, gk_mem=gk_mem, gn_hg=gn_hg, sc=sc, gcol=gcol, sink=sink, na_bias=na_bias,
                  wg=wg, wm=wm, wb=wb, wo=wo)
    x, _ = lax.scan(layer, x, params)
    y_prompt = x[:layout.off1].reshape(bp, lp, D_MODEL)
    y_sample = x[layout.off1:].reshape(bs, ls, D_MODEL)
    return (y_prompt, y_sample)
```

```python
import functools
import math

import numpy as np
import jax
import jax.numpy as jnp
from jax import lax
from jax.experimental import pallas as pl
from jax.experimental.pallas import tpu as pltpu

F32 = jnp.float32
BF16 = jnp.bfloat16

D_MODEL = 1024
HEAD_DIM = 64
BR_HEADS = 4
BR_W = BR_HEADS * HEAD_DIM
N_BRANCH = 5
DF_DK = HEAD_DIM // 2
WIN = 128
WIN_BLOCK = 128
WIN_KV_HEADS = 2
GRID_W = 64
NA_KH = 8
NA_KW = 16
N_BUCKETS = 32
MAX_DIST = 128
EPS = 1e-6
NEG = -1e30
LB_FLOOR = 1e-30
LOG2E = 1.4426950408889634

A_Q = 0
A_G = 4 * BR_W
B_Q = A_G + BR_W
B_G = B_Q + 3 * BR_W
C_Q = B_G + BR_W
C_K = C_Q + BR_W
C_V = C_K + WIN_KV_HEADS * HEAD_DIM
C_G = C_V + WIN_KV_HEADS * HEAD_DIM
D_Q = C_G + BR_W
D_G = D_Q + 3 * BR_W
E_Q = D_G + BR_W
E_G = E_Q + BR_W
M_G = E_G + BR_W

VMEM_LIMIT_BYTES = 56 * 1024 * 1024

PROJ_TM = 512
MERGE_TM = 256
MEM_TM = 256
HG_C = 128
DF_T = 256
NA_RB = 8
WIN_NB = 4
N_PROJ_SEG = 14
DF_VROWS = 80
DF_RC = 64
DF_UNROLL = 4


def _params(sem):
    return pltpu.CompilerParams(dimension_semantics=sem, vmem_limit_bytes=VMEM_LIMIT_BYTES)


def _const_spec(shape):
    nd = len(shape)
    return pl.BlockSpec(shape, lambda *_: (0,) * nd, pipeline_mode=pl.Buffered(1))


def _dot(a, b):
    return jnp.dot(a, b, preferred_element_type=F32)


def _dot_nt(a, b):
    return lax.dot_general(a, b, (((1,), (1,)), ((), ())), preferred_element_type=F32)


def _rms_rows(x, g):
    ms = jnp.mean(x * x, axis=-1, keepdims=True)
    return x * lax.rsqrt(ms + EPS) * g


def _group_rms(x, gmat):
    x2 = x * x
    hi = x2.astype(BF16)
    lo = (x2 - hi.astype(F32)).astype(BF16)
    ms = _dot(hi, gmat) + _dot(lo, gmat)
    return x * lax.rsqrt(ms + EPS)


def _head_mask(width=BR_W):
    lane = lax.broadcasted_iota(jnp.int32, (1, width), 1)
    return [(lane >= h * HEAD_DIM) & (lane < (h + 1) * HEAD_DIM) for h in range(BR_HEADS)]


def _stack_heads(q, masks):
    zero = jnp.zeros_like(q)
    return jnp.concatenate([jnp.where(m, q, zero) for m in masks], axis=0)


def _unstack_heads(o_all, masks, m):
    out = jnp.zeros((m, BR_W), F32)
    for h, mk in enumerate(masks):
        out = out + jnp.where(mk, o_all[h * m:(h + 1) * m, :], 0.0)
    return out


class _Layout:
    def __init__(self, n_prompt, len_prompt, n_sample, len_sample):
        self.classes = ((0, n_prompt, len_prompt), (n_prompt * len_prompt, n_sample, len_sample))
        self.off1 = n_prompt * len_prompt
        self.lp = len_prompt
        self.ls = len_sample
        self.total = self.off1 + n_sample * len_sample
        self.nseq = n_prompt + n_sample
        self.n_prompt = n_prompt
        assert self.off1 % len_sample == 0

    def seq_bounds(self, tok):
        in_p = tok < self.off1
        start_p = (tok // self.lp) * self.lp
        start_s = self.off1 + ((tok - self.off1) // self.ls) * self.ls
        return jnp.where(in_p, start_p, start_s), jnp.where(in_p, self.lp, self.ls)

    def seq_index(self, tok):
        return jnp.where(tok < self.off1, tok // self.lp, self.n_prompt + (tok - self.off1) // self.ls)


def _proj_body(x_ref, ng_ref, w_ref, ep_ref, g32_ref, g64_ref,
               aq_ref, av_ref, alf_ref, akk_ref, bq_ref, bk_ref, bv_ref,
               cq_ref, ck_ref, cv_ref, dq_ref, dk_ref, dv_ref, eq_ref):
    h = _rms_rows(x_ref[...], ng_ref[...]).astype(BF16)
    ep = ep_ref[...]

    def seg(i):
        return _dot(h, w_ref[:, i * BR_W:(i + 1) * BR_W])

    def row(r):
        return ep[r:r + 1, :]

    aq = seg(0)
    aq_ref[...] = (aq * jax.nn.sigmoid(aq)).astype(BF16)
    av_ref[...] = seg(1).astype(BF16)
    for d in range(2):
        z = seg(2 + d)
        lb = row(7 + d)
        e = jnp.exp(-jnp.abs(z))
        log_sig = jnp.minimum(z, 0.0) - jnp.log1p(e)
        t1 = jnp.log(jnp.maximum(lb, LB_FLOOR))
        t2 = jnp.log1p(-lb) + log_sig
        logf = jnp.maximum(t1, t2) + jnp.log1p(jnp.exp(-jnp.abs(t1 - t2)))
        sig_neg = jnp.where(z >= 0.0, e, 1.0) / (1.0 + e)
        alf_ref[:, d * BR_W:(d + 1) * BR_W] = logf
        akk_ref[:, d * BR_W:(d + 1) * BR_W] = ((1.0 - lb) * sig_neg).astype(BF16)
    g32 = g32_ref[...]
    g64 = g64_ref[...]
    bq_ref[...] = (_group_rms(seg(4), g32) * row(0)).astype(BF16)
    bk_ref[...] = (_group_rms(seg(5), g32) * row(1)).astype(BF16)
    bv_ref[...] = seg(6).astype(BF16)
    cq_ref[...] = (_group_rms(seg(7), g64) * row(2)).astype(BF16)
    ck_ref[...] = (_group_rms(seg(8), g64) * row(3)).astype(BF16)
    cv_ref[...] = seg(9).astype(BF16)
    dq_ref[...] = (_group_rms(seg(10), g64) * row(4)).astype(BF16)
    dk_ref[...] = (_group_rms(seg(11), g64) * row(5)).astype(BF16)
    dv_ref[...] = seg(12).astype(BF16)
    eq_ref[...] = (_group_rms(seg(13), g64) * row(6)).astype(BF16)


def _project(x, ng, w1, ep, g32, g64):
    t = x.shape[0]
    tm = PROJ_TM
    tile = lambda w: pl.BlockSpec((tm, w), lambda i: (i, 0))
    widths = [BR_W, BR_W, 2 * BR_W, 2 * BR_W] + [BR_W] * 10
    dtypes = [BF16, BF16, F32, BF16] + [BF16] * 10
    return pl.pallas_call(
        _proj_body,
        grid=(t // tm,),
        in_specs=[tile(D_MODEL), _const_spec((1, D_MODEL)), _const_spec(w1.shape), _const_spec(ep.shape),
                  _const_spec(g32.shape), _const_spec(g64.shape)],
        out_specs=[tile(w) for w in widths],
        out_shape=[jax.ShapeDtypeStruct((t, w), dt) for w, dt in zip(widths, dtypes)],
        compiler_params=_params(("parallel",)),
        name="proj",
    )(x, ng, w1, ep, g32, g64)


def _memkv_body(m_ref, g_ref, w_ref, gk_ref, g64_ref, mk_ref, mv_ref):
    mh = _rms_rows(m_ref[...], g_ref[...]).astype(BF16)
    kv = _dot(mh, w_ref[...])
    mk_ref[...] = (_group_rms(kv[:, :BR_W], g64_ref[...]) * gk_ref[...]).astype(BF16)
    mv_ref[...] = kv[:, BR_W:].astype(BF16)


def _mem_kv(mem, g, w, gk, g64):
    t = mem.shape[0]
    tm = 256
    return pl.pallas_call(
        _memkv_body,
        grid=(t // tm,),
        in_specs=[pl.BlockSpec((tm, D_MODEL), lambda i: (i, 0)), _const_spec((1, D_MODEL)), _const_spec(w.shape),
                  _const_spec((1, BR_W)), _const_spec(g64.shape)],
        out_specs=[pl.BlockSpec((tm, BR_W), lambda i: (i, 0))] * 2,
        out_shape=[jax.ShapeDtypeStruct((t, BR_W), BF16)] * 2,
        compiler_params=_params(("parallel",)),
        name="memkv",
    )(mem, g, w, gk, g64)


def _hgrn_constants(c, reverse):
    nl = int(math.log2(c))
    idx = np.arange(c)
    t = idx[:, None]
    u = idx[None, :]
    incl = (u <= t).astype(np.float32)
    rest = (u > t).astype(np.float32)
    tot = np.ones((8, c), np.float32)
    mds, mes, lms = [], [], [np.eye(c, dtype=np.float32)]
    for lev in range(nl):
        w = 1 << lev
        blk = idx // w
        odd = (blk % 2 == 1)
        md = (odd[:, None] & (u >= (blk * w)[:, None]) & (u <= t)).astype(np.float32)
        me = ((~odd)[:, None] & (u > t) & (u <= ((blk + 1) * w - 1)[:, None])).astype(np.float32)
        lm = (odd[:, None] & (blk[None, :] == (blk - 1)[:, None])).astype(np.float32)
        mds.append(md)
        mes.append(me)
        lms.append(lm)
    mats = [incl, rest, tot] + [md + me for md, me in zip(mds, mes)]
    if reverse:
        mats = [m[::-1, ::-1] for m in mats]
        lms = [m[::-1, ::-1] for m in lms]
    mall = np.concatenate(mats, axis=0)
    lmst = np.stack([np.tile(m, (BR_HEADS, 1)) for m in lms])
    bd = np.kron(np.eye(BR_HEADS, dtype=np.float32), np.ones((HEAD_DIM, HEAD_DIM), np.float32))
    return mall, lmst, bd


def _hgrn_chunk(layout, chunk, reverse, q_ref, k_ref, v_ref, lf_ref, mall_ref, lm_ref, bd_ref, o_ref, st_ref):
    c = HG_C
    nl = int(math.log2(c))
    tok = chunk * c
    start, length = layout.seq_bounds(tok)
    fresh = (tok + c == start + length) if reverse else (tok == start)

    masks = _head_mask()
    lf = lf_ref[...]
    hi = lf.astype(BF16)
    mid = (lf - hi.astype(F32)).astype(BF16)
    mall = mall_ref[...]
    cums = _dot(mall, hi) + _dot(mall, mid)
    b = cums[0:c, :]
    b_rest = cums[c:2 * c, :]
    b_tot = cums[2 * c:2 * c + 1, :]
    base = 2 * c + 8

    qf = q_ref[...].astype(F32)
    kf = k_ref[...].astype(F32)
    v = v_ref[...]
    st = jnp.where(fresh, 0.0, st_ref[...])

    o = _dot_nt((qf * jnp.exp(b)).astype(BF16), st.astype(BF16))
    a_all = _dot_nt(_stack_heads(q_ref[...], masks), k_ref[...]) * lm_ref[0]
    for lev in range(nl):
        dec = jnp.exp(cums[base + lev * c:base + (lev + 1) * c, :])
        ql = (qf * dec).astype(BF16)
        kl = (kf * dec).astype(BF16)
        a_all = a_all + _dot_nt(_stack_heads(ql, masks), kl) * lm_ref[lev + 1]
    a_bf = a_all.astype(BF16)
    zero = jnp.zeros_like(v)
    for h, mk in enumerate(masks):
        o = o + _dot(a_bf[h * c:(h + 1) * c, :], jnp.where(mk, v, zero))
    o_ref[...] = o

    kst = (kf * jnp.exp(b_rest)).astype(BF16)
    vt = v.astype(F32).T.astype(BF16)
    st_ref[...] = (st * jnp.exp(b_tot) + _dot(vt, kst)) * bd_ref[...]


def _hgrn_body(layout, nct, qf_ref, kf_ref, vf_ref, lff_ref, qb_ref, kb_ref, vb_ref, lfb_ref,
               mallf_ref, lmf_ref, mallb_ref, lmb_ref, bd_ref, of_ref, ob_ref, stf_ref, stb_ref):
    i = pl.program_id(0)
    _hgrn_chunk(layout, i, False, qf_ref, kf_ref, vf_ref, lff_ref, mallf_ref, lmf_ref, bd_ref, of_ref, stf_ref)
    _hgrn_chunk(layout, nct - 1 - i, True, qb_ref, kb_ref, vb_ref, lfb_ref, mallb_ref, lmb_ref, bd_ref,
                ob_ref, stb_ref)


def _hgrn_norm_body(of_ref, ob_ref, g64_ref, gn_ref, o_ref):
    o_ref[...] = (_group_rms(of_ref[...] + ob_ref[...], g64_ref[...]) * gn_ref[...]).astype(BF16)


def _hgrn(layout, qs, v, kk, logf, consts_f, consts_b, g64, gn):
    t = qs.shape[0]
    c = HG_C
    nct = t // c
    mall_f, lm_f, bd = consts_f
    mall_b, lm_b, _ = consts_b

    def specs(reverse):
        cm = (lambda i: nct - 1 - i) if reverse else (lambda i: i)
        d = 1 if reverse else 0
        tile = pl.BlockSpec((c, BR_W), lambda i: (cm(i), 0))
        half = pl.BlockSpec((c, BR_W), lambda i: (cm(i), d))
        return tile, half

    tile_f, half_f = specs(False)
    tile_b, half_b = specs(True)
    o_f, o_b = pl.pallas_call(
        functools.partial(_hgrn_body, layout, nct),
        grid=(nct,),
        in_specs=[tile_f, half_f, tile_f, half_f, tile_b, half_b, tile_b, half_b,
                  _const_spec(mall_f.shape), _const_spec(lm_f.shape), _const_spec(mall_b.shape),
                  _const_spec(lm_b.shape), _const_spec(bd.shape)],
        out_specs=[tile_f, tile_b],
        out_shape=[jax.ShapeDtypeStruct((t, BR_W), F32)] * 2,
        scratch_shapes=[pltpu.VMEM((BR_W, BR_W), F32)] * 2,
        compiler_params=_params(("arbitrary",)),
        name="hgrn_scan",
    )(qs, kk, v, logf, qs, kk, v, logf, mall_f, lm_f, mall_b, lm_b, bd)

    tm = PROJ_TM
    tok = pl.BlockSpec((tm, BR_W), lambda i: (i, 0))
    return pl.pallas_call(
        _hgrn_norm_body,
        grid=(t // tm,),
        in_specs=[tok, tok, _const_spec(g64.shape), _const_spec((1, BR_W))],
        out_specs=tok,
        out_shape=jax.ShapeDtypeStruct((t, BR_W), BF16),
        compiler_params=_params(("parallel",)),
        name="hgrn_norm",
    )(o_f, o_b, g64, gn)


def _diff_body(nkt, qt_ref, k_ref, vt_ref, bias_ref, sc_ref, gcol_ref, o_ref, qm_ref, m_ref, acc_ref,
               s0_ref, s1_ref, mx0_ref, mx1_ref, p_ref):
    tq = DF_T
    qi = pl.program_id(1)
    qt = qt_ref[...]
    rowid = lax.broadcasted_iota(jnp.int32, (BR_W, tq), 0)
    for hc in range(2 * BR_HEADS):
        sel = (rowid >= hc * DF_DK) & (rowid < (hc + 1) * DF_DK)
        qm_ref[hc] = jnp.where(sel, qt, jnp.zeros_like(qt))
    m_ref[...] = jnp.full(m_ref.shape, NEG, F32)
    acc_ref[...] = jnp.zeros_like(acc_ref)

    nu = 2 * BR_HEADS
    rows = [(r, r + DF_RC) for r in range(0, tq, DF_RC)]

    def qk_unit(kj, hc, buf, near=None):
        sbuf, mxbuf = buf
        s = _dot(k_ref[kj], qm_ref[hc])
        if near is not None:
            s = s + bias_ref[near, hc // 2]
        sbuf[hc] = s
        mxbuf[hc] = jnp.max(s.reshape(tq // 8, 8, tq), axis=0)

    def softmax_pv_unit(kj, hc, buf, const_bias):
        sbuf, mxbuf = buf
        c = const_bias(hc // 2)
        m_old = m_ref[hc]
        m_new = jnp.maximum(m_old, jnp.max(mxbuf[hc], axis=0, keepdims=True) + c)
        shift = m_new - c
        for r0, r1 in rows:
            p_ref[hc, r0:r1, :] = jnp.exp2(sbuf[hc, r0:r1, :] - shift).astype(BF16)
        m_ref[hc] = m_new
        acc_ref[hc] = acc_ref[hc] * jnp.exp2(m_old - m_new) + _dot(vt_ref[kj, hc // 2], p_ref[hc])

    def step(kj, buf, const_bias, nxt):
        for hc in range(nu):
            qk_unit(nxt[0], hc, nxt[1], nxt[2])
            softmax_pv_unit(kj, hc, buf, const_bias)

    last = nkt - 1
    n_low = jnp.maximum(qi - 1, 0)
    n_far = n_low + jnp.maximum(nkt - qi - 2, 0)
    buf_a = (s0_ref, mx0_ref)
    buf_b = (s1_ref, mx1_ref)

    def far_tile(f):
        return jnp.minimum(jnp.where(f < n_low, f, f - n_low + qi + 2), last)

    def dead_if(cond):
        pen = jnp.where(cond, 2.0 * NEG, 0.0)
        return lambda h: pen

    near = [jnp.clip(qi + d - 1, 0, last) for d in range(3)]
    for hc in range(nu):
        qk_unit(near[0], hc, buf_a, 0)
    step(near[0], buf_a, dead_if(qi == 0), (near[1], buf_b, 1))
    step(near[1], buf_b, dead_if(False), (near[2], buf_a, 2))
    step(near[2], buf_a, dead_if(qi == last), (far_tile(0), buf_b, None))

    def body(i, carry):
        f0 = DF_UNROLL * i
        for u in range(DF_UNROLL):
            kj = far_tile(f0 + u)
            row = jnp.where(kj < qi, 0, 1)
            live = f0 + u < n_far
            cur, nxt = (buf_b, buf_a) if u % 2 == 0 else (buf_a, buf_b)
            step(kj, cur, lambda h, live=live, row=row: jnp.where(live, sc_ref[row, h], 2.0 * NEG),
                 (far_tile(f0 + u + 1), nxt, None))
        return carry
    lax.fori_loop(0, (n_far + DF_UNROLL - 1) // DF_UNROLL, body, 0)

    lmb = sc_ref[2, 0]
    outs = []
    for h in range(BR_HEADS):
        a0 = acc_ref[2 * h]
        a1 = acc_ref[2 * h + 1]
        o0 = a0[:HEAD_DIM, :] / a0[HEAD_DIM:HEAD_DIM + 1, :]
        o1 = a1[:HEAD_DIM, :] / a1[HEAD_DIM:HEAD_DIM + 1, :]
        o = o0 - lmb * o1
        ms = jnp.mean(o * o, axis=0, keepdims=True)
        outs.append(o * lax.rsqrt(ms + EPS))
    ot = jnp.concatenate(outs, axis=0) * gcol_ref[...]
    o_ref[...] = ot.T.astype(BF16)


def _diff_attn_class(off, nseq, n, qt, k3, vt4, bias, sc, gcol):
    t = DF_T
    nkt = n // t
    qb = off // t
    sb = off // n
    return pl.pallas_call(
        functools.partial(_diff_body, nkt),
        grid=(nseq, nkt),
        in_specs=[pl.BlockSpec((BR_W, t), lambda s, i: (0, qb + s * nkt + i)),
                  pl.BlockSpec((nkt, t, BR_W), lambda s, i: (sb + s, 0, 0), pipeline_mode=pl.Buffered(1)),
                  pl.BlockSpec((nkt, BR_HEADS, DF_VROWS, t), lambda s, i: (sb + s, 0, 0, 0),
                               pipeline_mode=pl.Buffered(1)),
                  _const_spec(bias.shape),
                  pl.BlockSpec(memory_space=pltpu.SMEM),
                  _const_spec(gcol.shape)],
        out_specs=pl.BlockSpec((t, BR_W), lambda s, i: (s * nkt + i, 0)),
        out_shape=jax.ShapeDtypeStruct((nseq * n, BR_W), BF16),
        scratch_shapes=[pltpu.VMEM((2 * BR_HEADS, BR_W, t), BF16),
                        pltpu.VMEM((2 * BR_HEADS, 1, t), F32),
                        pltpu.VMEM((2 * BR_HEADS, DF_VROWS, t), F32),
                        pltpu.VMEM((2 * BR_HEADS, t, t), F32),
                        pltpu.VMEM((2 * BR_HEADS, t, t), F32),
                        pltpu.VMEM((2 * BR_HEADS, 8, t), F32),
                        pltpu.VMEM((2 * BR_HEADS, 8, t), F32),
                        pltpu.VMEM((2 * BR_HEADS, t, t), BF16)],
        compiler_params=_params(("parallel", "parallel")),
        name="diff_attn",
    )(qt, k3, vt4, bias, sc, gcol)


def _diff_attn(layout, q, k, v, bias, sc, gcol):
    t = q.shape[0]
    qt = q.T
    ones = jnp.ones((t, BR_HEADS, 1), BF16)
    zeros = jnp.zeros((t, BR_HEADS, DF_VROWS - HEAD_DIM - 1), BF16)
    vaug = jnp.concatenate([v.reshape(t, BR_HEADS, HEAD_DIM), ones, zeros], axis=-1)
    k3 = k.reshape(t // DF_T, DF_T, BR_W)
    vt4 = vaug.reshape(t // DF_T, DF_T, BR_HEADS, DF_VROWS).transpose(0, 2, 3, 1)
    outs = [_diff_attn_class(off, nseq, n, qt, k3, vt4, bias, sc, gcol) for off, nseq, n in layout.classes]
    return jnp.concatenate(outs, axis=0)


def _win_body(layout, q_ref, kp_ref, kc_ref, kn_ref, vp_ref, vc_ref, vn_ref, bias_ref, sink_ref, o_ref):
    blk = WIN_BLOCK
    i = pl.program_id(0)
    tok = i * (WIN_NB * blk)
    start, length = layout.seq_bounds(tok)
    masks = _head_mask()
    kcat = jnp.concatenate([kp_ref[...], kc_ref[...], kn_ref[...]], axis=0)
    vcat = jnp.concatenate([vp_ref[...], vc_ref[...], vn_ref[...]], axis=0)
    col = lax.broadcasted_iota(jnp.int32, (1, 3 * blk), 1)
    sink = sink_ref[...]
    bias = bias_ref[...]
    for j in range(WIN_NB):
        has_prev = tok + j * blk > start
        has_next = tok + (j + 1) * blk < start + length
        dead = ((col < blk) & jnp.logical_not(has_prev)) | ((col >= 2 * blk) & jnp.logical_not(has_next))
        qst = _stack_heads(q_ref[j * blk:(j + 1) * blk, :], masks)
        s = _dot_nt(qst, kcat[j * blk:(j + 3) * blk, :]) + bias + jnp.where(dead, NEG, 0.0)
        m = jnp.maximum(jnp.max(s, axis=-1, keepdims=True), sink)
        p = jnp.exp2(s - m)
        den = jnp.sum(p, axis=-1, keepdims=True) + jnp.exp2(sink - m)
        pn = (p * (1.0 / den)).astype(BF16)
        o_all = _dot(pn, vcat[j * blk:(j + 3) * blk, :])
        o_ref[j * blk:(j + 1) * blk, :] = _unstack_heads(o_all, masks, blk).astype(BF16)


def _win_attn(layout, q, k, v, bias, sink):
    t = q.shape[0]
    blk = WIN_BLOCK
    nb = t // blk
    cur = pl.BlockSpec((WIN_NB * blk, BR_W), lambda i: (i, 0))
    prev = pl.BlockSpec((blk, BR_W), lambda i: (jnp.maximum(i * WIN_NB - 1, 0), 0))
    nxt = pl.BlockSpec((blk, BR_W), lambda i: (jnp.minimum((i + 1) * WIN_NB, nb - 1), 0))
    return pl.pallas_call(
        functools.partial(_win_body, layout),
        grid=(nb // WIN_NB,),
        in_specs=[cur, prev, cur, nxt, prev, cur, nxt, _const_spec(bias.shape), _const_spec(sink.shape)],
        out_specs=cur,
        out_shape=jax.ShapeDtypeStruct((t, BR_W), BF16),
        compiler_params=_params(("parallel",)),
        name="win_attn",
    )(q, k, k, k, v, v, v, bias, sink)


def _na_body(rows, q_ref, k_ref, v_ref, bias_ref, o_ref):
    j = pl.program_id(1)
    masks = _head_mask()
    nk = NA_KH * GRID_W
    for i in range(NA_RB):
        r = j * NA_RB + i
        rs = jnp.clip(r - NA_KH // 2, 0, rows - NA_KH)
        koff = pl.multiple_of(rs * GRID_W, GRID_W)
        q = q_ref[i * GRID_W:(i + 1) * GRID_W, :]
        ks = k_ref[pl.ds(koff, nk), :]
        vs = v_ref[pl.ds(koff, nk), :]
        s = _dot_nt(_stack_heads(q, masks), ks) + bias_ref[r - rs]
        m = jnp.max(s, axis=-1, keepdims=True)
        p = jnp.exp2(s - m)
        pn = (p * (1.0 / jnp.sum(p, axis=-1, keepdims=True))).astype(BF16)
        o_ref[i * GRID_W:(i + 1) * GRID_W, :] = _unstack_heads(_dot(pn, vs), masks, GRID_W).astype(BF16)


def _na_class(off, nseq, n, q, k, v, bias):
    rows = n // GRID_W
    qt = NA_RB * GRID_W
    nj = n // qt
    qb = off // qt
    sb = off // n
    seq = pl.BlockSpec((n, BR_W), lambda s, j: (sb + s, 0), pipeline_mode=pl.Buffered(1))
    return pl.pallas_call(
        functools.partial(_na_body, rows),
        grid=(nseq, nj),
        in_specs=[pl.BlockSpec((qt, BR_W), lambda s, j: (qb + s * nj + j, 0)), seq, seq, _const_spec(bias.shape)],
        out_specs=pl.BlockSpec((qt, BR_W), lambda s, j: (s * nj + j, 0)),
        out_shape=jax.ShapeDtypeStruct((nseq * n, BR_W), BF16),
        compiler_params=_params(("parallel", "parallel")),
        name="na_attn",
    )(q, k, v, bias)


def _na_attn(layout, q, k, v, bias):
    return jnp.concatenate([_na_class(off, nseq, n, q, k, v, bias) for off, nseq, n in layout.classes], axis=0)


def _mem_body(q_ref, mk_ref, mv_ref, o_ref):
    masks = _head_mask()
    s = _dot_nt(_stack_heads(q_ref[...], masks), mk_ref[0])
    m = jnp.max(s, axis=-1, keepdims=True)
    p = jnp.exp2(s - m)
    pn = (p * (1.0 / jnp.sum(p, axis=-1, keepdims=True))).astype(BF16)
    o_ref[...] = _unstack_heads(_dot(pn, mv_ref[0]), masks, MEM_TM).astype(BF16)


def _mem_attn(layout, q, mk, mv):
    t = q.shape[0]
    tm = MEM_TM
    mem_len = mk.shape[1]
    tile = pl.BlockSpec((tm, BR_W), lambda i: (i, 0))
    mem = pl.BlockSpec((1, mem_len, BR_W), lambda i: (layout.seq_index(i * tm), 0, 0))
    return pl.pallas_call(
        _mem_body,
        grid=(t // tm,),
        in_specs=[tile, mem, mem],
        out_specs=tile,
        out_shape=jax.ShapeDtypeStruct((t, BR_W), BF16),
        compiler_params=_params(("parallel",)),
        name="mem_attn",
    )(q, mk, mv)


def _merge_body(x_ref, ng_ref, oa_ref, ob_ref, oc_ref, od_ref, oe_ref, wg_ref, wm_ref, wb_ref, wo_ref, y_ref):
    x = x_ref[...]
    h = _rms_rows(x, ng_ref[...]).astype(BF16)
    merged = jnp.zeros((x.shape[0], D_MODEL), F32)
    for kb, o_ref in enumerate((oa_ref, ob_ref, oc_ref, od_ref, oe_ref)):
        g = _dot(h, wg_ref[:, kb * BR_W:(kb + 1) * BR_W])
        br = (o_ref[...].astype(F32) * (g * jax.nn.sigmoid(g))).astype(BF16)
        mg = jax.nn.sigmoid(_dot(h, wm_ref[:, kb * D_MODEL:(kb + 1) * D_MODEL]))
        merged = merged + mg * _dot(br, wb_ref[kb])
    y_ref[...] = x + _dot(merged.astype(BF16), wo_ref[...])


def _merge(x, ng, branches, wg, wm, wb, wo):
    t = x.shape[0]
    tm = MERGE_TM
    xt = pl.BlockSpec((tm, D_MODEL), lambda i: (i, 0))
    bt = pl.BlockSpec((tm, BR_W), lambda i: (i, 0))
    return pl.pallas_call(
        _merge_body,
        grid=(t // tm,),
        in_specs=[xt, _const_spec((1, D_MODEL))] + [bt] * N_BRANCH
        + [_const_spec(wg.shape), _const_spec(wm.shape), _const_spec(wb.shape), _const_spec(wo.shape)],
        out_specs=xt,
        out_shape=jax.ShapeDtypeStruct((t, D_MODEL), F32),
        compiler_params=_params(("parallel",)),
        name="merge",
    )(x, ng, *branches, wg, wm, wb, wo)


def _t5_bucket(rel):
    half = N_BUCKETS // 2
    exact = half // 2
    n = jnp.abs(rel)
    nf = jnp.maximum(n, 1).astype(F32)
    large = exact + (jnp.log(nf / exact) / math.log(MAX_DIST / exact) * (half - exact)).astype(jnp.int32)
    large = jnp.clip(large, 0, half - 1)
    return jnp.where(rel > 0, half, 0) + jnp.where(n < exact, n, large)


def _group_matrix(group):
    return jnp.asarray(np.kron(np.eye(BR_W // group), np.full((group, group), 1.0 / group)), BF16)


def _lookup(table, idx):
    onehot = (idx[..., None] == jnp.arange(table.shape[0])).astype(F32)
    return jnp.dot(onehot, table, precision=lax.Precision.HIGHEST)


def _diff_bias_tables(rel_bias):
    t = DF_T
    table = rel_bias[:, :BR_HEADS].astype(F32) * LOG2E
    kl = jnp.arange(t)[:, None]
    ql = jnp.arange(t)[None, :]
    rel = jnp.stack([kl - ql + d * t for d in (-1, 0, 1)])
    tiles = _lookup(table, _t5_bucket(rel)).transpose(0, 3, 1, 2)
    far = _lookup(table, _t5_bucket(jnp.asarray([-2 * t, 2 * t], jnp.int32)))
    return tiles, far


def _win_bias_table(rel_bias):
    rel = jnp.arange(3 * WIN_BLOCK)[None, :] - WIN_BLOCK - jnp.arange(WIN_BLOCK)[:, None]
    bias = _lookup(rel_bias[:, BR_HEADS:].astype(F32) * LOG2E, _t5_bucket(rel)).transpose(2, 0, 1)
    bias = jnp.where((jnp.abs(rel) <= WIN)[None], bias, NEG)
    return bias.reshape(BR_HEADS * WIN_BLOCK, 3 * WIN_BLOCK)


def _na_bias_tables(rpb):
    depth = rpb.shape[0]
    col = np.arange(GRID_W)
    cs = np.clip(col - NA_KW // 2, 0, GRID_W - NA_KW)
    inwin = (col[None, :] >= cs[:, None]) & (col[None, :] < cs[:, None] + NA_KW)
    dc = col[None, :] - col[:, None] + (NA_KW - 1)
    onehot = jnp.asarray(dc[None] == np.arange(2 * NA_KW - 1)[:, None, None], F32)
    toep = jnp.einsum("lhrc,cqk->lhrqk", rpb.astype(F32) * LOG2E, onehot, precision=lax.Precision.HIGHEST)
    toep = jnp.where(jnp.asarray(inwin)[None, None, None], toep, NEG)
    out = []
    for d in range(NA_KH):
        lo = NA_KH - 1 - d
        b = toep[:, :, lo:lo + NA_KH].transpose(0, 1, 3, 2, 4)
        out.append(b.reshape(depth, BR_HEADS * GRID_W, NA_KH * GRID_W))
    return jnp.stack(out, axis=1)


def _tile_gain(g, reps, scale=1.0):
    return jnp.tile(g.astype(F32), reps)[None, :] * scale


def kernel(x_prompt, x_sample, mem_prompt, mem_sample, norm_g, mem_norm_g, w_in, w_mem_kv, rel_bias, hgrn_lb,
           hgrn_norm_g, diff_qk_g, diff_lambda, diff_subln_g, win_qk_g, win_sink, na_qk_g, na_rpb, mem_qk_g,
           w_branch, w_out):
    depth = w_in.shape[0]
    bp, lp, _ = x_prompt.shape
    bs, ls, _ = x_sample.shape
    mem_len = mem_prompt.shape[1]
    layout = _Layout(bp, lp, bs, ls)
    t = layout.total
    assert t % PROJ_TM == 0 and lp % DF_T == 0 and ls % DF_T == 0
    assert lp % (NA_RB * GRID_W) == 0 and ls % (NA_RB * GRID_W) == 0 and mem_len % MEM_TM == 0
    assert lp % (WIN_NB * WIN_BLOCK) == 0 and ls % (WIN_NB * WIN_BLOCK) == 0

    x = jnp.concatenate([x_prompt.reshape(bp * lp, D_MODEL), x_sample.reshape(bs * ls, D_MODEL)], axis=0)
    mem = jnp.concatenate([mem_prompt.reshape(bp * mem_len, D_MODEL), mem_sample.reshape(bs * mem_len, D_MODEL)], axis=0)

    sm = jax.nn.softmax(hgrn_lb.astype(F32), axis=1)
    lb_all = jnp.clip(jnp.cumsum(sm, axis=1) - sm[:, :1], 0.0, 1.0 - 1e-6)
    lam_init = jnp.asarray([0.8 - 0.6 * math.exp(-0.3 * l) for l in range(depth)], F32)
    lam = diff_lambda.astype(F32)
    lmb = jnp.exp(jnp.sum(lam[:, 0] * lam[:, 1], axis=-1)) - jnp.exp(jnp.sum(lam[:, 2] * lam[:, 3], axis=-1)) + lam_init

    w_in_b = w_in.astype(BF16)

    def expand_kv(w):
        w = w.reshape(depth, D_MODEL, WIN_KV_HEADS, HEAD_DIM)
        return jnp.repeat(w, BR_HEADS // WIN_KV_HEADS, axis=2).reshape(depth, D_MODEL, BR_W)

    w1 = jnp.concatenate([w_in_b[:, :, A_Q:A_G], w_in_b[:, :, B_Q:B_G], w_in_b[:, :, C_Q:C_K],
                          expand_kv(w_in_b[:, :, C_K:C_V]), expand_kv(w_in_b[:, :, C_V:C_G]),
                          w_in_b[:, :, D_Q:D_G], w_in_b[:, :, E_Q:E_G]], axis=-1)
    wg = jnp.concatenate([w_in_b[:, :, A_G:B_Q], w_in_b[:, :, B_G:C_Q], w_in_b[:, :, C_G:D_Q],
                          w_in_b[:, :, D_G:E_Q], w_in_b[:, :, E_G:M_G]], axis=-1)
    wm = w_in_b[:, :, M_G:]
    wb = w_branch.astype(BF16)
    wo = w_out.astype(BF16)
    wmem = w_mem_kv.astype(BF16)

    sc_b = DF_DK ** -0.5 * LOG2E
    sc_h = HEAD_DIM ** -0.5 * LOG2E
    zrow = jnp.zeros((depth, 1, BR_W), F32)

    def per_layer(fn):
        return jnp.stack([fn(l) for l in range(depth)])

    ep = jnp.concatenate([
        per_layer(lambda l: _tile_gain(diff_qk_g[l, 0], 8, sc_b)),
        per_layer(lambda l: _tile_gain(diff_qk_g[l, 1], 8)),
        per_layer(lambda l: _tile_gain(win_qk_g[l, 0], 4, sc_h)),
        per_layer(lambda l: _tile_gain(win_qk_g[l, 1], 4)),
        per_layer(lambda l: _tile_gain(na_qk_g[l, 0], 4, sc_h)),
        per_layer(lambda l: _tile_gain(na_qk_g[l, 1], 4)),
        per_layer(lambda l: _tile_gain(mem_qk_g[l, 0], 4, sc_h)),
        lb_all[0][:, None, :], lb_all[1][:, None, :]] + [zrow] * 7, axis=1)
    gk_mem = per_layer(lambda l: _tile_gain(mem_qk_g[l, 1], 4))
    gn_hg = hgrn_norm_g.astype(F32)[:, None, :]
    gcol = per_layer(lambda l: (jnp.tile(diff_subln_g[l].astype(F32), BR_HEADS) * (1.0 - lam_init[l]))[:, None])
    sink = per_layer(lambda l: jnp.repeat(win_sink[l].astype(F32) * LOG2E, WIN_BLOCK)[:, None])
    na_bias = _na_bias_tables(na_rpb)

    diff_bias, diff_far = _diff_bias_tables(rel_bias)
    win_bias = _win_bias_table(rel_bias)
    sc = jnp.concatenate([jnp.broadcast_to(diff_far[None], (depth, 2, BR_HEADS)),
                          jnp.broadcast_to(lmb[:, None, None], (depth, 1, BR_HEADS))], axis=1)

    g32 = _group_matrix(DF_DK)
    g64 = _group_matrix(HEAD_DIM)
    consts_f = tuple(jnp.asarray(a, dt) for a, dt in zip(_hgrn_constants(HG_C, False), (BF16, F32, F32)))
    consts_b = tuple(jnp.asarray(a, dt) for a, dt in zip(_hgrn_constants(HG_C, True), (BF16, F32, F32)))

    def layer(x, p):
        ng = p["ng"]
        (a_q, a_v, a_lf, a_kk, b_q, b_k, b_v, c_q, c_k, c_v, d_q, d_k, d_v, e_q) = _project(
            x, ng, p["w1"], p["ep"], g32, g64)
        mk, mv = _mem_kv(mem, p["mng"], p["wmem"], p["gk_mem"], g64)
        o_a = _hgrn(layout, a_q, a_v, a_kk, a_lf, consts_f, consts_b, g64, p["gn_hg"])
        o_b = _diff_attn(layout, b_q, b_k, b_v, diff_bias, p["sc"], p["gcol"])
        o_c = _win_attn(layout, c_q, c_k, c_v, win_bias, p["sink"])
        o_d = _na_attn(layout, d_q, d_k, d_v, p["na_bias"])
        o_e = _mem_attn(layout, e_q, mk.reshape(layout.nseq, mem_len, BR_W), mv.reshape(layout.nseq, mem_len, BR_W))
        y = _merge(x, ng, (o_a, o_b, o_c, o_d, o_e), p["wg"], p["wm"], p["wb"], p["wo"])
        return y, None

    params = dict(ng=norm_g.astype(F32)[:, None, :], mng=mem_norm_g.astype(F32)[:, None, :], w1=w1, ep=ep,
                  wmem=wmem, gk_mem=gk_mem, gn_hg=gn_hg, sc=sc, gcol=gcol, sink=sink, na_bias=na_bias,
                  wg=wg, wm=wm, wb=wb, wo=wo)
    x, _ = lax.scan(layer, x, params)
    y_prompt = x[:layout.off1].reshape(bp, lp, D_MODEL)
    y_sample = x[layout.off1:].reshape(bs, ls, D_MODEL)
    return (y_prompt, y_sample)
```

```python
import functools
import math

import numpy as np
import jax
import jax.numpy as jnp
from jax import lax
from jax.experimental import pallas as pl
from jax.experimental.pallas import tpu as pltpu

F32 = jnp.float32
BF16 = jnp.bfloat16

D_MODEL = 1024
HEAD_DIM = 64
BR_HEADS = 4
BR_W = BR_HEADS * HEAD_DIM
N_BRANCH = 5
DF_DK = HEAD_DIM // 2
WIN = 128
WIN_BLOCK = 128
WIN_KV_HEADS = 2
GRID_W = 64
NA_KH = 8
NA_KW = 16
N_BUCKETS = 32
MAX_DIST = 128
EPS = 1e-6
NEG = -1e30
LB_FLOOR = 1e-30
LOG2E = 1.4426950408889634

A_Q = 0
A_G = 4 * BR_W
B_Q = A_G + BR_W
B_G = B_Q + 3 * BR_W
C_Q = B_G + BR_W
C_K = C_Q + BR_W
C_V = C_K + WIN_KV_HEADS * HEAD_DIM
C_G = C_V + WIN_KV_HEADS * HEAD_DIM
D_Q = C_G + BR_W
D_G = D_Q + 3 * BR_W
E_Q = D_G + BR_W
E_G = E_Q + BR_W
M_G = E_G + BR_W

VMEM_LIMIT_BYTES = 56 * 1024 * 1024

PROJ_TM = 512
MERGE_TM = 256
MEM_TM = 256
HG_C = 128
DF_T = 256
NA_RB = 8
WIN_NB = 4
N_PROJ_SEG = 14
DF_VROWS = 80
DF_RC = 64
DF_UNROLL = 4


def _params(sem):
    return pltpu.CompilerParams(dimension_semantics=sem, vmem_limit_bytes=VMEM_LIMIT_BYTES)


def _const_spec(shape):
    nd = len(shape)
    return pl.BlockSpec(shape, lambda *_: (0,) * nd, pipeline_mode=pl.Buffered(1))


def _dot(a, b):
    return jnp.dot(a, b, preferred_element_type=F32)


def _dot_nt(a, b):
    return lax.dot_general(a, b, (((1,), (1,)), ((), ())), preferred_element_type=F32)


def _rms_rows(x, g):
    ms = jnp.mean(x * x, axis=-1, keepdims=True)
    return x * lax.rsqrt(ms + EPS) * g


def _group_rms(x, gmat):
    x2 = x * x
    hi = x2.astype(BF16)
    lo = (x2 - hi.astype(F32)).astype(BF16)
    ms = _dot(hi, gmat) + _dot(lo, gmat)
    return x * lax.rsqrt(ms + EPS)


def _head_mask(width=BR_W):
    lane = lax.broadcasted_iota(jnp.int32, (1, width), 1)
    return [(lane >= h * HEAD_DIM) & (lane < (h + 1) * HEAD_DIM) for h in range(BR_HEADS)]


def _stack_heads(q, masks):
    zero = jnp.zeros_like(q)
    return jnp.concatenate([jnp.where(m, q, zero) for m in masks], axis=0)


def _unstack_heads(o_all, masks, m):
    out = jnp.zeros((m, BR_W), F32)
    for h, mk in enumerate(masks):
        out = out + jnp.where(mk, o_all[h * m:(h + 1) * m, :], 0.0)
    return out


class _Layout:
    def __init__(self, n_prompt, len_prompt, n_sample, len_sample):
        self.classes = ((0, n_prompt, len_prompt), (n_prompt * len_prompt, n_sample, len_sample))
        self.off1 = n_prompt * len_prompt
        self.lp = len_prompt
        self.ls = len_sample
        self.total = self.off1 + n_sample * len_sample
        self.nseq = n_prompt + n_sample
        self.n_prompt = n_prompt
        assert self.off1 % len_sample == 0

    def seq_bounds(self, tok):
        in_p = tok < self.off1
        start_p = (tok // self.lp) * self.lp
        start_s = self.off1 + ((tok - self.off1) // self.ls) * self.ls
        return jnp.where(in_p, start_p, start_s), jnp.where(in_p, self.lp, self.ls)

    def seq_index(self, tok):
        return jnp.where(tok < self.off1, tok // self.lp, self.n_prompt + (tok - self.off1) // self.ls)


def _proj_body(x_ref, ng_ref, w_ref, wvt_ref, ep_ref, g32_ref, g64_ref,
               aq_ref, av_ref, alf_ref, akk_ref, bq_ref, bk_ref, bv_ref,
               cq_ref, ck_ref, cv_ref, dq_ref, dk_ref, dv_ref, eq_ref):
    h = _rms_rows(x_ref[...], ng_ref[...]).astype(BF16)
    ep = ep_ref[...]

    def seg(i):
        return _dot(h, w_ref[:, i * BR_W:(i + 1) * BR_W])

    def row(r):
        return ep[r:r + 1, :]

    aq = seg(0)
    aq_ref[...] = (aq * jax.nn.sigmoid(aq)).astype(BF16)
    av_ref[...] = seg(1).astype(BF16)
    for d in range(2):
        z = seg(2 + d)
        lb = row(7 + d)
        e = jnp.exp(-jnp.abs(z))
        log_sig = jnp.minimum(z, 0.0) - jnp.log1p(e)
        t1 = jnp.log(jnp.maximum(lb, LB_FLOOR))
        t2 = jnp.log1p(-lb) + log_sig
        logf = jnp.maximum(t1, t2) + jnp.log1p(jnp.exp(-jnp.abs(t1 - t2)))
        sig_neg = jnp.where(z >= 0.0, e, 1.0) / (1.0 + e)
        alf_ref[:, d * BR_W:(d + 1) * BR_W] = logf
        akk_ref[:, d * BR_W:(d + 1) * BR_W] = ((1.0 - lb) * sig_neg).astype(BF16)
    g32 = g32_ref[...]
    g64 = g64_ref[...]
    bq_ref[...] = (_group_rms(seg(4), g32) * row(0)).astype(BF16)
    bk_ref[...] = (_group_rms(seg(5), g32) * row(1)).astype(BF16)
    vt = _dot_nt(wvt_ref[...], h).astype(BF16)
    pad_row = lax.broadcasted_iota(jnp.int32, (DF_VROWS - HEAD_DIM, DF_T), 0)
    pad = jnp.where(pad_row == 0, 1.0, 0.0).astype(BF16)
    for j in range(PROJ_TM // DF_T):
        for hd in range(BR_HEADS):
            bv_ref[j, hd, 0:HEAD_DIM, :] = vt[hd * HEAD_DIM:(hd + 1) * HEAD_DIM, j * DF_T:(j + 1) * DF_T]
            bv_ref[j, hd, HEAD_DIM:DF_VROWS, :] = pad
    cq_ref[...] = (_group_rms(seg(7), g64) * row(2)).astype(BF16)
    ck_ref[...] = (_group_rms(seg(8), g64) * row(3)).astype(BF16)
    cv_ref[...] = seg(9).astype(BF16)
    dq_ref[...] = (_group_rms(seg(10), g64) * row(4)).astype(BF16)
    dk_ref[...] = (_group_rms(seg(11), g64) * row(5)).astype(BF16)
    dv_ref[...] = seg(12).astype(BF16)
    eq_ref[...] = (_group_rms(seg(13), g64) * row(6)).astype(BF16)


def _project(x, ng, w1, wvt, ep, g32, g64):
    t = x.shape[0]
    tm = PROJ_TM
    tile = lambda w: pl.BlockSpec((tm, w), lambda i: (i, 0))
    widths = [BR_W, BR_W, 2 * BR_W, 2 * BR_W] + [BR_W] * 10
    dtypes = [BF16, BF16, F32, BF16] + [BF16] * 10
    out_specs = [tile(w) for w in widths]
    out_shape = [jax.ShapeDtypeStruct((t, w), dt) for w, dt in zip(widths, dtypes)]
    kt = tm // DF_T
    out_specs[6] = pl.BlockSpec((kt, BR_HEADS, DF_VROWS, DF_T), lambda i: (i, 0, 0, 0))
    out_shape[6] = jax.ShapeDtypeStruct((t // DF_T, BR_HEADS, DF_VROWS, DF_T), BF16)
    return pl.pallas_call(
        _proj_body,
        grid=(t // tm,),
        in_specs=[tile(D_MODEL), _const_spec((1, D_MODEL)), _const_spec(w1.shape), _const_spec(wvt.shape),
                  _const_spec(ep.shape), _const_spec(g32.shape), _const_spec(g64.shape)],
        out_specs=out_specs,
        out_shape=out_shape,
        compiler_params=_params(("parallel",)),
        name="proj",
    )(x, ng, w1, wvt, ep, g32, g64)


def _memkv_body(m_ref, g_ref, w_ref, gk_ref, g64_ref, mk_ref, mv_ref):
    mh = _rms_rows(m_ref[...], g_ref[...]).astype(BF16)
    kv = _dot(mh, w_ref[...])
    mk_ref[...] = (_group_rms(kv[:, :BR_W], g64_ref[...]) * gk_ref[...]).astype(BF16)
    mv_ref[...] = kv[:, BR_W:].astype(BF16)


def _mem_kv(mem, g, w, gk, g64):
    t = mem.shape[0]
    tm = 256
    return pl.pallas_call(
        _memkv_body,
        grid=(t // tm,),
        in_specs=[pl.BlockSpec((tm, D_MODEL), lambda i: (i, 0)), _const_spec((1, D_MODEL)), _const_spec(w.shape),
                  _const_spec((1, BR_W)), _const_spec(g64.shape)],
        out_specs=[pl.BlockSpec((tm, BR_W), lambda i: (i, 0))] * 2,
        out_shape=[jax.ShapeDtypeStruct((t, BR_W), BF16)] * 2,
        compiler_params=_params(("parallel",)),
        name="memkv",
    )(mem, g, w, gk, g64)


def _hgrn_constants(c, reverse):
    nl = int(math.log2(c))
    idx = np.arange(c)
    t = idx[:, None]
    u = idx[None, :]
    incl = (u <= t).astype(np.float32)
    rest = (u > t).astype(np.float32)
    tot = np.ones((8, c), np.float32)
    mds, mes, lms = [], [], [np.eye(c, dtype=np.float32)]
    for lev in range(nl):
        w = 1 << lev
        blk = idx // w
        odd = (blk % 2 == 1)
        md = (odd[:, None] & (u >= (blk * w)[:, None]) & (u <= t)).astype(np.float32)
        me = ((~odd)[:, None] & (u > t) & (u <= ((blk + 1) * w - 1)[:, None])).astype(np.float32)
        lm = (odd[:, None] & (blk[None, :] == (blk - 1)[:, None])).astype(np.float32)
        mds.append(md)
        mes.append(me)
        lms.append(lm)
    mats = [incl, rest, tot] + [md + me for md, me in zip(mds, mes)]
    if reverse:
        mats = [m[::-1, ::-1] for m in mats]
        lms = [m[::-1, ::-1] for m in lms]
    mall = np.concatenate(mats, axis=0)
    lmst = np.stack([np.tile(m, (BR_HEADS, 1)) for m in lms])
    bd = np.kron(np.eye(BR_HEADS, dtype=np.float32), np.ones((HEAD_DIM, HEAD_DIM), np.float32))
    return mall, lmst, bd


def _hgrn_chunk(layout, chunk, reverse, q_ref, k_ref, v_ref, lf_ref, mall_ref, lm_ref, bd_ref, o_ref, st_ref):
    c = HG_C
    nl = int(math.log2(c))
    tok = chunk * c
    start, length = layout.seq_bounds(tok)
    fresh = (tok + c == start + length) if reverse else (tok == start)

    masks = _head_mask()
    lf = lf_ref[...]
    hi = lf.astype(BF16)
    mid = (lf - hi.astype(F32)).astype(BF16)
    mall = mall_ref[...]
    cums = _dot(mall, hi) + _dot(mall, mid)
    b = cums[0:c, :]
    b_rest = cums[c:2 * c, :]
    b_tot = cums[2 * c:2 * c + 1, :]
    base = 2 * c + 8

    qf = q_ref[...].astype(F32)
    kf = k_ref[...].astype(F32)
    v = v_ref[...]
    st = jnp.where(fresh, 0.0, st_ref[...])

    o = _dot_nt((qf * jnp.exp(b)).astype(BF16), st.astype(BF16))
    a_all = _dot_nt(_stack_heads(q_ref[...], masks), k_ref[...]) * lm_ref[0]
    for lev in range(nl):
        dec = jnp.exp(cums[base + lev * c:base + (lev + 1) * c, :])
        ql = (qf * dec).astype(BF16)
        kl = (kf * dec).astype(BF16)
        a_all = a_all + _dot_nt(_stack_heads(ql, masks), kl) * lm_ref[lev + 1]
    a_bf = a_all.astype(BF16)
    zero = jnp.zeros_like(v)
    for h, mk in enumerate(masks):
        o = o + _dot(a_bf[h * c:(h + 1) * c, :], jnp.where(mk, v, zero))
    o_ref[...] = o

    kst = (kf * jnp.exp(b_rest)).astype(BF16)
    vt = v.astype(F32).T.astype(BF16)
    st_ref[...] = (st * jnp.exp(b_tot) + _dot(vt, kst)) * bd_ref[...]


def _hgrn_body(layout, nct, qf_ref, kf_ref, vf_ref, lff_ref, qb_ref, kb_ref, vb_ref, lfb_ref,
               mallf_ref, lmf_ref, mallb_ref, lmb_ref, bd_ref, of_ref, ob_ref, stf_ref, stb_ref):
    i = pl.program_id(0)
    _hgrn_chunk(layout, i, False, qf_ref, kf_ref, vf_ref, lff_ref, mallf_ref, lmf_ref, bd_ref, of_ref, stf_ref)
    _hgrn_chunk(layout, nct - 1 - i, True, qb_ref, kb_ref, vb_ref, lfb_ref, mallb_ref, lmb_ref, bd_ref,
                ob_ref, stb_ref)


def _hgrn_norm_body(of_ref, ob_ref, g64_ref, gn_ref, o_ref):
    o_ref[...] = (_group_rms(of_ref[...] + ob_ref[...], g64_ref[...]) * gn_ref[...]).astype(BF16)


def _hgrn(layout, qs, v, kk, logf, consts_f, consts_b, g64, gn):
    t = qs.shape[0]
    c = HG_C
    nct = t // c
    mall_f, lm_f, bd = consts_f
    mall_b, lm_b, _ = consts_b

    def specs(reverse):
        cm = (lambda i: nct - 1 - i) if reverse else (lambda i: i)
        d = 1 if reverse else 0
        tile = pl.BlockSpec((c, BR_W), lambda i: (cm(i), 0))
        half = pl.BlockSpec((c, BR_W), lambda i: (cm(i), d))
        return tile, half

    tile_f, half_f = specs(False)
    tile_b, half_b = specs(True)
    o_f, o_b = pl.pallas_call(
        functools.partial(_hgrn_body, layout, nct),
        grid=(nct,),
        in_specs=[tile_f, half_f, tile_f, half_f, tile_b, half_b, tile_b, half_b,
                  _const_spec(mall_f.shape), _const_spec(lm_f.shape), _const_spec(mall_b.shape),
                  _const_spec(lm_b.shape), _const_spec(bd.shape)],
        out_specs=[tile_f, tile_b],
        out_shape=[jax.ShapeDtypeStruct((t, BR_W), F32)] * 2,
        scratch_shapes=[pltpu.VMEM((BR_W, BR_W), F32)] * 2,
        compiler_params=_params(("arbitrary",)),
        name="hgrn_scan",
    )(qs, kk, v, logf, qs, kk, v, logf, mall_f, lm_f, mall_b, lm_b, bd)

    tm = PROJ_TM
    tok = pl.BlockSpec((tm, BR_W), lambda i: (i, 0))
    return pl.pallas_call(
        _hgrn_norm_body,
        grid=(t // tm,),
        in_specs=[tok, tok, _const_spec(g64.shape), _const_spec((1, BR_W))],
        out_specs=tok,
        out_shape=jax.ShapeDtypeStruct((t, BR_W), BF16),
        compiler_params=_params(("parallel",)),
        name="hgrn_norm",
    )(o_f, o_b, g64, gn)


def _diff_body(nkt, q_ref, k_ref, vt_ref, bias_ref, sc_ref, gcol_ref, o_ref, qm_ref, m_ref, acc_ref,
               s0_ref, s1_ref, mx0_ref, mx1_ref, p_ref):
    tq = DF_T
    qi = pl.program_id(1)
    qt = q_ref[...].astype(F32).T.astype(BF16)
    rowid = lax.broadcasted_iota(jnp.int32, (BR_W, tq), 0)
    for hc in range(2 * BR_HEADS):
        sel = (rowid >= hc * DF_DK) & (rowid < (hc + 1) * DF_DK)
        qm_ref[hc] = jnp.where(sel, qt, jnp.zeros_like(qt))
    m_ref[...] = jnp.full(m_ref.shape, NEG, F32)
    acc_ref[...] = jnp.zeros_like(acc_ref)

    nu = 2 * BR_HEADS
    rows = [(r, r + DF_RC) for r in range(0, tq, DF_RC)]

    def qk_unit(kj, hc, buf, near=None):
        sbuf, mxbuf = buf
        s = _dot(k_ref[kj], qm_ref[hc])
        if near is not None:
            s = s + bias_ref[near, hc // 2]
        sbuf[hc] = s
        mxbuf[hc] = jnp.max(s.reshape(tq // 8, 8, tq), axis=0)

    def softmax_pv_unit(kj, hc, buf, const_bias):
        sbuf, mxbuf = buf
        c = const_bias(hc // 2)
        m_old = m_ref[hc]
        m_new = jnp.maximum(m_old, jnp.max(mxbuf[hc], axis=0, keepdims=True) + c)
        shift = m_new - c
        for r0, r1 in rows:
            p_ref[hc, r0:r1, :] = jnp.exp2(sbuf[hc, r0:r1, :] - shift).astype(BF16)
        m_ref[hc] = m_new
        acc_ref[hc] = acc_ref[hc] * jnp.exp2(m_old - m_new) + _dot(vt_ref[kj, hc // 2], p_ref[hc])

    def step(kj, buf, const_bias, nxt):
        for hc in range(nu):
            qk_unit(nxt[0], hc, nxt[1], nxt[2])
            softmax_pv_unit(kj, hc, buf, const_bias)

    last = nkt - 1
    n_low = jnp.maximum(qi - 1, 0)
    n_far = n_low + jnp.maximum(nkt - qi - 2, 0)
    buf_a = (s0_ref, mx0_ref)
    buf_b = (s1_ref, mx1_ref)

    def far_tile(f):
        return jnp.minimum(jnp.where(f < n_low, f, f - n_low + qi + 2), last)

    def dead_if(cond):
        pen = jnp.where(cond, 2.0 * NEG, 0.0)
        return lambda h: pen

    near = [jnp.clip(qi + d - 1, 0, last) for d in range(3)]
    for hc in range(nu):
        qk_unit(near[0], hc, buf_a, 0)
    step(near[0], buf_a, dead_if(qi == 0), (near[1], buf_b, 1))
    step(near[1], buf_b, dead_if(False), (near[2], buf_a, 2))
    step(near[2], buf_a, dead_if(qi == last), (far_tile(0), buf_b, None))

    def far_step(f, cur, nxt):
        kj = far_tile(f)
        row = jnp.where(kj < qi, 0, 1)
        live = f < n_far
        step(kj, cur, lambda h: jnp.where(live, sc_ref[row, h], 2.0 * NEG), (far_tile(f + 1), nxt, None))

    far_step(0, buf_b, buf_a)

    def body(i, carry):
        f0 = 1 + DF_UNROLL * i
        for u in range(DF_UNROLL):
            cur, nxt = (buf_a, buf_b) if u % 2 == 0 else (buf_b, buf_a)
            far_step(f0 + u, cur, nxt)
        return carry
    lax.fori_loop(0, (n_far - 1 + DF_UNROLL - 1) // DF_UNROLL, body, 0)

    lmb = sc_ref[2, 0]
    outs = []
    for h in range(BR_HEADS):
        a0 = acc_ref[2 * h]
        a1 = acc_ref[2 * h + 1]
        o0 = a0[:HEAD_DIM, :] / a0[HEAD_DIM:HEAD_DIM + 1, :]
        o1 = a1[:HEAD_DIM, :] / a1[HEAD_DIM:HEAD_DIM + 1, :]
        o = o0 - lmb * o1
        ms = jnp.mean(o * o, axis=0, keepdims=True)
        outs.append(o * lax.rsqrt(ms + EPS))
    ot = jnp.concatenate(outs, axis=0) * gcol_ref[...]
    o_ref[...] = ot.T.astype(BF16)


def _diff_attn_class(off, nseq, n, qt, k3, vt4, bias, sc, gcol):
    t = DF_T
    nkt = n // t
    qb = off // t
    sb = off // n
    return pl.pallas_call(
        functools.partial(_diff_body, nkt),
        grid=(nseq, nkt),
        in_specs=[pl.BlockSpec((t, BR_W), lambda s, i: (qb + s * nkt + i, 0)),
                  pl.BlockSpec((nkt, t, BR_W), lambda s, i: (sb + s, 0, 0), pipeline_mode=pl.Buffered(1)),
                  pl.BlockSpec((nkt, BR_HEADS, DF_VROWS, t), lambda s, i: (sb + s, 0, 0, 0),
                               pipeline_mode=pl.Buffered(1)),
                  _const_spec(bias.shape),
                  pl.BlockSpec(memory_space=pltpu.SMEM),
                  _const_spec(gcol.shape)],
        out_specs=pl.BlockSpec((t, BR_W), lambda s, i: (s * nkt + i, 0)),
        out_shape=jax.ShapeDtypeStruct((nseq * n, BR_W), BF16),
        scratch_shapes=[pltpu.VMEM((2 * BR_HEADS, BR_W, t), BF16),
                        pltpu.VMEM((2 * BR_HEADS, 1, t), F32),
                        pltpu.VMEM((2 * BR_HEADS, DF_VROWS, t), F32),
                        pltpu.VMEM((2 * BR_HEADS, t, t), F32),
                        pltpu.VMEM((2 * BR_HEADS, t, t), F32),
                        pltpu.VMEM((2 * BR_HEADS, 8, t), F32),
                        pltpu.VMEM((2 * BR_HEADS, 8, t), F32),
                        pltpu.VMEM((2 * BR_HEADS, t, t), BF16)],
        compiler_params=_params(("parallel", "parallel")),
        name="diff_attn",
    )(qt, k3, vt4, bias, sc, gcol)


def _diff_attn(layout, q, k, vt4, bias, sc, gcol):
    t = k.shape[0]
    k3 = k.reshape(t // DF_T, DF_T, BR_W)
    outs = [_diff_attn_class(off, nseq, n, q, k3, vt4, bias, sc, gcol) for off, nseq, n in layout.classes]
    return jnp.concatenate(outs, axis=0)


def _win_body(layout, q_ref, kp_ref, kc_ref, kn_ref, vp_ref, vc_ref, vn_ref, bias_ref, sink_ref, o_ref):
    blk = WIN_BLOCK
    i = pl.program_id(0)
    tok = i * (WIN_NB * blk)
    start, length = layout.seq_bounds(tok)
    masks = _head_mask()
    kcat = jnp.concatenate([kp_ref[...], kc_ref[...], kn_ref[...]], axis=0)
    vcat = jnp.concatenate([vp_ref[...], vc_ref[...], vn_ref[...]], axis=0)
    col = lax.broadcasted_iota(jnp.int32, (1, 3 * blk), 1)
    sink = sink_ref[...]
    bias = bias_ref[...]
    for j in range(WIN_NB):
        has_prev = tok + j * blk > start
        has_next = tok + (j + 1) * blk < start + length
        dead = ((col < blk) & jnp.logical_not(has_prev)) | ((col >= 2 * blk) & jnp.logical_not(has_next))
        qst = _stack_heads(q_ref[j * blk:(j + 1) * blk, :], masks)
        s = _dot_nt(qst, kcat[j * blk:(j + 3) * blk, :]) + bias + jnp.where(dead, NEG, 0.0)
        m = jnp.maximum(jnp.max(s, axis=-1, keepdims=True), sink)
        p = jnp.exp2(s - m)
        den = jnp.sum(p, axis=-1, keepdims=True) + jnp.exp2(sink - m)
        pn = (p * (1.0 / den)).astype(BF16)
        o_all = _dot(pn, vcat[j * blk:(j + 3) * blk, :])
        o_ref[j * blk:(j + 1) * blk, :] = _unstack_heads(o_all, masks, blk).astype(BF16)


def _win_attn(layout, q, k, v, bias, sink):
    t = q.shape[0]
    blk = WIN_BLOCK
    nb = t // blk
    cur = pl.BlockSpec((WIN_NB * blk, BR_W), lambda i: (i, 0))
    prev = pl.BlockSpec((blk, BR_W), lambda i: (jnp.maximum(i * WIN_NB - 1, 0), 0))
    nxt = pl.BlockSpec((blk, BR_W), lambda i: (jnp.minimum((i + 1) * WIN_NB, nb - 1), 0))
    return pl.pallas_call(
        functools.partial(_win_body, layout),
        grid=(nb // WIN_NB,),
        in_specs=[cur, prev, cur, nxt, prev, cur, nxt, _const_spec(bias.shape), _const_spec(sink.shape)],
        out_specs=cur,
        out_shape=jax.ShapeDtypeStruct((t, BR_W), BF16),
        compiler_params=_params(("parallel",)),
        name="win_attn",
    )(q, k, k, k, v, v, v, bias, sink)


def _na_body(rows, q_ref, k_ref, v_ref, bias_ref, o_ref):
    j = pl.program_id(1)
    masks = _head_mask()
    nk = NA_KH * GRID_W
    for i in range(NA_RB):
        r = j * NA_RB + i
        rs = jnp.clip(r - NA_KH // 2, 0, rows - NA_KH)
        koff = pl.multiple_of(rs * GRID_W, GRID_W)
        q = q_ref[i * GRID_W:(i + 1) * GRID_W, :]
        ks = k_ref[pl.ds(koff, nk), :]
        vs = v_ref[pl.ds(koff, nk), :]
        s = _dot_nt(_stack_heads(q, masks), ks) + bias_ref[r - rs]
        m = jnp.max(s, axis=-1, keepdims=True)
        p = jnp.exp2(s - m)
        pn = (p * (1.0 / jnp.sum(p, axis=-1, keepdims=True))).astype(BF16)
        o_ref[i * GRID_W:(i + 1) * GRID_W, :] = _unstack_heads(_dot(pn, vs), masks, GRID_W).astype(BF16)


def _na_class(off, nseq, n, q, k, v, bias):
    rows = n // GRID_W
    qt = NA_RB * GRID_W
    nj = n // qt
    qb = off // qt
    sb = off // n
    seq = pl.BlockSpec((n, BR_W), lambda s, j: (sb + s, 0), pipeline_mode=pl.Buffered(1))
    return pl.pallas_call(
        functools.partial(_na_body, rows),
        grid=(nseq, nj),
        in_specs=[pl.BlockSpec((qt, BR_W), lambda s, j: (qb + s * nj + j, 0)), seq, seq, _const_spec(bias.shape)],
        out_specs=pl.BlockSpec((qt, BR_W), lambda s, j: (s * nj + j, 0)),
        out_shape=jax.ShapeDtypeStruct((nseq * n, BR_W), BF16),
        compiler_params=_params(("parallel", "parallel")),
        name="na_attn",
    )(q, k, v, bias)


def _na_attn(layout, q, k, v, bias):
    return jnp.concatenate([_na_class(off, nseq, n, q, k, v, bias) for off, nseq, n in layout.classes], axis=0)


def _mem_body(q_ref, mk_ref, mv_ref, o_ref):
    masks = _head_mask()
    s = _dot_nt(_stack_heads(q_ref[...], masks), mk_ref[0])
    m = jnp.max(s, axis=-1, keepdims=True)
    p = jnp.exp2(s - m)
    pn = (p * (1.0 / jnp.sum(p, axis=-1, keepdims=True))).astype(BF16)
    o_ref[...] = _unstack_heads(_dot(pn, mv_ref[0]), masks, MEM_TM).astype(BF16)


def _mem_attn(layout, q, mk, mv):
    t = q.shape[0]
    tm = MEM_TM
    mem_len = mk.shape[1]
    tile = pl.BlockSpec((tm, BR_W), lambda i: (i, 0))
    mem = pl.BlockSpec((1, mem_len, BR_W), lambda i: (layout.seq_index(i * tm), 0, 0))
    return pl.pallas_call(
        _mem_body,
        grid=(t // tm,),
        in_specs=[tile, mem, mem],
        out_specs=tile,
        out_shape=jax.ShapeDtypeStruct((t, BR_W), BF16),
        compiler_params=_params(("parallel",)),
        name="mem_attn",
    )(q, mk, mv)


def _merge_body(x_ref, ng_ref, oa_ref, ob_ref, oc_ref, od_ref, oe_ref, wg_ref, wm_ref, wb_ref, wo_ref, y_ref):
    x = x_ref[...]
    h = _rms_rows(x, ng_ref[...]).astype(BF16)
    merged = jnp.zeros((x.shape[0], D_MODEL), F32)
    for kb, o_ref in enumerate((oa_ref, ob_ref, oc_ref, od_ref, oe_ref)):
        g = _dot(h, wg_ref[:, kb * BR_W:(kb + 1) * BR_W])
        br = (o_ref[...].astype(F32) * (g * jax.nn.sigmoid(g))).astype(BF16)
        mg = jax.nn.sigmoid(_dot(h, wm_ref[:, kb * D_MODEL:(kb + 1) * D_MODEL]))
        merged = merged + mg * _dot(br, wb_ref[kb])
    y_ref[...] = x + _dot(merged.astype(BF16), wo_ref[...])


def _merge(x, ng, branches, wg, wm, wb, wo):
    t = x.shape[0]
    tm = MERGE_TM
    xt = pl.BlockSpec((tm, D_MODEL), lambda i: (i, 0))
    bt = pl.BlockSpec((tm, BR_W), lambda i: (i, 0))
    return pl.pallas_call(
        _merge_body,
        grid=(t // tm,),
        in_specs=[xt, _const_spec((1, D_MODEL))] + [bt] * N_BRANCH
        + [_const_spec(wg.shape), _const_spec(wm.shape), _const_spec(wb.shape), _const_spec(wo.shape)],
        out_specs=xt,
        out_shape=jax.ShapeDtypeStruct((t, D_MODEL), F32),
        compiler_params=_params(("parallel",)),
        name="merge",
    )(x, ng, *branches, wg, wm, wb, wo)


def _t5_bucket(rel):
    half = N_BUCKETS // 2
    exact = half // 2
    n = jnp.abs(rel)
    nf = jnp.maximum(n, 1).astype(F32)
    large = exact + (jnp.log(nf / exact) / math.log(MAX_DIST / exact) * (half - exact)).astype(jnp.int32)
    large = jnp.clip(large, 0, half - 1)
    return jnp.where(rel > 0, half, 0) + jnp.where(n < exact, n, large)


def _group_matrix(group):
    return jnp.asarray(np.kron(np.eye(BR_W // group), np.full((group, group), 1.0 / group)), BF16)


def _lookup(table, idx):
    onehot = (idx[..., None] == jnp.arange(table.shape[0])).astype(F32)
    return jnp.dot(onehot, table, precision=lax.Precision.HIGHEST)


def _diff_bias_tables(rel_bias):
    t = DF_T
    table = rel_bias[:, :BR_HEADS].astype(F32) * LOG2E
    kl = jnp.arange(t)[:, None]
    ql = jnp.arange(t)[None, :]
    rel = jnp.stack([kl - ql + d * t for d in (-1, 0, 1)])
    tiles = _lookup(table, _t5_bucket(rel)).transpose(0, 3, 1, 2)
    far = _lookup(table, _t5_bucket(jnp.asarray([-2 * t, 2 * t], jnp.int32)))
    return tiles, far


def _win_bias_table(rel_bias):
    rel = jnp.arange(3 * WIN_BLOCK)[None, :] - WIN_BLOCK - jnp.arange(WIN_BLOCK)[:, None]
    bias = _lookup(rel_bias[:, BR_HEADS:].astype(F32) * LOG2E, _t5_bucket(rel)).transpose(2, 0, 1)
    bias = jnp.where((jnp.abs(rel) <= WIN)[None], bias, NEG)
    return bias.reshape(BR_HEADS * WIN_BLOCK, 3 * WIN_BLOCK)


def _na_bias_tables(rpb):
    depth = rpb.shape[0]
    col = np.arange(GRID_W)
    cs = np.clip(col - NA_KW // 2, 0, GRID_W - NA_KW)
    inwin = (col[None, :] >= cs[:, None]) & (col[None, :] < cs[:, None] + NA_KW)
    dc = col[None, :] - col[:, None] + (NA_KW - 1)
    onehot = jnp.asarray(dc[None] == np.arange(2 * NA_KW - 1)[:, None, None], F32)
    toep = jnp.einsum("lhrc,cqk->lhrqk", rpb.astype(F32) * LOG2E, onehot, precision=lax.Precision.HIGHEST)
    toep = jnp.where(jnp.asarray(inwin)[None, None, None], toep, NEG)
    out = []
    for d in range(NA_KH):
        lo = NA_KH - 1 - d
        b = toep[:, :, lo:lo + NA_KH].transpose(0, 1, 3, 2, 4)
        out.append(b.reshape(depth, BR_HEADS * GRID_W, NA_KH * GRID_W))
    return jnp.stack(out, axis=1)


def _tile_gain(g, reps, scale=1.0):
    return jnp.tile(g.astype(F32), reps)[None, :] * scale


def kernel(x_prompt, x_sample, mem_prompt, mem_sample, norm_g, mem_norm_g, w_in, w_mem_kv, rel_bias, hgrn_lb,
           hgrn_norm_g, diff_qk_g, diff_lambda, diff_subln_g, win_qk_g, win_sink, na_qk_g, na_rpb, mem_qk_g,
           w_branch, w_out):
    depth = w_in.shape[0]
    bp, lp, _ = x_prompt.shape
    bs, ls, _ = x_sample.shape
    mem_len = mem_prompt.shape[1]
    layout = _Layout(bp, lp, bs, ls)
    t = layout.total
    assert t % PROJ_TM == 0 and lp % DF_T == 0 and ls % DF_T == 0
    assert lp % (NA_RB * GRID_W) == 0 and ls % (NA_RB * GRID_W) == 0 and mem_len % MEM_TM == 0
    assert lp % (WIN_NB * WIN_BLOCK) == 0 and ls % (WIN_NB * WIN_BLOCK) == 0

    x = jnp.concatenate([x_prompt.reshape(bp * lp, D_MODEL), x_sample.reshape(bs * ls, D_MODEL)], axis=0)
    mem = jnp.concatenate([mem_prompt.reshape(bp * mem_len, D_MODEL), mem_sample.reshape(bs * mem_len, D_MODEL)], axis=0)

    sm = jax.nn.softmax(hgrn_lb.astype(F32), axis=1)
    lb_all = jnp.clip(jnp.cumsum(sm, axis=1) - sm[:, :1], 0.0, 1.0 - 1e-6)
    lam_init = jnp.asarray([0.8 - 0.6 * math.exp(-0.3 * l) for l in range(depth)], F32)
    lam = diff_lambda.astype(F32)
    lmb = jnp.exp(jnp.sum(lam[:, 0] * lam[:, 1], axis=-1)) - jnp.exp(jnp.sum(lam[:, 2] * lam[:, 3], axis=-1)) + lam_init

    w_in_b = w_in.astype(BF16)

    def expand_kv(w):
        w = w.reshape(depth, D_MODEL, WIN_KV_HEADS, HEAD_DIM)
        return jnp.repeat(w, BR_HEADS // WIN_KV_HEADS, axis=2).reshape(depth, D_MODEL, BR_W)

    w1 = jnp.concatenate([w_in_b[:, :, A_Q:A_G], w_in_b[:, :, B_Q:B_G], w_in_b[:, :, C_Q:C_K],
                          expand_kv(w_in_b[:, :, C_K:C_V]), expand_kv(w_in_b[:, :, C_V:C_G]),
                          w_in_b[:, :, D_Q:D_G], w_in_b[:, :, E_Q:E_G]], axis=-1)
    wvt = w_in_b[:, :, B_Q + 2 * BR_W:B_G].transpose(0, 2, 1)
    wg = jnp.concatenate([w_in_b[:, :, A_G:B_Q], w_in_b[:, :, B_G:C_Q], w_in_b[:, :, C_G:D_Q],
                          w_in_b[:, :, D_G:E_Q], w_in_b[:, :, E_G:M_G]], axis=-1)
    wm = w_in_b[:, :, M_G:]
    wb = w_branch.astype(BF16)
    wo = w_out.astype(BF16)
    wmem = w_mem_kv.astype(BF16)

    sc_b = DF_DK ** -0.5 * LOG2E
    sc_h = HEAD_DIM ** -0.5 * LOG2E
    zrow = jnp.zeros((depth, 1, BR_W), F32)

    def per_layer(fn):
        return jnp.stack([fn(l) for l in range(depth)])

    ep = jnp.concatenate([
        per_layer(lambda l: _tile_gain(diff_qk_g[l, 0], 8, sc_b)),
        per_layer(lambda l: _tile_gain(diff_qk_g[l, 1], 8)),
        per_layer(lambda l: _tile_gain(win_qk_g[l, 0], 4, sc_h)),
        per_layer(lambda l: _tile_gain(win_qk_g[l, 1], 4)),
        per_layer(lambda l: _tile_gain(na_qk_g[l, 0], 4, sc_h)),
        per_layer(lambda l: _tile_gain(na_qk_g[l, 1], 4)),
        per_layer(lambda l: _tile_gain(mem_qk_g[l, 0], 4, sc_h)),
        lb_all[0][:, None, :], lb_all[1][:, None, :]] + [zrow] * 7, axis=1)
    gk_mem = per_layer(lambda l: _tile_gain(mem_qk_g[l, 1], 4))
    gn_hg = hgrn_norm_g.astype(F32)[:, None, :]
    gcol = per_layer(lambda l: (jnp.tile(diff_subln_g[l].astype(F32), BR_HEADS) * (1.0 - lam_init[l]))[:, None])
    sink = per_layer(lambda l: jnp.repeat(win_sink[l].astype(F32) * LOG2E, WIN_BLOCK)[:, None])
    na_bias = _na_bias_tables(na_rpb)

    diff_bias, diff_far = _diff_bias_tables(rel_bias)
    win_bias = _win_bias_table(rel_bias)
    sc = jnp.concatenate([jnp.broadcast_to(diff_far[None], (depth, 2, BR_HEADS)),
                          jnp.broadcast_to(lmb[:, None, None], (depth, 1, BR_HEADS))], axis=1)

    g32 = _group_matrix(DF_DK)
    g64 = _group_matrix(HEAD_DIM)
    consts_f = tuple(jnp.asarray(a, dt) for a, dt in zip(_hgrn_constants(HG_C, False), (BF16, F32, F32)))
    consts_b = tuple(jnp.asarray(a, dt) for a, dt in zip(_hgrn_constants(HG_C, True), (BF16, F32, F32)))

    def layer(x, p):
        ng = p["ng"]
        (a_q, a_v, a_lf, a_kk, b_q, b_k, b_v, c_q, c_k, c_v, d_q, d_k, d_v, e_q) = _project(
            x, ng, p["w1"], p["wvt"], p["ep"], g32, g64)
        mk, mv = _mem_kv(mem, p["mng"], p["wmem"], p["gk_mem"], g64)
        o_a = _hgrn(layout, a_q, a_v, a_kk, a_lf, consts_f, consts_b, g64, p["gn_hg"])
        o_b = _diff_attn(layout, b_q, b_k, b_v, diff_bias, p["sc"], p["gcol"])
        o_c = _win_attn(layout, c_q, c_k, c_v, win_bias, p["sink"])
        o_d = _na_attn(layout, d_q, d_k, d_v, p["na_bias"])
        o_e = _mem_attn(layout, e_q, mk.reshape(layout.nseq, mem_len, BR_W), mv.reshape(layout.nseq, mem_len, BR_W))
        y = _merge(x, ng, (o_a, o_b, o_c, o_d, o_e), p["wg"], p["wm"], p["wb"], p["wo"])
        return y, None

    params = dict(ng=norm_g.astype(F32)[:, None, :], mng=mem_norm_g.astype(F32)[:, None, :], w1=w1, wvt=wvt, ep=ep,
                  wmem=wmem, gk_mem=gk_mem, gn_hg=gn_hg, sc=sc, gcol=gcol, sink=sink, na_bias=na_bias,
                  wg=wg, wm=wm, wb=wb, wo=wo)
    x, _ = lax.scan(layer, x, params)
    y_prompt = x[:layout.off1].reshape(bp, lp, D_MODEL)
    y_sample = x[layout.off1:].reshape(bs, ls, D_MODEL)
    return (y_prompt, y_sample)
```

```python
import functools
import math

import numpy as np
import jax
import jax.numpy as jnp
from jax import lax
from jax.experimental import pallas as pl
from jax.experimental.pallas import tpu as pltpu

F32 = jnp.float32
BF16 = jnp.bfloat16

D_MODEL = 1024
HEAD_DIM = 64
BR_HEADS = 4
BR_W = BR_HEADS * HEAD_DIM
N_BRANCH = 5
DF_DK = HEAD_DIM // 2
WIN = 128
WIN_BLOCK = 128
WIN_KV_HEADS = 2
GRID_W = 64
NA_KH = 8
NA_KW = 16
N_BUCKETS = 32
MAX_DIST = 128
EPS = 1e-6
NEG = -1e30
LB_FLOOR = 1e-30
LOG2E = 1.4426950408889634

A_Q = 0
A_G = 4 * BR_W
B_Q = A_G + BR_W
B_G = B_Q + 3 * BR_W
C_Q = B_G + BR_W
C_K = C_Q + BR_W
C_V = C_K + WIN_KV_HEADS * HEAD_DIM
C_G = C_V + WIN_KV_HEADS * HEAD_DIM
D_Q = C_G + BR_W
D_G = D_Q + 3 * BR_W
E_Q = D_G + BR_W
E_G = E_Q + BR_W
M_G = E_G + BR_W

VMEM_LIMIT_BYTES = 56 * 1024 * 1024

PROJ_TM = 512
MERGE_TM = 256
MEM_TM = 512
MEM_SPLIT = 4
HG_C = 128
DF_T = 256
NA_RB = 8
WIN_NB = 4
N_PROJ_SEG = 14
DF_VROWS = 80
DF_RC = 64
DF_UNROLL = 4


def _params(sem):
    return pltpu.CompilerParams(dimension_semantics=sem, vmem_limit_bytes=VMEM_LIMIT_BYTES)


def _const_spec(shape):
    nd = len(shape)
    return pl.BlockSpec(shape, lambda *_: (0,) * nd, pipeline_mode=pl.Buffered(1))


def _dot(a, b):
    return jnp.dot(a, b, preferred_element_type=F32)


def _dot_nt(a, b):
    return lax.dot_general(a, b, (((1,), (1,)), ((), ())), preferred_element_type=F32)


def _rms_rows(x, g):
    ms = jnp.mean(x * x, axis=-1, keepdims=True)
    return x * lax.rsqrt(ms + EPS) * g


def _group_rms(x, gmat):
    x2 = x * x
    hi = x2.astype(BF16)
    lo = (x2 - hi.astype(F32)).astype(BF16)
    ms = _dot(hi, gmat) + _dot(lo, gmat)
    return x * lax.rsqrt(ms + EPS)


def _head_mask(width=BR_W):
    lane = lax.broadcasted_iota(jnp.int32, (1, width), 1)
    return [(lane >= h * HEAD_DIM) & (lane < (h + 1) * HEAD_DIM) for h in range(BR_HEADS)]


def _stack_heads(q, masks):
    zero = jnp.zeros_like(q)
    return jnp.concatenate([jnp.where(m, q, zero) for m in masks], axis=0)


def _unstack_heads(o_all, masks, m):
    out = jnp.zeros((m, BR_W), F32)
    for h, mk in enumerate(masks):
        out = out + jnp.where(mk, o_all[h * m:(h + 1) * m, :], 0.0)
    return out


class _Layout:
    def __init__(self, n_prompt, len_prompt, n_sample, len_sample):
        self.classes = ((0, n_prompt, len_prompt), (n_prompt * len_prompt, n_sample, len_sample))
        self.off1 = n_prompt * len_prompt
        self.lp = len_prompt
        self.ls = len_sample
        self.total = self.off1 + n_sample * len_sample
        self.nseq = n_prompt + n_sample
        self.n_prompt = n_prompt
        assert self.off1 % len_sample == 0

    def seq_bounds(self, tok):
        in_p = tok < self.off1
        start_p = (tok // self.lp) * self.lp
        start_s = self.off1 + ((tok - self.off1) // self.ls) * self.ls
        return jnp.where(in_p, start_p, start_s), jnp.where(in_p, self.lp, self.ls)

    def seq_index(self, tok):
        return jnp.where(tok < self.off1, tok // self.lp, self.n_prompt + (tok - self.off1) // self.ls)


def _proj_body(x_ref, ng_ref, w_ref, wvt_ref, ep_ref, g32_ref, g64_ref,
               aq_ref, av_ref, alf_ref, akk_ref, bq_ref, bk_ref, bv_ref,
               cq_ref, ck_ref, cv_ref, dq_ref, dk_ref, dv_ref, eq_ref):
    h = _rms_rows(x_ref[...], ng_ref[...]).astype(BF16)
    ep = ep_ref[...]

    def seg(i):
        return _dot(h, w_ref[:, i * BR_W:(i + 1) * BR_W])

    def row(r):
        return ep[r:r + 1, :]

    aq = seg(0)
    aq_ref[...] = (aq * jax.nn.sigmoid(aq)).astype(BF16)
    av_ref[...] = seg(1).astype(BF16)
    for d in range(2):
        z = seg(2 + d)
        lb = row(7 + d)
        e = jnp.exp(-jnp.abs(z))
        log_sig = jnp.minimum(z, 0.0) - jnp.log1p(e)
        t1 = jnp.log(jnp.maximum(lb, LB_FLOOR))
        t2 = jnp.log1p(-lb) + log_sig
        logf = jnp.maximum(t1, t2) + jnp.log1p(jnp.exp(-jnp.abs(t1 - t2)))
        sig_neg = jnp.where(z >= 0.0, e, 1.0) / (1.0 + e)
        alf_ref[:, d * BR_W:(d + 1) * BR_W] = logf
        akk_ref[:, d * BR_W:(d + 1) * BR_W] = ((1.0 - lb) * sig_neg).astype(BF16)
    g32 = g32_ref[...]
    g64 = g64_ref[...]
    bq_ref[...] = (_group_rms(seg(4), g32) * row(0)).astype(BF16)
    bk_ref[...] = (_group_rms(seg(5), g32) * row(1)).astype(BF16)
    vt = _dot_nt(wvt_ref[...], h).astype(BF16)
    pad_row = lax.broadcasted_iota(jnp.int32, (DF_VROWS - HEAD_DIM, DF_T), 0)
    pad = jnp.where(pad_row == 0, 1.0, 0.0).astype(BF16)
    for j in range(PROJ_TM // DF_T):
        for hd in range(BR_HEADS):
            bv_ref[j, hd, 0:HEAD_DIM, :] = vt[hd * HEAD_DIM:(hd + 1) * HEAD_DIM, j * DF_T:(j + 1) * DF_T]
            bv_ref[j, hd, HEAD_DIM:DF_VROWS, :] = pad
    cq_ref[...] = (_group_rms(seg(7), g64) * row(2)).astype(BF16)
    ck_ref[...] = (_group_rms(seg(8), g64) * row(3)).astype(BF16)
    cv_ref[...] = seg(9).astype(BF16)
    dq_ref[...] = (_group_rms(seg(10), g64) * row(4)).astype(BF16)
    dk_ref[...] = (_group_rms(seg(11), g64) * row(5)).astype(BF16)
    dv_ref[...] = seg(12).astype(BF16)
    eq_ref[...] = (_group_rms(seg(13), g64) * row(6)).astype(BF16)


def _project(x, ng, w1, wvt, ep, g32, g64):
    t = x.shape[0]
    tm = PROJ_TM
    tile = lambda w: pl.BlockSpec((tm, w), lambda i: (i, 0))
    widths = [BR_W, BR_W, 2 * BR_W, 2 * BR_W] + [BR_W] * 10
    dtypes = [BF16, BF16, F32, BF16] + [BF16] * 10
    out_specs = [tile(w) for w in widths]
    out_shape = [jax.ShapeDtypeStruct((t, w), dt) for w, dt in zip(widths, dtypes)]
    kt = tm // DF_T
    out_specs[6] = pl.BlockSpec((kt, BR_HEADS, DF_VROWS, DF_T), lambda i: (i, 0, 0, 0))
    out_shape[6] = jax.ShapeDtypeStruct((t // DF_T, BR_HEADS, DF_VROWS, DF_T), BF16)
    return pl.pallas_call(
        _proj_body,
        grid=(t // tm,),
        in_specs=[tile(D_MODEL), _const_spec((1, D_MODEL)), _const_spec(w1.shape), _const_spec(wvt.shape),
                  _const_spec(ep.shape), _const_spec(g32.shape), _const_spec(g64.shape)],
        out_specs=out_specs,
        out_shape=out_shape,
        compiler_params=_params(("parallel",)),
        name="proj",
    )(x, ng, w1, wvt, ep, g32, g64)


def _memkv_body(m_ref, g_ref, w_ref, gk_ref, g64_ref, mk_ref, mv_ref):
    mh = _rms_rows(m_ref[...], g_ref[...]).astype(BF16)
    kv = _dot(mh, w_ref[...])
    mk_ref[...] = (_group_rms(kv[:, :BR_W], g64_ref[...]) * gk_ref[...]).astype(BF16)
    mv_ref[...] = kv[:, BR_W:].astype(BF16)


def _mem_kv(mem, g, w, gk, g64):
    t = mem.shape[0]
    tm = 256
    return pl.pallas_call(
        _memkv_body,
        grid=(t // tm,),
        in_specs=[pl.BlockSpec((tm, D_MODEL), lambda i: (i, 0)), _const_spec((1, D_MODEL)), _const_spec(w.shape),
                  _const_spec((1, BR_W)), _const_spec(g64.shape)],
        out_specs=[pl.BlockSpec((tm, BR_W), lambda i: (i, 0))] * 2,
        out_shape=[jax.ShapeDtypeStruct((t, BR_W), BF16)] * 2,
        compiler_params=_params(("parallel",)),
        name="memkv",
    )(mem, g, w, gk, g64)


def _hgrn_constants(c, reverse):
    nl = int(math.log2(c))
    idx = np.arange(c)
    t = idx[:, None]
    u = idx[None, :]
    incl = (u <= t).astype(np.float32)
    rest = (u > t).astype(np.float32)
    tot = np.ones((8, c), np.float32)
    mds, mes, lms = [], [], [np.eye(c, dtype=np.float32)]
    for lev in range(nl):
        w = 1 << lev
        blk = idx // w
        odd = (blk % 2 == 1)
        md = (odd[:, None] & (u >= (blk * w)[:, None]) & (u <= t)).astype(np.float32)
        me = ((~odd)[:, None] & (u > t) & (u <= ((blk + 1) * w - 1)[:, None])).astype(np.float32)
        lm = (odd[:, None] & (blk[None, :] == (blk - 1)[:, None])).astype(np.float32)
        mds.append(md)
        mes.append(me)
        lms.append(lm)
    mats = [incl, rest, tot] + [md + me for md, me in zip(mds, mes)]
    if reverse:
        mats = [m[::-1, ::-1] for m in mats]
        lms = [m[::-1, ::-1] for m in lms]
    mall = np.concatenate(mats, axis=0)
    lmst = np.stack([np.tile(m, (BR_HEADS, 1)) for m in lms])
    bd = np.kron(np.eye(BR_HEADS, dtype=np.float32), np.ones((HEAD_DIM, HEAD_DIM), np.float32))
    return mall, lmst, bd


def _hgrn_chunk(layout, chunk, reverse, q_ref, k_ref, v_ref, lf_ref, mall_ref, lm_ref, bd_ref, o_ref, st_ref):
    c = HG_C
    nl = int(math.log2(c))
    tok = chunk * c
    start, length = layout.seq_bounds(tok)
    fresh = (tok + c == start + length) if reverse else (tok == start)

    masks = _head_mask()
    lf = lf_ref[...]
    hi = lf.astype(BF16)
    mid = (lf - hi.astype(F32)).astype(BF16)
    mall = mall_ref[...]
    cums = _dot(mall, hi) + _dot(mall, mid)
    b = cums[0:c, :]
    b_rest = cums[c:2 * c, :]
    b_tot = cums[2 * c:2 * c + 1, :]
    base = 2 * c + 8

    qf = q_ref[...].astype(F32)
    kf = k_ref[...].astype(F32)
    v = v_ref[...]
    st = jnp.where(fresh, 0.0, st_ref[...])

    o = _dot_nt((qf * jnp.exp(b)).astype(BF16), st.astype(BF16))
    a_all = _dot_nt(_stack_heads(q_ref[...], masks), k_ref[...]) * lm_ref[0]
    for lev in range(nl):
        dec = jnp.exp(cums[base + lev * c:base + (lev + 1) * c, :])
        ql = (qf * dec).astype(BF16)
        kl = (kf * dec).astype(BF16)
        a_all = a_all + _dot_nt(_stack_heads(ql, masks), kl) * lm_ref[lev + 1]
    a_bf = a_all.astype(BF16)
    zero = jnp.zeros_like(v)
    for h, mk in enumerate(masks):
        o = o + _dot(a_bf[h * c:(h + 1) * c, :], jnp.where(mk, v, zero))
    o_ref[...] = o

    kst = (kf * jnp.exp(b_rest)).astype(BF16)
    vt = v.astype(F32).T.astype(BF16)
    st_ref[...] = (st * jnp.exp(b_tot) + _dot(vt, kst)) * bd_ref[...]


def _hgrn_body(layout, nct, qf_ref, kf_ref, vf_ref, lff_ref, qb_ref, kb_ref, vb_ref, lfb_ref,
               mallf_ref, lmf_ref, mallb_ref, lmb_ref, bd_ref, of_ref, ob_ref, stf_ref, stb_ref):
    i = pl.program_id(0)
    _hgrn_chunk(layout, i, False, qf_ref, kf_ref, vf_ref, lff_ref, mallf_ref, lmf_ref, bd_ref, of_ref, stf_ref)
    _hgrn_chunk(layout, nct - 1 - i, True, qb_ref, kb_ref, vb_ref, lfb_ref, mallb_ref, lmb_ref, bd_ref,
                ob_ref, stb_ref)


def _hgrn_norm_body(of_ref, ob_ref, g64_ref, gn_ref, o_ref):
    o_ref[...] = (_group_rms(of_ref[...] + ob_ref[...], g64_ref[...]) * gn_ref[...]).astype(BF16)


def _hgrn(layout, qs, v, kk, logf, consts_f, consts_b, g64, gn):
    t = qs.shape[0]
    c = HG_C
    nct = t // c
    mall_f, lm_f, bd = consts_f
    mall_b, lm_b, _ = consts_b

    def specs(reverse):
        cm = (lambda i: nct - 1 - i) if reverse else (lambda i: i)
        d = 1 if reverse else 0
        tile = pl.BlockSpec((c, BR_W), lambda i: (cm(i), 0))
        half = pl.BlockSpec((c, BR_W), lambda i: (cm(i), d))
        return tile, half

    tile_f, half_f = specs(False)
    tile_b, half_b = specs(True)
    o_f, o_b = pl.pallas_call(
        functools.partial(_hgrn_body, layout, nct),
        grid=(nct,),
        in_specs=[tile_f, half_f, tile_f, half_f, tile_b, half_b, tile_b, half_b,
                  _const_spec(mall_f.shape), _const_spec(lm_f.shape), _const_spec(mall_b.shape),
                  _const_spec(lm_b.shape), _const_spec(bd.shape)],
        out_specs=[tile_f, tile_b],
        out_shape=[jax.ShapeDtypeStruct((t, BR_W), F32)] * 2,
        scratch_shapes=[pltpu.VMEM((BR_W, BR_W), F32)] * 2,
        compiler_params=_params(("arbitrary",)),
        name="hgrn_scan",
    )(qs, kk, v, logf, qs, kk, v, logf, mall_f, lm_f, mall_b, lm_b, bd)

    tm = PROJ_TM
    tok = pl.BlockSpec((tm, BR_W), lambda i: (i, 0))
    return pl.pallas_call(
        _hgrn_norm_body,
        grid=(t // tm,),
        in_specs=[tok, tok, _const_spec(g64.shape), _const_spec((1, BR_W))],
        out_specs=tok,
        out_shape=jax.ShapeDtypeStruct((t, BR_W), BF16),
        compiler_params=_params(("parallel",)),
        name="hgrn_norm",
    )(o_f, o_b, g64, gn)


def _diff_body(nkt, q_ref, k_ref, vt_ref, bias_ref, sc_ref, gcol_ref, o_ref, qm_ref, m_ref, acc_ref,
               s0_ref, s1_ref, mx0_ref, mx1_ref, p_ref):
    tq = DF_T
    qi = pl.program_id(1)
    qt = q_ref[...].astype(F32).T.astype(BF16)
    rowid = lax.broadcasted_iota(jnp.int32, (BR_W, tq), 0)
    for hc in range(2 * BR_HEADS):
        sel = (rowid >= hc * DF_DK) & (rowid < (hc + 1) * DF_DK)
        qm_ref[hc] = jnp.where(sel, qt, jnp.zeros_like(qt))
    m_ref[...] = jnp.full(m_ref.shape, NEG, F32)
    acc_ref[...] = jnp.zeros_like(acc_ref)

    nu = 2 * BR_HEADS
    rows = [(r, r + DF_RC) for r in range(0, tq, DF_RC)]

    def qk_unit(kj, hc, buf, near=None):
        sbuf, mxbuf = buf
        s = _dot(k_ref[kj], qm_ref[hc])
        if near is not None:
            s = s + bias_ref[near, hc // 2]
        sbuf[hc] = s
        mxbuf[hc] = jnp.max(s.reshape(tq // 8, 8, tq), axis=0)

    def softmax_pv_unit(kj, hc, buf, const_bias):
        sbuf, mxbuf = buf
        c = const_bias(hc // 2)
        m_old = m_ref[hc]
        m_new = jnp.maximum(m_old, jnp.max(mxbuf[hc], axis=0, keepdims=True) + c)
        shift = m_new - c
        p = jnp.concatenate([jnp.exp2(sbuf[hc, r0:r1, :] - shift).astype(BF16) for r0, r1 in rows], axis=0)
        m_ref[hc] = m_new
        acc_ref[hc] = acc_ref[hc] * jnp.exp2(m_old - m_new) + _dot(vt_ref[kj, hc // 2], p)

    def step(kj, buf, const_bias, nxt):
        for hc in range(nu):
            qk_unit(nxt[0], hc, nxt[1], nxt[2])
            softmax_pv_unit(kj, hc, buf, const_bias)

    last = nkt - 1
    n_low = jnp.maximum(qi - 1, 0)
    n_far = n_low + jnp.maximum(nkt - qi - 2, 0)
    buf_a = (s0_ref, mx0_ref)
    buf_b = (s1_ref, mx1_ref)

    def far_tile(f):
        return jnp.minimum(jnp.where(f < n_low, f, f - n_low + qi + 2), last)

    def dead_if(cond):
        pen = jnp.where(cond, 2.0 * NEG, 0.0)
        return lambda h: pen

    near = [jnp.clip(qi + d - 1, 0, last) for d in range(3)]
    for hc in range(nu):
        qk_unit(near[0], hc, buf_a, 0)
    step(near[0], buf_a, dead_if(qi == 0), (near[1], buf_b, 1))
    step(near[1], buf_b, dead_if(False), (near[2], buf_a, 2))
    step(near[2], buf_a, dead_if(qi == last), (far_tile(0), buf_b, None))

    def far_step(f, cur, nxt):
        kj = far_tile(f)
        row = jnp.where(kj < qi, 0, 1)
        live = f < n_far
        step(kj, cur, lambda h: jnp.where(live, sc_ref[row, h], 2.0 * NEG), (far_tile(f + 1), nxt, None))

    far_step(0, buf_b, buf_a)

    def body(i, carry):
        f0 = 1 + DF_UNROLL * i
        for u in range(DF_UNROLL):
            cur, nxt = (buf_a, buf_b) if u % 2 == 0 else (buf_b, buf_a)
            far_step(f0 + u, cur, nxt)
        return carry
    lax.fori_loop(0, (n_far - 1 + DF_UNROLL - 1) // DF_UNROLL, body, 0)

    lmb = sc_ref[2, 0]
    outs = []
    for h in range(BR_HEADS):
        a0 = acc_ref[2 * h]
        a1 = acc_ref[2 * h + 1]
        o0 = a0[:HEAD_DIM, :] / a0[HEAD_DIM:HEAD_DIM + 1, :]
        o1 = a1[:HEAD_DIM, :] / a1[HEAD_DIM:HEAD_DIM + 1, :]
        o = o0 - lmb * o1
        ms = jnp.mean(o * o, axis=0, keepdims=True)
        outs.append(o * lax.rsqrt(ms + EPS))
    ot = jnp.concatenate(outs, axis=0) * gcol_ref[...]
    o_ref[...] = ot.T.astype(BF16)


def _diff_attn_class(off, nseq, n, qt, k3, vt4, bias, sc, gcol):
    t = DF_T
    nkt = n // t
    qb = off // t
    sb = off // n
    return pl.pallas_call(
        functools.partial(_diff_body, nkt),
        grid=(nseq, nkt),
        in_specs=[pl.BlockSpec((t, BR_W), lambda s, i: (qb + s * nkt + i, 0)),
                  pl.BlockSpec((nkt, t, BR_W), lambda s, i: (sb + s, 0, 0), pipeline_mode=pl.Buffered(1)),
                  pl.BlockSpec((nkt, BR_HEADS, DF_VROWS, t), lambda s, i: (sb + s, 0, 0, 0),
                               pipeline_mode=pl.Buffered(1)),
                  _const_spec(bias.shape),
                  pl.BlockSpec(memory_space=pltpu.SMEM),
                  _const_spec(gcol.shape)],
        out_specs=pl.BlockSpec((t, BR_W), lambda s, i: (s * nkt + i, 0)),
        out_shape=jax.ShapeDtypeStruct((nseq * n, BR_W), BF16),
        scratch_shapes=[pltpu.VMEM((2 * BR_HEADS, BR_W, t), BF16),
                        pltpu.VMEM((2 * BR_HEADS, 1, t), F32),
                        pltpu.VMEM((2 * BR_HEADS, DF_VROWS, t), F32),
                        pltpu.VMEM((2 * BR_HEADS, t, t), F32),
                        pltpu.VMEM((2 * BR_HEADS, t, t), F32),
                        pltpu.VMEM((2 * BR_HEADS, 8, t), F32),
                        pltpu.VMEM((2 * BR_HEADS, 8, t), F32),
                        pltpu.VMEM((2 * BR_HEADS, t, t), BF16)],
        compiler_params=_params(("parallel", "parallel")),
        name="diff_attn",
    )(qt, k3, vt4, bias, sc, gcol)


def _diff_attn(layout, q, k, vt4, bias, sc, gcol):
    t = k.shape[0]
    k3 = k.reshape(t // DF_T, DF_T, BR_W)
    outs = [_diff_attn_class(off, nseq, n, q, k3, vt4, bias, sc, gcol) for off, nseq, n in layout.classes]
    return jnp.concatenate(outs, axis=0)


def _win_body(layout, q_ref, kp_ref, kc_ref, kn_ref, vp_ref, vc_ref, vn_ref, bias_ref, sink_ref, o_ref):
    blk = WIN_BLOCK
    i = pl.program_id(0)
    tok = i * (WIN_NB * blk)
    start, length = layout.seq_bounds(tok)
    masks = _head_mask()
    kcat = jnp.concatenate([kp_ref[...], kc_ref[...], kn_ref[...]], axis=0)
    vcat = jnp.concatenate([vp_ref[...], vc_ref[...], vn_ref[...]], axis=0)
    col = lax.broadcasted_iota(jnp.int32, (1, 3 * blk), 1)
    sink = sink_ref[...]
    bias = bias_ref[...]
    subs = range(WIN_NB)

    def dead_cols(j):
        has_prev = tok + j * blk > start
        has_next = tok + (j + 1) * blk < start + length
        return ((col < blk) & jnp.logical_not(has_prev)) | ((col >= 2 * blk) & jnp.logical_not(has_next))

    s = [_dot_nt(_stack_heads(q_ref[j * blk:(j + 1) * blk, :], masks), kcat[j * blk:(j + 3) * blk, :]) for j in subs]
    s = [s[j] + bias + jnp.where(dead_cols(j), NEG, 0.0) for j in subs]
    m = [jnp.maximum(jnp.max(s[j], axis=-1, keepdims=True), sink) for j in subs]
    p = [jnp.exp2(s[j] - m[j]) for j in subs]
    den = [jnp.sum(p[j], axis=-1, keepdims=True) + jnp.exp2(sink - m[j]) for j in subs]
    pn = [(p[j] * (1.0 / den[j])).astype(BF16) for j in subs]
    o_all = [_dot(pn[j], vcat[j * blk:(j + 3) * blk, :]) for j in subs]
    for j in subs:
        o_ref[j * blk:(j + 1) * blk, :] = _unstack_heads(o_all[j], masks, blk).astype(BF16)


def _win_attn(layout, q, k, v, bias, sink):
    t = q.shape[0]
    blk = WIN_BLOCK
    nb = t // blk
    cur = pl.BlockSpec((WIN_NB * blk, BR_W), lambda i: (i, 0))
    prev = pl.BlockSpec((blk, BR_W), lambda i: (jnp.maximum(i * WIN_NB - 1, 0), 0))
    nxt = pl.BlockSpec((blk, BR_W), lambda i: (jnp.minimum((i + 1) * WIN_NB, nb - 1), 0))
    return pl.pallas_call(
        functools.partial(_win_body, layout),
        grid=(nb // WIN_NB,),
        in_specs=[cur, prev, cur, nxt, prev, cur, nxt, _const_spec(bias.shape), _const_spec(sink.shape)],
        out_specs=cur,
        out_shape=jax.ShapeDtypeStruct((t, BR_W), BF16),
        compiler_params=_params(("parallel",)),
        name="win_attn",
    )(q, k, k, k, v, v, v, bias, sink)


def _na_body(rows, q_ref, k_ref, v_ref, bias_ref, o_ref):
    j = pl.program_id(1)
    masks = _head_mask()
    nk = NA_KH * GRID_W
    rr = range(NA_RB)
    r = [j * NA_RB + i for i in rr]
    rs = [jnp.clip(r[i] - NA_KH // 2, 0, rows - NA_KH) for i in rr]
    koff = [pl.multiple_of(rs[i] * GRID_W, GRID_W) for i in rr]
    s = [_dot_nt(_stack_heads(q_ref[i * GRID_W:(i + 1) * GRID_W, :], masks), k_ref[pl.ds(koff[i], nk), :])
         + bias_ref[r[i] - rs[i]] for i in rr]
    m = [jnp.max(s[i], axis=-1, keepdims=True) for i in rr]
    p = [jnp.exp2(s[i] - m[i]) for i in rr]
    pn = [(p[i] * (1.0 / jnp.sum(p[i], axis=-1, keepdims=True))).astype(BF16) for i in rr]
    o_all = [_dot(pn[i], v_ref[pl.ds(koff[i], nk), :]) for i in rr]
    for i in rr:
        o_ref[i * GRID_W:(i + 1) * GRID_W, :] = _unstack_heads(o_all[i], masks, GRID_W).astype(BF16)


def _na_class(off, nseq, n, q, k, v, bias):
    rows = n // GRID_W
    qt = NA_RB * GRID_W
    nj = n // qt
    qb = off // qt
    sb = off // n
    seq = pl.BlockSpec((n, BR_W), lambda s, j: (sb + s, 0), pipeline_mode=pl.Buffered(1))
    return pl.pallas_call(
        functools.partial(_na_body, rows),
        grid=(nseq, nj),
        in_specs=[pl.BlockSpec((qt, BR_W), lambda s, j: (qb + s * nj + j, 0)), seq, seq, _const_spec(bias.shape)],
        out_specs=pl.BlockSpec((qt, BR_W), lambda s, j: (s * nj + j, 0)),
        out_shape=jax.ShapeDtypeStruct((nseq * n, BR_W), BF16),
        compiler_params=_params(("parallel", "parallel")),
        name="na_attn",
    )(q, k, v, bias)


def _na_attn(layout, q, k, v, bias):
    return jnp.concatenate([_na_class(off, nseq, n, q, k, v, bias) for off, nseq, n in layout.classes], axis=0)


def _mem_body(q_ref, mk_ref, mv_ref, o_ref):
    masks = _head_mask()
    sub = MEM_TM // MEM_SPLIT
    parts = range(MEM_SPLIT)
    mk = mk_ref[0]
    mv = mv_ref[0]
    s = [_dot_nt(_stack_heads(q_ref[i * sub:(i + 1) * sub, :], masks), mk) for i in parts]
    m = [jnp.max(s[i], axis=-1, keepdims=True) for i in parts]
    p = [jnp.exp2(s[i] - m[i]) for i in parts]
    pn = [(p[i] * (1.0 / jnp.sum(p[i], axis=-1, keepdims=True))).astype(BF16) for i in parts]
    o_all = [_dot(pn[i], mv) for i in parts]
    for i in parts:
        o_ref[i * sub:(i + 1) * sub, :] = _unstack_heads(o_all[i], masks, sub).astype(BF16)


def _mem_attn(layout, q, mk, mv):
    t = q.shape[0]
    tm = MEM_TM
    mem_len = mk.shape[1]
    tile = pl.BlockSpec((tm, BR_W), lambda i: (i, 0))
    mem = pl.BlockSpec((1, mem_len, BR_W), lambda i: (layout.seq_index(i * tm), 0, 0))
    return pl.pallas_call(
        _mem_body,
        grid=(t // tm,),
        in_specs=[tile, mem, mem],
        out_specs=tile,
        out_shape=jax.ShapeDtypeStruct((t, BR_W), BF16),
        compiler_params=_params(("parallel",)),
        name="mem_attn",
    )(q, mk, mv)


def _merge_body(x_ref, ng_ref, oa_ref, ob_ref, oc_ref, od_ref, oe_ref, wg_ref, wm_ref, wb_ref, wo_ref, y_ref):
    x = x_ref[...]
    h = _rms_rows(x, ng_ref[...]).astype(BF16)
    merged = jnp.zeros((x.shape[0], D_MODEL), F32)
    for kb, o_ref in enumerate((oa_ref, ob_ref, oc_ref, od_ref, oe_ref)):
        g = _dot(h, wg_ref[:, kb * BR_W:(kb + 1) * BR_W])
        br = (o_ref[...].astype(F32) * (g * jax.nn.sigmoid(g))).astype(BF16)
        mg = jax.nn.sigmoid(_dot(h, wm_ref[:, kb * D_MODEL:(kb + 1) * D_MODEL]))
        merged = merged + mg * _dot(br, wb_ref[kb])
    y_ref[...] = x + _dot(merged.astype(BF16), wo_ref[...])


def _merge(x, ng, branches, wg, wm, wb, wo):
    t = x.shape[0]
    tm = MERGE_TM
    xt = pl.BlockSpec((tm, D_MODEL), lambda i: (i, 0))
    bt = pl.BlockSpec((tm, BR_W), lambda i: (i, 0))
    return pl.pallas_call(
        _merge_body,
        grid=(t // tm,),
        in_specs=[xt, _const_spec((1, D_MODEL))] + [bt] * N_BRANCH
        + [_const_spec(wg.shape), _const_spec(wm.shape), _const_spec(wb.shape), _const_spec(wo.shape)],
        out_specs=xt,
        out_shape=jax.ShapeDtypeStruct((t, D_MODEL), F32),
        input_output_aliases={0: 0},
        compiler_params=_params(("parallel",)),
        name="merge",
    )(x, ng, *branches, wg, wm, wb, wo)


def _t5_bucket(rel):
    half = N_BUCKETS // 2
    exact = half // 2
    n = jnp.abs(rel)
    nf = jnp.maximum(n, 1).astype(F32)
    large = exact + (jnp.log(nf / exact) / math.log(MAX_DIST / exact) * (half - exact)).astype(jnp.int32)
    large = jnp.clip(large, 0, half - 1)
    return jnp.where(rel > 0, half, 0) + jnp.where(n < exact, n, large)


def _group_matrix(group):
    return jnp.asarray(np.kron(np.eye(BR_W // group), np.full((group, group), 1.0 / group)), BF16)


def _lookup(table, idx):
    onehot = (idx[..., None] == jnp.arange(table.shape[0])).astype(F32)
    return jnp.dot(onehot, table, precision=lax.Precision.HIGHEST)


def _diff_bias_tables(rel_bias):
    t = DF_T
    table = rel_bias[:, :BR_HEADS].astype(F32) * LOG2E
    kl = jnp.arange(t)[:, None]
    ql = jnp.arange(t)[None, :]
    rel = jnp.stack([kl - ql + d * t for d in (-1, 0, 1)])
    tiles = _lookup(table, _t5_bucket(rel)).transpose(0, 3, 1, 2)
    far = _lookup(table, _t5_bucket(jnp.asarray([-2 * t, 2 * t], jnp.int32)))
    return tiles, far


def _win_bias_table(rel_bias):
    rel = jnp.arange(3 * WIN_BLOCK)[None, :] - WIN_BLOCK - jnp.arange(WIN_BLOCK)[:, None]
    bias = _lookup(rel_bias[:, BR_HEADS:].astype(F32) * LOG2E, _t5_bucket(rel)).transpose(2, 0, 1)
    bias = jnp.where((jnp.abs(rel) <= WIN)[None], bias, NEG)
    return bias.reshape(BR_HEADS * WIN_BLOCK, 3 * WIN_BLOCK)


def _na_bias_tables(rpb):
    depth = rpb.shape[0]
    col = np.arange(GRID_W)
    cs = np.clip(col - NA_KW // 2, 0, GRID_W - NA_KW)
    inwin = (col[None, :] >= cs[:, None]) & (col[None, :] < cs[:, None] + NA_KW)
    dc = col[None, :] - col[:, None] + (NA_KW - 1)
    onehot = jnp.asarray(dc[None] == np.arange(2 * NA_KW - 1)[:, None, None], F32)
    toep = jnp.einsum("lhrc,cqk->lhrqk", rpb.astype(F32) * LOG2E, onehot, precision=lax.Precision.HIGHEST)
    toep = jnp.where(jnp.asarray(inwin)[None, None, None], toep, NEG)
    out = []
    for d in range(NA_KH):
        lo = NA_KH - 1 - d
        b = toep[:, :, lo:lo + NA_KH].transpose(0, 1, 3, 2, 4)
        out.append(b.reshape(depth, BR_HEADS * GRID_W, NA_KH * GRID_W))
    return jnp.stack(out, axis=1)


def _tile_gain(g, reps, scale=1.0):
    return jnp.tile(g.astype(F32), reps)[None, :] * scale


def kernel(x_prompt, x_sample, mem_prompt, mem_sample, norm_g, mem_norm_g, w_in, w_mem_kv, rel_bias, hgrn_lb,
           hgrn_norm_g, diff_qk_g, diff_lambda, diff_subln_g, win_qk_g, win_sink, na_qk_g, na_rpb, mem_qk_g,
           w_branch, w_out):
    depth = w_in.shape[0]
    bp, lp, _ = x_prompt.shape
    bs, ls, _ = x_sample.shape
    mem_len = mem_prompt.shape[1]
    layout = _Layout(bp, lp, bs, ls)
    t = layout.total
    assert t % PROJ_TM == 0 and lp % DF_T == 0 and ls % DF_T == 0
    assert lp % (NA_RB * GRID_W) == 0 and ls % (NA_RB * GRID_W) == 0 and lp % MEM_TM == 0 and ls % MEM_TM == 0
    assert lp % (WIN_NB * WIN_BLOCK) == 0 and ls % (WIN_NB * WIN_BLOCK) == 0

    x = jnp.concatenate([x_prompt.reshape(bp * lp, D_MODEL), x_sample.reshape(bs * ls, D_MODEL)], axis=0)
    mem = jnp.concatenate([mem_prompt.reshape(bp * mem_len, D_MODEL), mem_sample.reshape(bs * mem_len, D_MODEL)], axis=0)

    sm = jax.nn.softmax(hgrn_lb.astype(F32), axis=1)
    lb_all = jnp.clip(jnp.cumsum(sm, axis=1) - sm[:, :1], 0.0, 1.0 - 1e-6)
    lam_init = jnp.asarray([0.8 - 0.6 * math.exp(-0.3 * l) for l in range(depth)], F32)
    lam = diff_lambda.astype(F32)
    lmb = jnp.exp(jnp.sum(lam[:, 0] * lam[:, 1], axis=-1)) - jnp.exp(jnp.sum(lam[:, 2] * lam[:, 3], axis=-1)) + lam_init

    w_in_b = w_in.astype(BF16)

    def expand_kv(w):
        w = w.reshape(depth, D_MODEL, WIN_KV_HEADS, HEAD_DIM)
        return jnp.repeat(w, BR_HEADS // WIN_KV_HEADS, axis=2).reshape(depth, D_MODEL, BR_W)

    w1 = jnp.concatenate([w_in_b[:, :, A_Q:A_G], w_in_b[:, :, B_Q:B_G], w_in_b[:, :, C_Q:C_K],
                          expand_kv(w_in_b[:, :, C_K:C_V]), expand_kv(w_in_b[:, :, C_V:C_G]),
                          w_in_b[:, :, D_Q:D_G], w_in_b[:, :, E_Q:E_G]], axis=-1)
    wvt = w_in_b[:, :, B_Q + 2 * BR_W:B_G].transpose(0, 2, 1)
    wg = jnp.concatenate([w_in_b[:, :, A_G:B_Q], w_in_b[:, :, B_G:C_Q], w_in_b[:, :, C_G:D_Q],
                          w_in_b[:, :, D_G:E_Q], w_in_b[:, :, E_G:M_G]], axis=-1)
    wm = w_in_b[:, :, M_G:]
    wb = w_branch.astype(BF16)
    wo = w_out.astype(BF16)
    wmem = w_mem_kv.astype(BF16)

    sc_b = DF_DK ** -0.5 * LOG2E
    sc_h = HEAD_DIM ** -0.5 * LOG2E
    zrow = jnp.zeros((depth, 1, BR_W), F32)

    def per_layer(fn):
        return jnp.stack([fn(l) for l in range(depth)])

    ep = jnp.concatenate([
        per_layer(lambda l: _tile_gain(diff_qk_g[l, 0], 8, sc_b)),
        per_layer(lambda l: _tile_gain(diff_qk_g[l, 1], 8)),
        per_layer(lambda l: _tile_gain(win_qk_g[l, 0], 4, sc_h)),
        per_layer(lambda l: _tile_gain(win_qk_g[l, 1], 4)),
        per_layer(lambda l: _tile_gain(na_qk_g[l, 0], 4, sc_h)),
        per_layer(lambda l: _tile_gain(na_qk_g[l, 1], 4)),
        per_layer(lambda l: _tile_gain(mem_qk_g[l, 0], 4, sc_h)),
        lb_all[0][:, None, :], lb_all[1][:, None, :]] + [zrow] * 7, axis=1)
    gk_mem = per_layer(lambda l: _tile_gain(mem_qk_g[l, 1], 4))
    gn_hg = hgrn_norm_g.astype(F32)[:, None, :]
    gcol = per_layer(lambda l: (jnp.tile(diff_subln_g[l].astype(F32), BR_HEADS) * (1.0 - lam_init[l]))[:, None])
    sink = per_layer(lambda l: jnp.repeat(win_sink[l].astype(F32) * LOG2E, WIN_BLOCK)[:, None])
    na_bias = _na_bias_tables(na_rpb)

    diff_bias, diff_far = _diff_bias_tables(rel_bias)
    win_bias = _win_bias_table(rel_bias)
    sc = jnp.concatenate([jnp.broadcast_to(diff_far[None], (depth, 2, BR_HEADS)),
                          jnp.broadcast_to(lmb[:, None, None], (depth, 1, BR_HEADS))], axis=1)

    g32 = _group_matrix(DF_DK)
    g64 = _group_matrix(HEAD_DIM)
    consts_f = tuple(jnp.asarray(a, dt) for a, dt in zip(_hgrn_constants(HG_C, False), (BF16, F32, F32)))
    consts_b = tuple(jnp.asarray(a, dt) for a, dt in zip(_hgrn_constants(HG_C, True), (BF16, F32, F32)))

    def layer(x, p):
        ng = p["ng"]
        (a_q, a_v, a_lf, a_kk, b_q, b_k, b_v, c_q, c_k, c_v, d_q, d_k, d_v, e_q) = _project(
            x, ng, p["w1"], p["wvt"], p["ep"], g32, g64)
        mk, mv = _mem_kv(mem, p["mng"], p["wmem"], p["gk_mem"], g64)
        o_a = _hgrn(layout, a_q, a_v, a_kk, a_lf, consts_f, consts_b, g64, p["gn_hg"])
        o_b = _diff_attn(layout, b_q, b_k, b_v, diff_bias, p["sc"], p["gcol"])
        o_c = _win_attn(layout, c_q, c_k, c_v, win_bias, p["sink"])
        o_d = _na_attn(layout, d_q, d_k, d_v, p["na_bias"])
        o_e = _mem_attn(layout, e_q, mk.reshape(layout.nseq, mem_len, BR_W), mv.reshape(layout.nseq, mem_len, BR_W))
        y = _merge(x, ng, (o_a, o_b, o_c, o_d, o_e), p["wg"], p["wm"], p["wb"], p["wo"])
        return y, None

    params = dict(ng=norm_g.astype(F32)[:, None, :], mng=mem_norm_g.astype(F32)[:, None, :], w1=w1, wvt=wvt, ep=ep,
                  wmem=wmem, gk_mem=gk_mem, gn_hg=gn_hg, sc=sc, gcol=gcol, sink=sink, na_bias=na_bias,
                  wg=wg, wm=wm, wb=wb, wo=wo)
    x, _ = lax.scan(layer, x, params)
    y_prompt = x[:layout.off1].reshape(bp, lp, D_MODEL)
    y_sample = x[layout.off1:].reshape(bs, ls, D_MODEL)
    return (y_prompt, y_sample)
```

```python
import functools
import math

import numpy as np
import jax
import jax.numpy as jnp
from jax import lax
from jax.experimental import pallas as pl
from jax.experimental.pallas import tpu as pltpu

F32 = jnp.float32
BF16 = jnp.bfloat16

D_MODEL = 1024
HEAD_DIM = 64
BR_HEADS = 4
BR_W = BR_HEADS * HEAD_DIM
N_BRANCH = 5
DF_DK = HEAD_DIM // 2
WIN = 128
WIN_BLOCK = 128
WIN_KV_HEADS = 2
GRID_W = 64
NA_KH = 8
NA_KW = 16
N_BUCKETS = 32
MAX_DIST = 128
EPS = 1e-6
NEG = -1e30
LB_FLOOR = 1e-30
LOG2E = 1.4426950408889634

A_Q = 0
A_G = 4 * BR_W
B_Q = A_G + BR_W
B_G = B_Q + 3 * BR_W
C_Q = B_G + BR_W
C_K = C_Q + BR_W
C_V = C_K + WIN_KV_HEADS * HEAD_DIM
C_G = C_V + WIN_KV_HEADS * HEAD_DIM
D_Q = C_G + BR_W
D_G = D_Q + 3 * BR_W
E_Q = D_G + BR_W
E_G = E_Q + BR_W
M_G = E_G + BR_W

VMEM_LIMIT_BYTES = 56 * 1024 * 1024

PROJ_TM = 512
MERGE_TM = 256
MEM_TM = 512
MEM_SPLIT = 4
HG_C = 128
HG_MAT_LEVELS = 2
DF_T = 256
NA_RB = 8
WIN_NB = 4
N_PROJ_SEG = 14
DF_VROWS = 80
DF_RC = 64
DF_KH = 128
DF_UNROLL = 4


def _params(sem):
    return pltpu.CompilerParams(dimension_semantics=sem, vmem_limit_bytes=VMEM_LIMIT_BYTES)


def _const_spec(shape):
    nd = len(shape)
    return pl.BlockSpec(shape, lambda *_: (0,) * nd, pipeline_mode=pl.Buffered(1))


def _dot(a, b):
    return jnp.dot(a, b, preferred_element_type=F32)


def _dot_nt(a, b):
    return lax.dot_general(a, b, (((1,), (1,)), ((), ())), preferred_element_type=F32)


def _rms_rows(x, g):
    ms = jnp.mean(x * x, axis=-1, keepdims=True)
    return x * lax.rsqrt(ms + EPS) * g


def _group_rms(x, gmat):
    x2 = x * x
    hi = x2.astype(BF16)
    lo = (x2 - hi.astype(F32)).astype(BF16)
    ms = _dot(hi, gmat) + _dot(lo, gmat)
    return x * lax.rsqrt(ms + EPS)


def _head_mask(width=BR_W):
    lane = lax.broadcasted_iota(jnp.int32, (1, width), 1)
    return [(lane >= h * HEAD_DIM) & (lane < (h + 1) * HEAD_DIM) for h in range(BR_HEADS)]


def _stack_heads(q, masks):
    zero = jnp.zeros_like(q)
    return jnp.concatenate([jnp.where(m, q, zero) for m in masks], axis=0)


def _unstack_heads(o_all, masks, m):
    out = jnp.zeros((m, BR_W), F32)
    for h, mk in enumerate(masks):
        out = out + jnp.where(mk, o_all[h * m:(h + 1) * m, :], 0.0)
    return out


class _Layout:
    def __init__(self, n_prompt, len_prompt, n_sample, len_sample):
        self.classes = ((0, n_prompt, len_prompt), (n_prompt * len_prompt, n_sample, len_sample))
        self.off1 = n_prompt * len_prompt
        self.lp = len_prompt
        self.ls = len_sample
        self.total = self.off1 + n_sample * len_sample
        self.nseq = n_prompt + n_sample
        self.n_prompt = n_prompt
        assert self.off1 % len_sample == 0

    def seq_bounds(self, tok):
        in_p = tok < self.off1
        start_p = (tok // self.lp) * self.lp
        start_s = self.off1 + ((tok - self.off1) // self.ls) * self.ls
        return jnp.where(in_p, start_p, start_s), jnp.where(in_p, self.lp, self.ls)

    def seq_index(self, tok):
        return jnp.where(tok < self.off1, tok // self.lp, self.n_prompt + (tok - self.off1) // self.ls)


def _proj_body(x_ref, ng_ref, w_ref, wvt_ref, ep_ref, g32_ref, g64_ref,
               aq_ref, av_ref, alf_ref, akk_ref, bq_ref, bk_ref, bv_ref,
               cq_ref, ck_ref, cv_ref, dq_ref, dk_ref, dv_ref, eq_ref):
    h = _rms_rows(x_ref[...], ng_ref[...]).astype(BF16)
    ep = ep_ref[...]

    def seg(i):
        return _dot(h, w_ref[:, i * BR_W:(i + 1) * BR_W])

    def row(r):
        return ep[r:r + 1, :]

    aq = seg(0)
    aq_ref[...] = (aq * jax.nn.sigmoid(aq)).astype(BF16)
    av_ref[...] = seg(1).astype(BF16)
    for d in range(2):
        z = seg(2 + d)
        lb = row(7 + d)
        e = jnp.exp(-jnp.abs(z))
        log_sig = jnp.minimum(z, 0.0) - jnp.log1p(e)
        t1 = jnp.log(jnp.maximum(lb, LB_FLOOR))
        t2 = jnp.log1p(-lb) + log_sig
        logf = jnp.maximum(t1, t2) + jnp.log1p(jnp.exp(-jnp.abs(t1 - t2)))
        sig_neg = jnp.where(z >= 0.0, e, 1.0) / (1.0 + e)
        alf_ref[:, d * BR_W:(d + 1) * BR_W] = logf
        akk_ref[:, d * BR_W:(d + 1) * BR_W] = ((1.0 - lb) * sig_neg).astype(BF16)
    g32 = g32_ref[...]
    g64 = g64_ref[...]
    bq_ref[...] = (_group_rms(seg(4), g32) * row(0)).astype(BF16)
    bk_ref[...] = (_group_rms(seg(5), g32) * row(1)).astype(BF16)
    vt = _dot_nt(wvt_ref[...], h).astype(BF16)
    pad_row = lax.broadcasted_iota(jnp.int32, (DF_VROWS - HEAD_DIM, DF_T), 0)
    pad = jnp.where(pad_row == 0, 1.0, 0.0).astype(BF16)
    for j in range(PROJ_TM // DF_T):
        for hd in range(BR_HEADS):
            bv_ref[j, hd, 0:HEAD_DIM, :] = vt[hd * HEAD_DIM:(hd + 1) * HEAD_DIM, j * DF_T:(j + 1) * DF_T]
            bv_ref[j, hd, HEAD_DIM:DF_VROWS, :] = pad
    cq_ref[...] = (_group_rms(seg(7), g64) * row(2)).astype(BF16)
    ck_ref[...] = (_group_rms(seg(8), g64) * row(3)).astype(BF16)
    cv_ref[...] = seg(9).astype(BF16)
    dq_ref[...] = (_group_rms(seg(10), g64) * row(4)).astype(BF16)
    dk_ref[...] = (_group_rms(seg(11), g64) * row(5)).astype(BF16)
    dv_ref[...] = seg(12).astype(BF16)
    eq_ref[...] = (_group_rms(seg(13), g64) * row(6)).astype(BF16)


def _project(x, ng, w1, wvt, ep, g32, g64):
    t = x.shape[0]
    tm = PROJ_TM
    tile = lambda w: pl.BlockSpec((tm, w), lambda i: (i, 0))
    widths = [BR_W, BR_W, 2 * BR_W, 2 * BR_W] + [BR_W] * 10
    dtypes = [BF16, BF16, F32, BF16] + [BF16] * 10
    out_specs = [tile(w) for w in widths]
    out_shape = [jax.ShapeDtypeStruct((t, w), dt) for w, dt in zip(widths, dtypes)]
    kt = tm // DF_T
    out_specs[6] = pl.BlockSpec((kt, BR_HEADS, DF_VROWS, DF_T), lambda i: (i, 0, 0, 0))
    out_shape[6] = jax.ShapeDtypeStruct((t // DF_T, BR_HEADS, DF_VROWS, DF_T), BF16)
    return pl.pallas_call(
        _proj_body,
        grid=(t // tm,),
        in_specs=[tile(D_MODEL), _const_spec((1, D_MODEL)), _const_spec(w1.shape), _const_spec(wvt.shape),
                  _const_spec(ep.shape), _const_spec(g32.shape), _const_spec(g64.shape)],
        out_specs=out_specs,
        out_shape=out_shape,
        compiler_params=_params(("parallel",)),
        name="proj",
    )(x, ng, w1, wvt, ep, g32, g64)


def _memkv_body(m_ref, g_ref, w_ref, gk_ref, g64_ref, mk_ref, mv_ref):
    mh = _rms_rows(m_ref[...], g_ref[...]).astype(BF16)
    kv = _dot(mh, w_ref[...])
    mk_ref[...] = (_group_rms(kv[:, :BR_W], g64_ref[...]) * gk_ref[...]).astype(BF16)
    mv_ref[...] = kv[:, BR_W:].astype(BF16)


def _mem_kv(mem, g, w, gk, g64):
    t = mem.shape[0]
    tm = 256
    return pl.pallas_call(
        _memkv_body,
        grid=(t // tm,),
        in_specs=[pl.BlockSpec((tm, D_MODEL), lambda i: (i, 0)), _const_spec((1, D_MODEL)), _const_spec(w.shape),
                  _const_spec((1, BR_W)), _const_spec(g64.shape)],
        out_specs=[pl.BlockSpec((tm, BR_W), lambda i: (i, 0))] * 2,
        out_shape=[jax.ShapeDtypeStruct((t, BR_W), BF16)] * 2,
        compiler_params=_params(("parallel",)),
        name="memkv",
    )(mem, g, w, gk, g64)


def _hgrn_constants(c, reverse):
    nl = int(math.log2(c))
    idx = np.arange(c)
    t = idx[:, None]
    u = idx[None, :]
    incl = (u <= t).astype(np.float32)
    rest = (u > t).astype(np.float32)
    tot = np.ones((8, c), np.float32)
    mds, mes, lms = [], [], [np.eye(c, dtype=np.float32)]
    for lev in range(nl):
        w = 1 << lev
        blk = idx // w
        odd = (blk % 2 == 1)
        md = (odd[:, None] & (u >= (blk * w)[:, None]) & (u <= t)).astype(np.float32)
        me = ((~odd)[:, None] & (u > t) & (u <= ((blk + 1) * w - 1)[:, None])).astype(np.float32)
        lm = (odd[:, None] & (blk[None, :] == (blk - 1)[:, None])).astype(np.float32)
        mds.append(md)
        mes.append(me)
        lms.append(lm)
    del rest
    mats = [incl, tot] + [md + me for md, me in zip(mds[:HG_MAT_LEVELS], mes[:HG_MAT_LEVELS])]
    if reverse:
        mats = [m[::-1, ::-1] for m in mats]
        lms = [m[::-1, ::-1] for m in lms]
    mall = np.concatenate(mats, axis=0)
    lmst = np.stack([np.tile(m, (BR_HEADS, 1)) for m in lms])
    bd = np.kron(np.eye(BR_HEADS, dtype=np.float32), np.ones((HEAD_DIM, HEAD_DIM), np.float32))
    return mall, lmst, bd


def _hgrn_chunk(layout, chunk, reverse, q_ref, k_ref, v_ref, lf_ref, mall_ref, lm_ref, bd_ref, o_ref, st_ref):
    c = HG_C
    nl = int(math.log2(c))
    tok = chunk * c
    start, length = layout.seq_bounds(tok)
    fresh = (tok + c == start + length) if reverse else (tok == start)

    masks = _head_mask()
    lf = lf_ref[...]
    hi = lf.astype(BF16)
    mid = (lf - hi.astype(F32)).astype(BF16)
    mall = mall_ref[...]
    cums = _dot(mall, hi) + _dot(mall, mid)
    b = cums[0:c, :]
    b_tot = cums[c:c + 1, :]
    b_rest = b_tot - b
    base = c + 8

    def level_decay(lev):
        if lev < HG_MAT_LEVELS:
            return cums[base + lev * c:base + (lev + 1) * c, :]
        w = 1 << lev
        b3 = b.reshape(c // (2 * w), 2 * w, BR_W)
        r = w if reverse else w - 1
        d3 = b3 - b3[:, r:r + 1, :]
        second = lax.broadcasted_iota(jnp.int32, (1, 2 * w, 1), 1) >= w
        query_half = jnp.logical_not(second) if reverse else second
        return jnp.where(query_half, d3, -d3).reshape(c, BR_W)

    qf = q_ref[...].astype(F32)
    kf = k_ref[...].astype(F32)
    v = v_ref[...]
    st = jnp.where(fresh, 0.0, st_ref[...])

    o = _dot_nt((qf * jnp.exp(b)).astype(BF16), st.astype(BF16))
    a_all = _dot_nt(_stack_heads(q_ref[...], masks), k_ref[...]) * lm_ref[0]
    for lev in range(nl):
        dec = jnp.exp(level_decay(lev))
        ql = (qf * dec).astype(BF16)
        kl = (kf * dec).astype(BF16)
        a_all = a_all + _dot_nt(_stack_heads(ql, masks), kl) * lm_ref[lev + 1]
    a_bf = a_all.astype(BF16)
    zero = jnp.zeros_like(v)
    for h, mk in enumerate(masks):
        o = o + _dot(a_bf[h * c:(h + 1) * c, :], jnp.where(mk, v, zero))
    o_ref[...] = o

    kst = (kf * jnp.exp(b_rest)).astype(BF16)
    vt = v.astype(F32).T.astype(BF16)
    st_ref[...] = (st * jnp.exp(b_tot) + _dot(vt, kst)) * bd_ref[...]


def _hgrn_body(layout, nct, qf_ref, kf_ref, vf_ref, lff_ref, qb_ref, kb_ref, vb_ref, lfb_ref,
               mallf_ref, lmf_ref, mallb_ref, lmb_ref, bd_ref, of_ref, ob_ref, stf_ref, stb_ref):
    i = pl.program_id(0)
    _hgrn_chunk(layout, i, False, qf_ref, kf_ref, vf_ref, lff_ref, mallf_ref, lmf_ref, bd_ref, of_ref, stf_ref)
    _hgrn_chunk(layout, nct - 1 - i, True, qb_ref, kb_ref, vb_ref, lfb_ref, mallb_ref, lmb_ref, bd_ref,
                ob_ref, stb_ref)


def _hgrn_norm_body(of_ref, ob_ref, g64_ref, gn_ref, o_ref):
    o_ref[...] = (_group_rms(of_ref[...] + ob_ref[...], g64_ref[...]) * gn_ref[...]).astype(BF16)


def _hgrn(layout, qs, v, kk, logf, consts_f, consts_b, g64, gn):
    t = qs.shape[0]
    c = HG_C
    nct = t // c
    mall_f, lm_f, bd = consts_f
    mall_b, lm_b, _ = consts_b

    def specs(reverse):
        cm = (lambda i: nct - 1 - i) if reverse else (lambda i: i)
        d = 1 if reverse else 0
        tile = pl.BlockSpec((c, BR_W), lambda i: (cm(i), 0))
        half = pl.BlockSpec((c, BR_W), lambda i: (cm(i), d))
        return tile, half

    tile_f, half_f = specs(False)
    tile_b, half_b = specs(True)
    o_f, o_b = pl.pallas_call(
        functools.partial(_hgrn_body, layout, nct),
        grid=(nct,),
        in_specs=[tile_f, half_f, tile_f, half_f, tile_b, half_b, tile_b, half_b,
                  _const_spec(mall_f.shape), _const_spec(lm_f.shape), _const_spec(mall_b.shape),
                  _const_spec(lm_b.shape), _const_spec(bd.shape)],
        out_specs=[tile_f, tile_b],
        out_shape=[jax.ShapeDtypeStruct((t, BR_W), F32)] * 2,
        scratch_shapes=[pltpu.VMEM((BR_W, BR_W), F32)] * 2,
        compiler_params=_params(("arbitrary",)),
        name="hgrn_scan",
    )(qs, kk, v, logf, qs, kk, v, logf, mall_f, lm_f, mall_b, lm_b, bd)

    tm = PROJ_TM
    tok = pl.BlockSpec((tm, BR_W), lambda i: (i, 0))
    return pl.pallas_call(
        _hgrn_norm_body,
        grid=(t // tm,),
        in_specs=[tok, tok, _const_spec(g64.shape), _const_spec((1, BR_W))],
        out_specs=tok,
        out_shape=jax.ShapeDtypeStruct((t, BR_W), BF16),
        compiler_params=_params(("parallel",)),
        name="hgrn_norm",
    )(o_f, o_b, g64, gn)


def _diff_body(nkt, q_ref, k_ref, vt_ref, bias_ref, sc_ref, gcol_ref, o_ref, qm_ref, m_ref, acc_ref,
               s0_ref, s1_ref, mx0_ref, mx1_ref, p_ref):
    tq = DF_T
    qi = pl.program_id(1)
    qt = q_ref[...].astype(F32).T.astype(BF16)
    rowid = lax.broadcasted_iota(jnp.int32, (DF_KH, tq), 0)
    for hc in range(2 * BR_HEADS):
        half, sub = divmod(hc * DF_DK, DF_KH)
        sel = (rowid >= sub) & (rowid < sub + DF_DK)
        qh = qt[half * DF_KH:(half + 1) * DF_KH, :]
        qm_ref[hc] = jnp.where(sel, qh, jnp.zeros_like(qh))
    m_ref[...] = jnp.full(m_ref.shape, NEG, F32)
    acc_ref[...] = jnp.zeros_like(acc_ref)

    nu = 2 * BR_HEADS
    rows = [(r, r + DF_RC) for r in range(0, tq, DF_RC)]

    def qk_unit(kj, hc, buf, near=None):
        sbuf, mxbuf = buf
        half = hc * DF_DK // DF_KH
        s = _dot(k_ref[kj, :, half * DF_KH:(half + 1) * DF_KH], qm_ref[hc])
        if near is not None:
            s = s + bias_ref[near, hc // 2]
        sbuf[hc] = s
        mxbuf[hc] = jnp.max(s.reshape(tq // 8, 8, tq), axis=0)

    def softmax_pv_unit(kj, hc, buf, const_bias):
        sbuf, mxbuf = buf
        c = const_bias(hc // 2)
        m_old = m_ref[hc]
        m_new = jnp.maximum(m_old, jnp.max(mxbuf[hc], axis=0, keepdims=True) + c)
        shift = m_new - c
        p = jnp.concatenate([jnp.exp2(sbuf[hc, r0:r1, :] - shift).astype(BF16) for r0, r1 in rows], axis=0)
        m_ref[hc] = m_new
        acc_ref[hc] = acc_ref[hc] * jnp.exp2(m_old - m_new) + _dot(vt_ref[kj, hc // 2], p)

    def step(kj, buf, const_bias, nxt):
        for hc in range(nu):
            qk_unit(nxt[0], hc, nxt[1], nxt[2])
            softmax_pv_unit(kj, hc, buf, const_bias)

    last = nkt - 1
    n_low = jnp.maximum(qi - 1, 0)
    n_far = n_low + jnp.maximum(nkt - qi - 2, 0)
    buf_a = (s0_ref, mx0_ref)
    buf_b = (s1_ref, mx1_ref)

    def far_tile(f):
        return jnp.minimum(jnp.where(f < n_low, f, f - n_low + qi + 2), last)

    def dead_if(cond):
        pen = jnp.where(cond, 2.0 * NEG, 0.0)
        return lambda h: pen

    near = [jnp.clip(qi + d - 1, 0, last) for d in range(3)]
    for hc in range(nu):
        qk_unit(near[0], hc, buf_a, 0)
    step(near[0], buf_a, dead_if(qi == 0), (near[1], buf_b, 1))
    step(near[1], buf_b, dead_if(False), (near[2], buf_a, 2))
    step(near[2], buf_a, dead_if(qi == last), (far_tile(0), buf_b, None))

    def far_step(f, cur, nxt):
        kj = far_tile(f)
        row = jnp.where(kj < qi, 0, 1)
        live = f < n_far
        step(kj, cur, lambda h: jnp.where(live, sc_ref[row, h], 2.0 * NEG), (far_tile(f + 1), nxt, None))

    far_step(0, buf_b, buf_a)

    def body(i, carry):
        f0 = 1 + DF_UNROLL * i
        for u in range(DF_UNROLL):
            cur, nxt = (buf_a, buf_b) if u % 2 == 0 else (buf_b, buf_a)
            far_step(f0 + u, cur, nxt)
        return carry
    lax.fori_loop(0, (n_far - 1 + DF_UNROLL - 1) // DF_UNROLL, body, 0)

    lmb = sc_ref[2, 0]
    outs = []
    for h in range(BR_HEADS):
        a0 = acc_ref[2 * h]
        a1 = acc_ref[2 * h + 1]
        o0 = a0[:HEAD_DIM, :] / a0[HEAD_DIM:HEAD_DIM + 1, :]
        o1 = a1[:HEAD_DIM, :] / a1[HEAD_DIM:HEAD_DIM + 1, :]
        o = o0 - lmb * o1
        ms = jnp.mean(o * o, axis=0, keepdims=True)
        outs.append(o * lax.rsqrt(ms + EPS))
    ot = jnp.concatenate(outs, axis=0) * gcol_ref[...]
    o_ref[...] = ot.T.astype(BF16)


def _diff_attn_class(off, nseq, n, qt, k3, vt4, bias, sc, gcol):
    t = DF_T
    nkt = n // t
    qb = off // t
    sb = off // n
    return pl.pallas_call(
        functools.partial(_diff_body, nkt),
        grid=(nseq, nkt),
        in_specs=[pl.BlockSpec((t, BR_W), lambda s, i: (qb + s * nkt + i, 0)),
                  pl.BlockSpec((nkt, t, BR_W), lambda s, i: (sb + s, 0, 0), pipeline_mode=pl.Buffered(1)),
                  pl.BlockSpec((nkt, BR_HEADS, DF_VROWS, t), lambda s, i: (sb + s, 0, 0, 0),
                               pipeline_mode=pl.Buffered(1)),
                  _const_spec(bias.shape),
                  pl.BlockSpec(memory_space=pltpu.SMEM),
                  _const_spec(gcol.shape)],
        out_specs=pl.BlockSpec((t, BR_W), lambda s, i: (s * nkt + i, 0)),
        out_shape=jax.ShapeDtypeStruct((nseq * n, BR_W), BF16),
        scratch_shapes=[pltpu.VMEM((2 * BR_HEADS, DF_KH, t), BF16),
                        pltpu.VMEM((2 * BR_HEADS, 1, t), F32),
                        pltpu.VMEM((2 * BR_HEADS, DF_VROWS, t), F32),
                        pltpu.VMEM((2 * BR_HEADS, t, t), F32),
                        pltpu.VMEM((2 * BR_HEADS, t, t), F32),
                        pltpu.VMEM((2 * BR_HEADS, 8, t), F32),
                        pltpu.VMEM((2 * BR_HEADS, 8, t), F32),
                        pltpu.VMEM((2 * BR_HEADS, t, t), BF16)],
        compiler_params=_params(("parallel", "parallel")),
        name="diff_attn",
    )(qt, k3, vt4, bias, sc, gcol)


def _diff_attn(layout, q, k, vt4, bias, sc, gcol):
    t = k.shape[0]
    k3 = k.reshape(t // DF_T, DF_T, BR_W)
    outs = [_diff_attn_class(off, nseq, n, q, k3, vt4, bias, sc, gcol) for off, nseq, n in layout.classes]
    return jnp.concatenate(outs, axis=0)


def _win_body(layout, q_ref, kp_ref, kc_ref, kn_ref, vp_ref, vc_ref, vn_ref, bias_ref, sink_ref, o_ref):
    blk = WIN_BLOCK
    i = pl.program_id(0)
    tok = i * (WIN_NB * blk)
    start, length = layout.seq_bounds(tok)
    masks = _head_mask()
    kcat = jnp.concatenate([kp_ref[...], kc_ref[...], kn_ref[...]], axis=0)
    vcat = jnp.concatenate([vp_ref[...], vc_ref[...], vn_ref[...]], axis=0)
    col = lax.broadcasted_iota(jnp.int32, (1, 3 * blk), 1)
    sink = sink_ref[...]
    bias = bias_ref[...]
    subs = range(WIN_NB)

    def dead_cols(j):
        has_prev = tok + j * blk > start
        has_next = tok + (j + 1) * blk < start + length
        return ((col < blk) & jnp.logical_not(has_prev)) | ((col >= 2 * blk) & jnp.logical_not(has_next))

    s = [_dot_nt(_stack_heads(q_ref[j * blk:(j + 1) * blk, :], masks), kcat[j * blk:(j + 3) * blk, :]) for j in subs]
    s = [s[j] + bias + jnp.where(dead_cols(j), NEG, 0.0) for j in subs]
    m = [jnp.maximum(jnp.max(s[j], axis=-1, keepdims=True), sink) for j in subs]
    p = [jnp.exp2(s[j] - m[j]) for j in subs]
    den = [jnp.sum(p[j], axis=-1, keepdims=True) + jnp.exp2(sink - m[j]) for j in subs]
    pn = [(p[j] * (1.0 / den[j])).astype(BF16) for j in subs]
    o_all = [_dot(pn[j], vcat[j * blk:(j + 3) * blk, :]) for j in subs]
    for j in subs:
        o_ref[j * blk:(j + 1) * blk, :] = _unstack_heads(o_all[j], masks, blk).astype(BF16)


def _win_attn(layout, q, k, v, bias, sink):
    t = q.shape[0]
    blk = WIN_BLOCK
    nb = t // blk
    cur = pl.BlockSpec((WIN_NB * blk, BR_W), lambda i: (i, 0))
    prev = pl.BlockSpec((blk, BR_W), lambda i: (jnp.maximum(i * WIN_NB - 1, 0), 0))
    nxt = pl.BlockSpec((blk, BR_W), lambda i: (jnp.minimum((i + 1) * WIN_NB, nb - 1), 0))
    return pl.pallas_call(
        functools.partial(_win_body, layout),
        grid=(nb // WIN_NB,),
        in_specs=[cur, prev, cur, nxt, prev, cur, nxt, _const_spec(bias.shape), _const_spec(sink.shape)],
        out_specs=cur,
        out_shape=jax.ShapeDtypeStruct((t, BR_W), BF16),
        compiler_params=_params(("parallel",)),
        name="win_attn",
    )(q, k, k, k, v, v, v, bias, sink)


def _na_body(rows, q_ref, k_ref, v_ref, bias_ref, o_ref):
    j = pl.program_id(1)
    masks = _head_mask()
    nk = NA_KH * GRID_W
    rr = range(NA_RB)
    r = [j * NA_RB + i for i in rr]
    rs = [jnp.clip(r[i] - NA_KH // 2, 0, rows - NA_KH) for i in rr]
    koff = [pl.multiple_of(rs[i] * GRID_W, GRID_W) for i in rr]
    s = [_dot_nt(_stack_heads(q_ref[i * GRID_W:(i + 1) * GRID_W, :], masks), k_ref[pl.ds(koff[i], nk), :])
         + bias_ref[r[i] - rs[i]] for i in rr]
    m = [jnp.max(s[i], axis=-1, keepdims=True) for i in rr]
    p = [jnp.exp2(s[i] - m[i]) for i in rr]
    pn = [(p[i] * (1.0 / jnp.sum(p[i], axis=-1, keepdims=True))).astype(BF16) for i in rr]
    o_all = [_dot(pn[i], v_ref[pl.ds(koff[i], nk), :]) for i in rr]
    for i in rr:
        o_ref[i * GRID_W:(i + 1) * GRID_W, :] = _unstack_heads(o_all[i], masks, GRID_W).astype(BF16)


def _na_class(off, nseq, n, q, k, v, bias):
    rows = n // GRID_W
    qt = NA_RB * GRID_W
    nj = n // qt
    qb = off // qt
    sb = off // n
    seq = pl.BlockSpec((n, BR_W), lambda s, j: (sb + s, 0), pipeline_mode=pl.Buffered(1))
    return pl.pallas_call(
        functools.partial(_na_body, rows),
        grid=(nseq, nj),
        in_specs=[pl.BlockSpec((qt, BR_W), lambda s, j: (qb + s * nj + j, 0)), seq, seq, _const_spec(bias.shape)],
        out_specs=pl.BlockSpec((qt, BR_W), lambda s, j: (s * nj + j, 0)),
        out_shape=jax.ShapeDtypeStruct((nseq * n, BR_W), BF16),
        compiler_params=_params(("parallel", "parallel")),
        name="na_attn",
    )(q, k, v, bias)


def _na_attn(layout, q, k, v, bias):
    return jnp.concatenate([_na_class(off, nseq, n, q, k, v, bias) for off, nseq, n in layout.classes], axis=0)


def _mem_body(q_ref, mk_ref, mv_ref, o_ref):
    masks = _head_mask()
    sub = MEM_TM // MEM_SPLIT
    parts = range(MEM_SPLIT)
    mk = mk_ref[0]
    mv = mv_ref[0]
    s = [_dot_nt(_stack_heads(q_ref[i * sub:(i + 1) * sub, :], masks), mk) for i in parts]
    m = [jnp.max(s[i], axis=-1, keepdims=True) for i in parts]
    p = [jnp.exp2(s[i] - m[i]) for i in parts]
    pn = [(p[i] * (1.0 / jnp.sum(p[i], axis=-1, keepdims=True))).astype(BF16) for i in parts]
    o_all = [_dot(pn[i], mv) for i in parts]
    for i in parts:
        o_ref[i * sub:(i + 1) * sub, :] = _unstack_heads(o_all[i], masks, sub).astype(BF16)


def _mem_attn(layout, q, mk, mv):
    t = q.shape[0]
    tm = MEM_TM
    mem_len = mk.shape[1]
    tile = pl.BlockSpec((tm, BR_W), lambda i: (i, 0))
    mem = pl.BlockSpec((1, mem_len, BR_W), lambda i: (layout.seq_index(i * tm), 0, 0))
    return pl.pallas_call(
        _mem_body,
        grid=(t // tm,),
        in_specs=[tile, mem, mem],
        out_specs=tile,
        out_shape=jax.ShapeDtypeStruct((t, BR_W), BF16),
        compiler_params=_params(("parallel",)),
        name="mem_attn",
    )(q, mk, mv)


def _merge_body(x_ref, ng_ref, oa_ref, ob_ref, oc_ref, od_ref, oe_ref, wg_ref, wm_ref, wb_ref, wo_ref, y_ref):
    x = x_ref[...]
    h = _rms_rows(x, ng_ref[...]).astype(BF16)
    merged = jnp.zeros((x.shape[0], D_MODEL), F32)
    for kb, o_ref in enumerate((oa_ref, ob_ref, oc_ref, od_ref, oe_ref)):
        g = _dot(h, wg_ref[:, kb * BR_W:(kb + 1) * BR_W])
        br = (o_ref[...].astype(F32) * (g * jax.nn.sigmoid(g))).astype(BF16)
        mg = jax.nn.sigmoid(_dot(h, wm_ref[:, kb * D_MODEL:(kb + 1) * D_MODEL]))
        merged = merged + mg * _dot(br, wb_ref[kb])
    y_ref[...] = x + _dot(merged.astype(BF16), wo_ref[...])


def _merge(x, ng, branches, wg, wm, wb, wo):
    t = x.shape[0]
    tm = MERGE_TM
    xt = pl.BlockSpec((tm, D_MODEL), lambda i: (i, 0))
    bt = pl.BlockSpec((tm, BR_W), lambda i: (i, 0))
    return pl.pallas_call(
        _merge_body,
        grid=(t // tm,),
        in_specs=[xt, _const_spec((1, D_MODEL))] + [bt] * N_BRANCH
        + [_const_spec(wg.shape), _const_spec(wm.shape), _const_spec(wb.shape), _const_spec(wo.shape)],
        out_specs=xt,
        out_shape=jax.ShapeDtypeStruct((t, D_MODEL), F32),
        input_output_aliases={0: 0},
        compiler_params=_params(("parallel",)),
        name="merge",
    )(x, ng, *branches, wg, wm, wb, wo)


def _t5_bucket(rel):
    half = N_BUCKETS // 2
    exact = half // 2
    n = jnp.abs(rel)
    nf = jnp.maximum(n, 1).astype(F32)
    large = exact + (jnp.log(nf / exact) / math.log(MAX_DIST / exact) * (half - exact)).astype(jnp.int32)
    large = jnp.clip(large, 0, half - 1)
    return jnp.where(rel > 0, half, 0) + jnp.where(n < exact, n, large)


def _group_matrix(group):
    return jnp.asarray(np.kron(np.eye(BR_W // group), np.full((group, group), 1.0 / group)), BF16)


def _lookup(table, idx):
    onehot = (idx[..., None] == jnp.arange(table.shape[0])).astype(F32)
    return jnp.dot(onehot, table, precision=lax.Precision.HIGHEST)


def _diff_bias_tables(rel_bias):
    t = DF_T
    table = rel_bias[:, :BR_HEADS].astype(F32) * LOG2E
    kl = jnp.arange(t)[:, None]
    ql = jnp.arange(t)[None, :]
    rel = jnp.stack([kl - ql + d * t for d in (-1, 0, 1)])
    tiles = _lookup(table, _t5_bucket(rel)).transpose(0, 3, 1, 2)
    far = _lookup(table, _t5_bucket(jnp.asarray([-2 * t, 2 * t], jnp.int32)))
    return tiles, far


def _win_bias_table(rel_bias):
    rel = jnp.arange(3 * WIN_BLOCK)[None, :] - WIN_BLOCK - jnp.arange(WIN_BLOCK)[:, None]
    bias = _lookup(rel_bias[:, BR_HEADS:].astype(F32) * LOG2E, _t5_bucket(rel)).transpose(2, 0, 1)
    bias = jnp.where((jnp.abs(rel) <= WIN)[None], bias, NEG)
    return bias.reshape(BR_HEADS * WIN_BLOCK, 3 * WIN_BLOCK)


def _na_bias_tables(rpb):
    depth = rpb.shape[0]
    col = np.arange(GRID_W)
    cs = np.clip(col - NA_KW // 2, 0, GRID_W - NA_KW)
    inwin = (col[None, :] >= cs[:, None]) & (col[None, :] < cs[:, None] + NA_KW)
    dc = col[None, :] - col[:, None] + (NA_KW - 1)
    onehot = jnp.asarray(dc[None] == np.arange(2 * NA_KW - 1)[:, None, None], F32)
    toep = jnp.einsum("lhrc,cqk->lhrqk", rpb.astype(F32) * LOG2E, onehot, precision=lax.Precision.HIGHEST)
    toep = jnp.where(jnp.asarray(inwin)[None, None, None], toep, NEG)
    out = []
    for d in range(NA_KH):
        lo = NA_KH - 1 - d
        b = toep[:, :, lo:lo + NA_KH].transpose(0, 1, 3, 2, 4)
        out.append(b.reshape(depth, BR_HEADS * GRID_W, NA_KH * GRID_W))
    return jnp.stack(out, axis=1)


def _tile_gain(g, reps, scale=1.0):
    return jnp.tile(g.astype(F32), reps)[None, :] * scale


def kernel(x_prompt, x_sample, mem_prompt, mem_sample, norm_g, mem_norm_g, w_in, w_mem_kv, rel_bias, hgrn_lb,
           hgrn_norm_g, diff_qk_g, diff_lambda, diff_subln_g, win_qk_g, win_sink, na_qk_g, na_rpb, mem_qk_g,
           w_branch, w_out):
    depth = w_in.shape[0]
    bp, lp, _ = x_prompt.shape
    bs, ls, _ = x_sample.shape
    mem_len = mem_prompt.shape[1]
    layout = _Layout(bp, lp, bs, ls)
    t = layout.total
    assert t % PROJ_TM == 0 and lp % DF_T == 0 and ls % DF_T == 0
    assert lp % (NA_RB * GRID_W) == 0 and ls % (NA_RB * GRID_W) == 0 and lp % MEM_TM == 0 and ls % MEM_TM == 0
    assert lp % (WIN_NB * WIN_BLOCK) == 0 and ls % (WIN_NB * WIN_BLOCK) == 0

    x = jnp.concatenate([x_prompt.reshape(bp * lp, D_MODEL), x_sample.reshape(bs * ls, D_MODEL)], axis=0)
    mem = jnp.concatenate([mem_prompt.reshape(bp * mem_len, D_MODEL), mem_sample.reshape(bs * mem_len, D_MODEL)], axis=0)

    sm = jax.nn.softmax(hgrn_lb.astype(F32), axis=1)
    lb_all = jnp.clip(jnp.cumsum(sm, axis=1) - sm[:, :1], 0.0, 1.0 - 1e-6)
    lam_init = jnp.asarray([0.8 - 0.6 * math.exp(-0.3 * l) for l in range(depth)], F32)
    lam = diff_lambda.astype(F32)
    lmb = jnp.exp(jnp.sum(lam[:, 0] * lam[:, 1], axis=-1)) - jnp.exp(jnp.sum(lam[:, 2] * lam[:, 3], axis=-1)) + lam_init

    w_in_b = w_in.astype(BF16)

    def expand_kv(w):
        w = w.reshape(depth, D_MODEL, WIN_KV_HEADS, HEAD_DIM)
        return jnp.repeat(w, BR_HEADS // WIN_KV_HEADS, axis=2).reshape(depth, D_MODEL, BR_W)

    w1 = jnp.concatenate([w_in_b[:, :, A_Q:A_G], w_in_b[:, :, B_Q:B_G], w_in_b[:, :, C_Q:C_K],
                          expand_kv(w_in_b[:, :, C_K:C_V]), expand_kv(w_in_b[:, :, C_V:C_G]),
                          w_in_b[:, :, D_Q:D_G], w_in_b[:, :, E_Q:E_G]], axis=-1)
    wvt = w_in_b[:, :, B_Q + 2 * BR_W:B_G].transpose(0, 2, 1)
    wg = jnp.concatenate([w_in_b[:, :, A_G:B_Q], w_in_b[:, :, B_G:C_Q], w_in_b[:, :, C_G:D_Q],
                          w_in_b[:, :, D_G:E_Q], w_in_b[:, :, E_G:M_G]], axis=-1)
    wm = w_in_b[:, :, M_G:]
    wb = w_branch.astype(BF16)
    wo = w_out.astype(BF16)
    wmem = w_mem_kv.astype(BF16)

    sc_b = DF_DK ** -0.5 * LOG2E
    sc_h = HEAD_DIM ** -0.5 * LOG2E
    zrow = jnp.zeros((depth, 1, BR_W), F32)

    def per_layer(fn):
        return jnp.stack([fn(l) for l in range(depth)])

    ep = jnp.concatenate([
        per_layer(lambda l: _tile_gain(diff_qk_g[l, 0], 8, sc_b)),
        per_layer(lambda l: _tile_gain(diff_qk_g[l, 1], 8)),
        per_layer(lambda l: _tile_gain(win_qk_g[l, 0], 4, sc_h)),
        per_layer(lambda l: _tile_gain(win_qk_g[l, 1], 4)),
        per_layer(lambda l: _tile_gain(na_qk_g[l, 0], 4, sc_h)),
        per_layer(lambda l: _tile_gain(na_qk_g[l, 1], 4)),
        per_layer(lambda l: _tile_gain(mem_qk_g[l, 0], 4, sc_h)),
        lb_all[0][:, None, :], lb_all[1][:, None, :]] + [zrow] * 7, axis=1)
    gk_mem = per_layer(lambda l: _tile_gain(mem_qk_g[l, 1], 4))
    gn_hg = hgrn_norm_g.astype(F32)[:, None, :]
    gcol = per_layer(lambda l: (jnp.tile(diff_subln_g[l].astype(F32), BR_HEADS) * (1.0 - lam_init[l]))[:, None])
    sink = per_layer(lambda l: jnp.repeat(win_sink[l].astype(F32) * LOG2E, WIN_BLOCK)[:, None])
    na_bias = _na_bias_tables(na_rpb)

    diff_bias, diff_far = _diff_bias_tables(rel_bias)
    win_bias = _win_bias_table(rel_bias)
    sc = jnp.concatenate([jnp.broadcast_to(diff_far[None], (depth, 2, BR_HEADS)),
                          jnp.broadcast_to(lmb[:, None, None], (depth, 1, BR_HEADS))], axis=1)

    g32 = _group_matrix(DF_DK)
    g64 = _group_matrix(HEAD_DIM)
    consts_f = tuple(jnp.asarray(a, dt) for a, dt in zip(_hgrn_constants(HG_C, False), (BF16, F32, F32)))
    consts_b = tuple(jnp.asarray(a, dt) for a, dt in zip(_hgrn_constants(HG_C, True), (BF16, F32, F32)))

    def layer(x, p):
        ng = p["ng"]
        (a_q, a_v, a_lf, a_kk, b_q, b_k, b_v, c_q, c_k, c_v, d_q, d_k, d_v, e_q) = _project(
            x, ng, p["w1"], p["wvt"], p["ep"], g32, g64)
        mk, mv = _mem_kv(mem, p["mng"], p["wmem"], p["gk_mem"], g64)
        o_a = _hgrn(layout, a_q, a_v, a_kk, a_lf, consts_f, consts_b, g64, p["gn_hg"])
        o_b = _diff_attn(layout, b_q, b_k, b_v, diff_bias, p["sc"], p["gcol"])
        o_c = _win_attn(layout, c_q, c_k, c_v, win_bias, p["sink"])
        o_d = _na_attn(layout, d_q, d_k, d_v, p["na_bias"])
        o_e = _mem_attn(layout, e_q, mk.reshape(layout.nseq, mem_len, BR_W), mv.reshape(layout.nseq, mem_len, BR_W))
        y = _merge(x, ng, (o_a, o_b, o_c, o_d, o_e), p["wg"], p["wm"], p["wb"], p["wo"])
        return y, None

    params = dict(ng=norm_g.astype(F32)[:, None, :], mng=mem_norm_g.astype(F32)[:, None, :], w1=w1, wvt=wvt, ep=ep,
                  wmem=wmem, gk_mem=gk_mem, gn_hg=gn_hg, sc=sc, gcol=gcol, sink=sink, na_bias=na_bias,
                  wg=wg, wm=wm, wb=wb, wo=wo)
    x, _ = lax.scan(layer, x, params)
    y_prompt = x[:layout.off1].reshape(bp, lp, D_MODEL)
    y_sample = x[layout.off1:].reshape(bs, ls, D_MODEL)
    return (y_prompt, y_sample)
```

```python
import functools
import math

import numpy as np
import jax
import jax.numpy as jnp
from jax import lax
from jax.experimental import pallas as pl
from jax.experimental.pallas import tpu as pltpu

F32 = jnp.float32
BF16 = jnp.bfloat16

D_MODEL = 1024
HEAD_DIM = 64
BR_HEADS = 4
BR_W = BR_HEADS * HEAD_DIM
N_BRANCH = 5
DF_DK = HEAD_DIM // 2
WIN = 128
WIN_BLOCK = 128
WIN_KV_HEADS = 2
GRID_W = 64
NA_KH = 8
NA_KW = 16
N_BUCKETS = 32
MAX_DIST = 128
EPS = 1e-6
NEG = -1e30
LB_FLOOR = 1e-30
LOG2E = 1.4426950408889634

A_Q = 0
A_G = 4 * BR_W
B_Q = A_G + BR_W
B_G = B_Q + 3 * BR_W
C_Q = B_G + BR_W
C_K = C_Q + BR_W
C_V = C_K + WIN_KV_HEADS * HEAD_DIM
C_G = C_V + WIN_KV_HEADS * HEAD_DIM
D_Q = C_G + BR_W
D_G = D_Q + 3 * BR_W
E_Q = D_G + BR_W
E_G = E_Q + BR_W
M_G = E_G + BR_W

VMEM_LIMIT_BYTES = 56 * 1024 * 1024

PROJ_TM = 512
MERGE_TM = 512
MEM_TM = 512
MEM_SPLIT = 4
HG_C = 128
HG_MAT_LEVELS = 2
DF_T = 256
NA_RB = 8
WIN_NB = 4
HG_NORM_TM = 2048
DF_VROWS = 80
DF_RC = 64
DF_KH = 128
DF_UNROLL = 4


def _params(sem):
    return pltpu.CompilerParams(dimension_semantics=sem, vmem_limit_bytes=VMEM_LIMIT_BYTES)


def _const_spec(shape):
    nd = len(shape)
    return pl.BlockSpec(shape, lambda *_: (0,) * nd, pipeline_mode=pl.Buffered(1))


def _dot(a, b):
    return jnp.dot(a, b, preferred_element_type=F32)


def _dot_nt(a, b):
    return lax.dot_general(a, b, (((1,), (1,)), ((), ())), preferred_element_type=F32)


def _rms_rows(x, g):
    ms = jnp.mean(x * x, axis=-1, keepdims=True)
    return x * lax.rsqrt(ms + EPS) * g


def _group_rms(x, gmat):
    x2 = x * x
    hi = x2.astype(BF16)
    lo = (x2 - hi.astype(F32)).astype(BF16)
    ms = _dot(hi, gmat) + _dot(lo, gmat)
    return x * lax.rsqrt(ms + EPS)


def _head_mask(width=BR_W):
    lane = lax.broadcasted_iota(jnp.int32, (1, width), 1)
    return [(lane >= h * HEAD_DIM) & (lane < (h + 1) * HEAD_DIM) for h in range(BR_HEADS)]


def _stack_heads(q, masks):
    zero = jnp.zeros_like(q)
    return jnp.concatenate([jnp.where(m, q, zero) for m in masks], axis=0)


def _unstack_heads(o_all, masks, m):
    out = jnp.zeros((m, BR_W), F32)
    for h, mk in enumerate(masks):
        out = out + jnp.where(mk, o_all[h * m:(h + 1) * m, :], 0.0)
    return out


class _Layout:
    def __init__(self, n_prompt, len_prompt, n_sample, len_sample):
        self.classes = ((0, n_prompt, len_prompt), (n_prompt * len_prompt, n_sample, len_sample))
        self.off1 = n_prompt * len_prompt
        self.lp = len_prompt
        self.ls = len_sample
        self.total = self.off1 + n_sample * len_sample
        self.nseq = n_prompt + n_sample
        self.n_prompt = n_prompt
        assert self.off1 % len_sample == 0

    def seq_bounds(self, tok):
        in_p = tok < self.off1
        start_p = (tok // self.lp) * self.lp
        start_s = self.off1 + ((tok - self.off1) // self.ls) * self.ls
        return jnp.where(in_p, start_p, start_s), jnp.where(in_p, self.lp, self.ls)

    def seq_index(self, tok):
        return jnp.where(tok < self.off1, tok // self.lp, self.n_prompt + (tok - self.off1) // self.ls)


def _proj_body(x_ref, ng_ref, w_ref, wvt_ref, ep_ref, g32_ref, g64_ref,
               aq_ref, av_ref, alf_ref, akk_ref, bq_ref, bk_ref, bv_ref,
               cq_ref, ck_ref, cv_ref, dq_ref, dk_ref, dv_ref, eq_ref):
    h = _rms_rows(x_ref[...], ng_ref[...]).astype(BF16)
    ep = ep_ref[...]

    def seg(i):
        return _dot(h, w_ref[:, i * BR_W:(i + 1) * BR_W])

    def row(r):
        return ep[r:r + 1, :]

    aq = seg(0)
    aq_ref[...] = (aq * jax.nn.sigmoid(aq)).astype(BF16)
    av_ref[...] = seg(1).astype(BF16)
    for d in range(2):
        z = seg(2 + d)
        lb = row(7 + d)
        e = jnp.exp(-jnp.abs(z))
        log_sig = jnp.minimum(z, 0.0) - jnp.log1p(e)
        t1 = jnp.log(jnp.maximum(lb, LB_FLOOR))
        t2 = jnp.log1p(-lb) + log_sig
        logf = jnp.maximum(t1, t2) + jnp.log1p(jnp.exp(-jnp.abs(t1 - t2)))
        sig_neg = jnp.where(z >= 0.0, e, 1.0) / (1.0 + e)
        alf_ref[:, d * BR_W:(d + 1) * BR_W] = logf
        akk_ref[:, d * BR_W:(d + 1) * BR_W] = ((1.0 - lb) * sig_neg).astype(BF16)
    g32 = g32_ref[...]
    g64 = g64_ref[...]
    bq_ref[...] = (_group_rms(seg(4), g32) * row(0)).astype(BF16)
    bk_ref[...] = (_group_rms(seg(5), g32) * row(1)).astype(BF16)
    vt = _dot_nt(wvt_ref[...], h).astype(BF16)
    pad_row = lax.broadcasted_iota(jnp.int32, (DF_VROWS - HEAD_DIM, DF_T), 0)
    pad = jnp.where(pad_row == 0, 1.0, 0.0).astype(BF16)
    for j in range(PROJ_TM // DF_T):
        for hd in range(BR_HEADS):
            bv_ref[j, hd, 0:HEAD_DIM, :] = vt[hd * HEAD_DIM:(hd + 1) * HEAD_DIM, j * DF_T:(j + 1) * DF_T]
            bv_ref[j, hd, HEAD_DIM:DF_VROWS, :] = pad
    cq_ref[...] = (_group_rms(seg(7), g64) * row(2)).astype(BF16)
    ck_ref[...] = (_group_rms(seg(8), g64) * row(3)).astype(BF16)
    cv_ref[...] = seg(9).astype(BF16)
    dq_ref[...] = (_group_rms(seg(10), g64) * row(4)).astype(BF16)
    dk_ref[...] = (_group_rms(seg(11), g64) * row(5)).astype(BF16)
    dv_ref[...] = seg(12).astype(BF16)
    eq_ref[...] = (_group_rms(seg(13), g64) * row(6)).astype(BF16)


def _project(x, ng, w1, wvt, ep, g32, g64):
    t = x.shape[0]
    tm = PROJ_TM
    tile = lambda w: pl.BlockSpec((tm, w), lambda i: (i, 0))
    widths = [BR_W, BR_W, 2 * BR_W, 2 * BR_W] + [BR_W] * 10
    dtypes = [BF16, BF16, F32, BF16] + [BF16] * 10
    out_specs = [tile(w) for w in widths]
    out_shape = [jax.ShapeDtypeStruct((t, w), dt) for w, dt in zip(widths, dtypes)]
    kt = tm // DF_T
    out_specs[6] = pl.BlockSpec((kt, BR_HEADS, DF_VROWS, DF_T), lambda i: (i, 0, 0, 0))
    out_shape[6] = jax.ShapeDtypeStruct((t // DF_T, BR_HEADS, DF_VROWS, DF_T), BF16)
    return pl.pallas_call(
        _proj_body,
        grid=(t // tm,),
        in_specs=[tile(D_MODEL), _const_spec((1, D_MODEL)), _const_spec(w1.shape), _const_spec(wvt.shape),
                  _const_spec(ep.shape), _const_spec(g32.shape), _const_spec(g64.shape)],
        out_specs=out_specs,
        out_shape=out_shape,
        compiler_params=_params(("parallel",)),
        name="proj",
    )(x, ng, w1, wvt, ep, g32, g64)


def _memkv_body(m_ref, g_ref, w_ref, gk_ref, g64_ref, mk_ref, mv_ref):
    mh = _rms_rows(m_ref[...], g_ref[...]).astype(BF16)
    kv = _dot(mh, w_ref[...])
    mk_ref[...] = (_group_rms(kv[:, :BR_W], g64_ref[...]) * gk_ref[...]).astype(BF16)
    mv_ref[...] = kv[:, BR_W:].astype(BF16)


def _mem_kv(mem, g, w, gk, g64):
    t = mem.shape[0]
    tm = 256
    return pl.pallas_call(
        _memkv_body,
        grid=(t // tm,),
        in_specs=[pl.BlockSpec((tm, D_MODEL), lambda i: (i, 0)), _const_spec((1, D_MODEL)), _const_spec(w.shape),
                  _const_spec((1, BR_W)), _const_spec(g64.shape)],
        out_specs=[pl.BlockSpec((tm, BR_W), lambda i: (i, 0))] * 2,
        out_shape=[jax.ShapeDtypeStruct((t, BR_W), BF16)] * 2,
        compiler_params=_params(("parallel",)),
        name="memkv",
    )(mem, g, w, gk, g64)


def _hgrn_constants(c, reverse):
    nl = int(math.log2(c))
    idx = np.arange(c)
    t = idx[:, None]
    u = idx[None, :]
    incl = (u <= t).astype(np.float32)
    rest = (u > t).astype(np.float32)
    tot = np.ones((8, c), np.float32)
    mds, mes, lms = [], [], [np.eye(c, dtype=np.float32)]
    for lev in range(nl):
        w = 1 << lev
        blk = idx // w
        odd = (blk % 2 == 1)
        md = (odd[:, None] & (u >= (blk * w)[:, None]) & (u <= t)).astype(np.float32)
        me = ((~odd)[:, None] & (u > t) & (u <= ((blk + 1) * w - 1)[:, None])).astype(np.float32)
        lm = (odd[:, None] & (blk[None, :] == (blk - 1)[:, None])).astype(np.float32)
        mds.append(md)
        mes.append(me)
        lms.append(lm)
    del rest
    mats = [incl, tot] + [md + me for md, me in zip(mds[:HG_MAT_LEVELS], mes[:HG_MAT_LEVELS])]
    if reverse:
        mats = [m[::-1, ::-1] for m in mats]
        lms = [m[::-1, ::-1] for m in lms]
    mall = np.concatenate(mats, axis=0)
    lmst = np.stack([np.tile(m, (1, BR_HEADS)) for m in lms])
    bd = np.kron(np.eye(BR_HEADS, dtype=np.float32), np.ones((HEAD_DIM, HEAD_DIM), np.float32))
    return mall, lmst, bd


def _hgrn_chunk(layout, chunk, reverse, q_ref, k_ref, v_ref, lf_ref, mall_ref, lm_ref, bd_ref, o_ref, st_ref):
    c = HG_C
    nl = int(math.log2(c))
    tok = chunk * c
    start, length = layout.seq_bounds(tok)
    fresh = (tok + c == start + length) if reverse else (tok == start)

    masks = _head_mask()
    lf = lf_ref[...]
    hi = lf.astype(BF16)
    mid = (lf - hi.astype(F32)).astype(BF16)
    mall = mall_ref[...]
    cums = _dot(mall, hi) + _dot(mall, mid)
    b = cums[0:c, :]
    b_tot = cums[c:c + 1, :]
    b_rest = b_tot - b
    base = c + 8

    def level_decay(lev):
        if lev < HG_MAT_LEVELS:
            return cums[base + lev * c:base + (lev + 1) * c, :]
        w = 1 << lev
        b3 = b.reshape(c // (2 * w), 2 * w, BR_W)
        r = w if reverse else w - 1
        d3 = b3 - b3[:, r:r + 1, :]
        second = lax.broadcasted_iota(jnp.int32, (1, 2 * w, 1), 1) >= w
        query_half = jnp.logical_not(second) if reverse else second
        return jnp.where(query_half, d3, -d3).reshape(c, BR_W)

    qf = q_ref[...].astype(F32)
    kf = k_ref[...].astype(F32)
    v = v_ref[...]
    st = jnp.where(fresh, 0.0, st_ref[...])

    o = _dot_nt((qf * jnp.exp(b)).astype(BF16), st.astype(BF16))
    dim_row = lax.broadcasted_iota(jnp.int32, (BR_W, 1), 0)
    head_rows = [(dim_row >= h * HEAD_DIM) & (dim_row < (h + 1) * HEAD_DIM) for h in range(BR_HEADS)]

    def scores(ql, kl):
        kt = kl.T.astype(BF16)
        rhs = jnp.concatenate([jnp.where(hr, kt, jnp.zeros_like(kt)) for hr in head_rows], axis=1)
        return _dot(ql, rhs)

    a_all = scores(q_ref[...], kf) * lm_ref[0]
    for lev in range(nl):
        dec = jnp.exp(level_decay(lev))
        a_all = a_all + scores((qf * dec).astype(BF16), kf * dec) * lm_ref[lev + 1]
    a_bf = a_all.astype(BF16)
    zero = jnp.zeros_like(v)
    for h, mk in enumerate(masks):
        o = o + _dot(a_bf[:, h * c:(h + 1) * c], jnp.where(mk, v, zero))
    o_ref[...] = o

    kst = (kf * jnp.exp(b_rest)).astype(BF16)
    vt = v.astype(F32).T.astype(BF16)
    st_ref[...] = (st * jnp.exp(b_tot) + _dot(vt, kst)) * bd_ref[...]


def _hgrn_body(layout, nct, qf_ref, kf_ref, vf_ref, lff_ref, qb_ref, kb_ref, vb_ref, lfb_ref,
               mallf_ref, lmf_ref, mallb_ref, lmb_ref, bd_ref, of_ref, ob_ref, stf_ref, stb_ref):
    i = pl.program_id(0)
    _hgrn_chunk(layout, i, False, qf_ref, kf_ref, vf_ref, lff_ref, mallf_ref, lmf_ref, bd_ref, of_ref, stf_ref)
    _hgrn_chunk(layout, nct - 1 - i, True, qb_ref, kb_ref, vb_ref, lfb_ref, mallb_ref, lmb_ref, bd_ref,
                ob_ref, stb_ref)


def _hgrn_norm_body(of_ref, ob_ref, g64_ref, gn_ref, o_ref):
    o_ref[...] = (_group_rms(of_ref[...] + ob_ref[...], g64_ref[...]) * gn_ref[...]).astype(BF16)


def _hgrn(layout, qs, v, kk, logf, consts_f, consts_b, g64, gn):
    t = qs.shape[0]
    c = HG_C
    nct = t // c
    mall_f, lm_f, bd = consts_f
    mall_b, lm_b, _ = consts_b

    def specs(reverse):
        cm = (lambda i: nct - 1 - i) if reverse else (lambda i: i)
        d = 1 if reverse else 0
        tile = pl.BlockSpec((c, BR_W), lambda i: (cm(i), 0))
        half = pl.BlockSpec((c, BR_W), lambda i: (cm(i), d))
        return tile, half

    tile_f, half_f = specs(False)
    tile_b, half_b = specs(True)
    o_f, o_b = pl.pallas_call(
        functools.partial(_hgrn_body, layout, nct),
        grid=(nct,),
        in_specs=[tile_f, half_f, tile_f, half_f, tile_b, half_b, tile_b, half_b,
                  _const_spec(mall_f.shape), _const_spec(lm_f.shape), _const_spec(mall_b.shape),
                  _const_spec(lm_b.shape), _const_spec(bd.shape)],
        out_specs=[tile_f, tile_b],
        out_shape=[jax.ShapeDtypeStruct((t, BR_W), F32)] * 2,
        scratch_shapes=[pltpu.VMEM((BR_W, BR_W), F32)] * 2,
        compiler_params=_params(("arbitrary",)),
        name="hgrn_scan",
    )(qs, kk, v, logf, qs, kk, v, logf, mall_f, lm_f, mall_b, lm_b, bd)

    tm = math.gcd(t, HG_NORM_TM)
    tok = pl.BlockSpec((tm, BR_W), lambda i: (i, 0))
    return pl.pallas_call(
        _hgrn_norm_body,
        grid=(t // tm,),
        in_specs=[tok, tok, _const_spec(g64.shape), _const_spec((1, BR_W))],
        out_specs=tok,
        out_shape=jax.ShapeDtypeStruct((t, BR_W), BF16),
        compiler_params=_params(("parallel",)),
        name="hgrn_norm",
    )(o_f, o_b, g64, gn)


def _diff_body(nkt, q_ref, k_ref, vt_ref, bias_ref, sc_ref, gcol_ref, o_ref, qm_ref, m_ref, acc_ref,
               s0_ref, s1_ref, mx0_ref, mx1_ref, p_ref):
    tq = DF_T
    qi = pl.program_id(1)
    qt = q_ref[...].astype(F32).T.astype(BF16)
    rowid = lax.broadcasted_iota(jnp.int32, (DF_KH, tq), 0)
    for hc in range(2 * BR_HEADS):
        half, sub = divmod(hc * DF_DK, DF_KH)
        sel = (rowid >= sub) & (rowid < sub + DF_DK)
        qh = qt[half * DF_KH:(half + 1) * DF_KH, :]
        qm_ref[hc] = jnp.where(sel, qh, jnp.zeros_like(qh))
    m_ref[...] = jnp.full(m_ref.shape, NEG, F32)
    acc_ref[...] = jnp.zeros_like(acc_ref)

    nu = 2 * BR_HEADS
    rows = [(r, r + DF_RC) for r in range(0, tq, DF_RC)]

    def qk_unit(kj, hc, buf, near=None):
        sbuf, mxbuf = buf
        half = hc * DF_DK // DF_KH
        s = _dot(k_ref[kj, :, half * DF_KH:(half + 1) * DF_KH], qm_ref[hc])
        if near is not None:
            s = s + bias_ref[near, hc // 2]
        sbuf[hc] = s
        mxbuf[hc] = jnp.max(s.reshape(tq // 8, 8, tq), axis=0)

    def softmax_pv_unit(kj, hc, buf, const_bias):
        sbuf, mxbuf = buf
        c = const_bias(hc // 2)
        m_old = m_ref[hc]
        m_new = jnp.maximum(m_old, jnp.max(mxbuf[hc], axis=0, keepdims=True) + c)
        shift = m_new - c
        p = jnp.concatenate([jnp.exp2(sbuf[hc, r0:r1, :] - shift).astype(BF16) for r0, r1 in rows], axis=0)
        m_ref[hc] = m_new
        acc_ref[hc] = acc_ref[hc] * jnp.exp2(m_old - m_new) + _dot(vt_ref[kj, hc // 2], p)

    def step(kj, buf, const_bias, nxt):
        for hc in range(nu):
            qk_unit(nxt[0], hc, nxt[1], nxt[2])
            softmax_pv_unit(kj, hc, buf, const_bias)

    last = nkt - 1
    n_low = jnp.maximum(qi - 1, 0)
    n_far = n_low + jnp.maximum(nkt - qi - 2, 0)
    buf_a = (s0_ref, mx0_ref)
    buf_b = (s1_ref, mx1_ref)

    def far_tile(f):
        return jnp.minimum(jnp.where(f < n_low, f, f - n_low + qi + 2), last)

    def dead_if(cond):
        pen = jnp.where(cond, 2.0 * NEG, 0.0)
        return lambda h: pen

    near = [jnp.clip(qi + d - 1, 0, last) for d in range(3)]
    for hc in range(nu):
        qk_unit(near[0], hc, buf_a, 0)
    step(near[0], buf_a, dead_if(qi == 0), (near[1], buf_b, 1))
    step(near[1], buf_b, dead_if(False), (near[2], buf_a, 2))
    step(near[2], buf_a, dead_if(qi == last), (far_tile(0), buf_b, None))

    def far_step(f, cur, nxt):
        kj = far_tile(f)
        row = jnp.where(kj < qi, 0, 1)
        live = f < n_far
        step(kj, cur, lambda h: jnp.where(live, sc_ref[row, h], 2.0 * NEG), (far_tile(f + 1), nxt, None))

    far_step(0, buf_b, buf_a)

    def body(i, carry):
        f0 = 1 + DF_UNROLL * i
        for u in range(DF_UNROLL):
            cur, nxt = (buf_a, buf_b) if u % 2 == 0 else (buf_b, buf_a)
            far_step(f0 + u, cur, nxt)
        return carry
    lax.fori_loop(0, (n_far - 1 + DF_UNROLL - 1) // DF_UNROLL, body, 0)

    lmb = sc_ref[2, 0]
    outs = []
    for h in range(BR_HEADS):
        a0 = acc_ref[2 * h]
        a1 = acc_ref[2 * h + 1]
        o0 = a0[:HEAD_DIM, :] / a0[HEAD_DIM:HEAD_DIM + 1, :]
        o1 = a1[:HEAD_DIM, :] / a1[HEAD_DIM:HEAD_DIM + 1, :]
        o = o0 - lmb * o1
        ms = jnp.mean(o * o, axis=0, keepdims=True)
        outs.append(o * lax.rsqrt(ms + EPS))
    ot = jnp.concatenate(outs, axis=0) * gcol_ref[...]
    o_ref[...] = ot.T.astype(BF16)


def _diff_attn_class(off, nseq, n, qt, k3, vt4, bias, sc, gcol):
    t = DF_T
    nkt = n // t
    qb = off // t
    sb = off // n
    return pl.pallas_call(
        functools.partial(_diff_body, nkt),
        grid=(nseq, nkt),
        in_specs=[pl.BlockSpec((t, BR_W), lambda s, i: (qb + s * nkt + i, 0)),
                  pl.BlockSpec((nkt, t, BR_W), lambda s, i: (sb + s, 0, 0), pipeline_mode=pl.Buffered(1)),
                  pl.BlockSpec((nkt, BR_HEADS, DF_VROWS, t), lambda s, i: (sb + s, 0, 0, 0),
                               pipeline_mode=pl.Buffered(1)),
                  _const_spec(bias.shape),
                  pl.BlockSpec(memory_space=pltpu.SMEM),
                  _const_spec(gcol.shape)],
        out_specs=pl.BlockSpec((t, BR_W), lambda s, i: (s * nkt + i, 0)),
        out_shape=jax.ShapeDtypeStruct((nseq * n, BR_W), BF16),
        scratch_shapes=[pltpu.VMEM((2 * BR_HEADS, DF_KH, t), BF16),
                        pltpu.VMEM((2 * BR_HEADS, 1, t), F32),
                        pltpu.VMEM((2 * BR_HEADS, DF_VROWS, t), F32),
                        pltpu.VMEM((2 * BR_HEADS, t, t), F32),
                        pltpu.VMEM((2 * BR_HEADS, t, t), F32),
                        pltpu.VMEM((2 * BR_HEADS, 8, t), F32),
                        pltpu.VMEM((2 * BR_HEADS, 8, t), F32),
                        pltpu.VMEM((2 * BR_HEADS, t, t), BF16)],
        compiler_params=_params(("parallel", "parallel")),
        name="diff_attn",
    )(qt, k3, vt4, bias, sc, gcol)


def _diff_attn(layout, q, k, vt4, bias, sc, gcol):
    t = k.shape[0]
    k3 = k.reshape(t // DF_T, DF_T, BR_W)
    outs = [_diff_attn_class(off, nseq, n, q, k3, vt4, bias, sc, gcol) for off, nseq, n in layout.classes]
    return jnp.concatenate(outs, axis=0)


def _win_body(layout, q_ref, kp_ref, kc_ref, kn_ref, vp_ref, vc_ref, vn_ref, bias_ref, sink_ref, o_ref):
    blk = WIN_BLOCK
    i = pl.program_id(0)
    tok = i * (WIN_NB * blk)
    start, length = layout.seq_bounds(tok)
    masks = _head_mask()
    kcat = jnp.concatenate([kp_ref[...], kc_ref[...], kn_ref[...]], axis=0)
    vcat = jnp.concatenate([vp_ref[...], vc_ref[...], vn_ref[...]], axis=0)
    col = lax.broadcasted_iota(jnp.int32, (1, 3 * blk), 1)
    sink = sink_ref[...]
    bias = bias_ref[...]
    subs = range(WIN_NB)

    def dead_cols(j):
        has_prev = tok + j * blk > start
        has_next = tok + (j + 1) * blk < start + length
        return ((col < blk) & jnp.logical_not(has_prev)) | ((col >= 2 * blk) & jnp.logical_not(has_next))

    s = [_dot_nt(_stack_heads(q_ref[j * blk:(j + 1) * blk, :], masks), kcat[j * blk:(j + 3) * blk, :]) for j in subs]
    s = [s[j] + bias + jnp.where(dead_cols(j), NEG, 0.0) for j in subs]
    m = [jnp.maximum(jnp.max(s[j], axis=-1, keepdims=True), sink) for j in subs]
    p = [jnp.exp2(s[j] - m[j]) for j in subs]
    den = [jnp.sum(p[j], axis=-1, keepdims=True) + jnp.exp2(sink - m[j]) for j in subs]
    pn = [(p[j] * (1.0 / den[j])).astype(BF16) for j in subs]
    o_all = [_dot(pn[j], vcat[j * blk:(j + 3) * blk, :]) for j in subs]
    for j in subs:
        o_ref[j * blk:(j + 1) * blk, :] = _unstack_heads(o_all[j], masks, blk).astype(BF16)


def _win_attn(layout, q, k, v, bias, sink):
    t = q.shape[0]
    blk = WIN_BLOCK
    nb = t // blk
    cur = pl.BlockSpec((WIN_NB * blk, BR_W), lambda i: (i, 0))
    prev = pl.BlockSpec((blk, BR_W), lambda i: (jnp.maximum(i * WIN_NB - 1, 0), 0))
    nxt = pl.BlockSpec((blk, BR_W), lambda i: (jnp.minimum((i + 1) * WIN_NB, nb - 1), 0))
    return pl.pallas_call(
        functools.partial(_win_body, layout),
        grid=(nb // WIN_NB,),
        in_specs=[cur, prev, cur, nxt, prev, cur, nxt, _const_spec(bias.shape), _const_spec(sink.shape)],
        out_specs=cur,
        out_shape=jax.ShapeDtypeStruct((t, BR_W), BF16),
        compiler_params=_params(("parallel",)),
        name="win_attn",
    )(q, k, k, k, v, v, v, bias, sink)


def _na_body(rows, q_ref, k_ref, v_ref, bias_ref, o_ref):
    j = pl.program_id(1)
    masks = _head_mask()
    nk = NA_KH * GRID_W
    rr = range(NA_RB)
    r = [j * NA_RB + i for i in rr]
    rs = [jnp.clip(r[i] - NA_KH // 2, 0, rows - NA_KH) for i in rr]
    koff = [pl.multiple_of(rs[i] * GRID_W, GRID_W) for i in rr]
    s = [_dot_nt(_stack_heads(q_ref[i * GRID_W:(i + 1) * GRID_W, :], masks), k_ref[pl.ds(koff[i], nk), :])
         + bias_ref[r[i] - rs[i]] for i in rr]
    m = [jnp.max(s[i], axis=-1, keepdims=True) for i in rr]
    p = [jnp.exp2(s[i] - m[i]) for i in rr]
    pn = [(p[i] * (1.0 / jnp.sum(p[i], axis=-1, keepdims=True))).astype(BF16) for i in rr]
    o_all = [_dot(pn[i], v_ref[pl.ds(koff[i], nk), :]) for i in rr]
    for i in rr:
        o_ref[i * GRID_W:(i + 1) * GRID_W, :] = _unstack_heads(o_all[i], masks, GRID_W).astype(BF16)


def _na_class(off, nseq, n, q, k, v, bias):
    rows = n // GRID_W
    qt = NA_RB * GRID_W
    nj = n // qt
    qb = off // qt
    sb = off // n
    seq = pl.BlockSpec((n, BR_W), lambda s, j: (sb + s, 0), pipeline_mode=pl.Buffered(1))
    return pl.pallas_call(
        functools.partial(_na_body, rows),
        grid=(nseq, nj),
        in_specs=[pl.BlockSpec((qt, BR_W), lambda s, j: (qb + s * nj + j, 0)), seq, seq, _const_spec(bias.shape)],
        out_specs=pl.BlockSpec((qt, BR_W), lambda s, j: (s * nj + j, 0)),
        out_shape=jax.ShapeDtypeStruct((nseq * n, BR_W), BF16),
        compiler_params=_params(("parallel", "parallel")),
        name="na_attn",
    )(q, k, v, bias)


def _na_attn(layout, q, k, v, bias):
    return jnp.concatenate([_na_class(off, nseq, n, q, k, v, bias) for off, nseq, n in layout.classes], axis=0)


def _mem_body(q_ref, mk_ref, mv_ref, o_ref):
    masks = _head_mask()
    sub = MEM_TM // MEM_SPLIT
    parts = range(MEM_SPLIT)
    mk = mk_ref[0]
    mv = mv_ref[0]
    s = [_dot_nt(_stack_heads(q_ref[i * sub:(i + 1) * sub, :], masks), mk) for i in parts]
    m = [jnp.max(s[i], axis=-1, keepdims=True) for i in parts]
    p = [jnp.exp2(s[i] - m[i]) for i in parts]
    pn = [(p[i] * (1.0 / jnp.sum(p[i], axis=-1, keepdims=True))).astype(BF16) for i in parts]
    o_all = [_dot(pn[i], mv) for i in parts]
    for i in parts:
        o_ref[i * sub:(i + 1) * sub, :] = _unstack_heads(o_all[i], masks, sub).astype(BF16)


def _mem_attn(layout, q, mk, mv):
    t = q.shape[0]
    tm = MEM_TM
    mem_len = mk.shape[1]
    tile = pl.BlockSpec((tm, BR_W), lambda i: (i, 0))
    mem = pl.BlockSpec((1, mem_len, BR_W), lambda i: (layout.seq_index(i * tm), 0, 0))
    return pl.pallas_call(
        _mem_body,
        grid=(t // tm,),
        in_specs=[tile, mem, mem],
        out_specs=tile,
        out_shape=jax.ShapeDtypeStruct((t, BR_W), BF16),
        compiler_params=_params(("parallel",)),
        name="mem_attn",
    )(q, mk, mv)


def _merge_body(x_ref, ng_ref, oa_ref, ob_ref, oc_ref, od_ref, oe_ref, wg_ref, wm_ref, wb_ref, wo_ref, y_ref):
    x = x_ref[...]
    h = _rms_rows(x, ng_ref[...]).astype(BF16)
    merged = jnp.zeros((x.shape[0], D_MODEL), F32)
    for kb, o_ref in enumerate((oa_ref, ob_ref, oc_ref, od_ref, oe_ref)):
        g = _dot(h, wg_ref[:, kb * BR_W:(kb + 1) * BR_W])
        br = (o_ref[...].astype(F32) * (g * jax.nn.sigmoid(g))).astype(BF16)
        mg = jax.nn.sigmoid(_dot(h, wm_ref[:, kb * D_MODEL:(kb + 1) * D_MODEL]))
        merged = merged + mg * _dot(br, wb_ref[kb])
    y_ref[...] = x + _dot(merged.astype(BF16), wo_ref[...])


def _merge(x, ng, branches, wg, wm, wb, wo):
    t = x.shape[0]
    tm = MERGE_TM
    xt = pl.BlockSpec((tm, D_MODEL), lambda i: (i, 0))
    bt = pl.BlockSpec((tm, BR_W), lambda i: (i, 0))
    return pl.pallas_call(
        _merge_body,
        grid=(t // tm,),
        in_specs=[xt, _const_spec((1, D_MODEL))] + [bt] * N_BRANCH
        + [_const_spec(wg.shape), _const_spec(wm.shape), _const_spec(wb.shape), _const_spec(wo.shape)],
        out_specs=xt,
        out_shape=jax.ShapeDtypeStruct((t, D_MODEL), F32),
        input_output_aliases={0: 0},
        compiler_params=_params(("parallel",)),
        name="merge",
    )(x, ng, *branches, wg, wm, wb, wo)


def _t5_bucket(rel):
    half = N_BUCKETS // 2
    exact = half // 2
    n = jnp.abs(rel)
    nf = jnp.maximum(n, 1).astype(F32)
    large = exact + (jnp.log(nf / exact) / math.log(MAX_DIST / exact) * (half - exact)).astype(jnp.int32)
    large = jnp.clip(large, 0, half - 1)
    return jnp.where(rel > 0, half, 0) + jnp.where(n < exact, n, large)


def _group_matrix(group):
    return jnp.asarray(np.kron(np.eye(BR_W // group), np.full((group, group), 1.0 / group)), BF16)


def _lookup(table, idx):
    onehot = (idx[..., None] == jnp.arange(table.shape[0])).astype(F32)
    return jnp.dot(onehot, table, precision=lax.Precision.HIGHEST)


def _diff_bias_tables(rel_bias):
    t = DF_T
    table = rel_bias[:, :BR_HEADS].astype(F32) * LOG2E
    kl = jnp.arange(t)[:, None]
    ql = jnp.arange(t)[None, :]
    rel = jnp.stack([kl - ql + d * t for d in (-1, 0, 1)])
    tiles = _lookup(table, _t5_bucket(rel)).transpose(0, 3, 1, 2)
    far = _lookup(table, _t5_bucket(jnp.asarray([-2 * t, 2 * t], jnp.int32)))
    return tiles, far


def _win_bias_table(rel_bias):
    rel = jnp.arange(3 * WIN_BLOCK)[None, :] - WIN_BLOCK - jnp.arange(WIN_BLOCK)[:, None]
    bias = _lookup(rel_bias[:, BR_HEADS:].astype(F32) * LOG2E, _t5_bucket(rel)).transpose(2, 0, 1)
    bias = jnp.where((jnp.abs(rel) <= WIN)[None], bias, NEG)
    return bias.reshape(BR_HEADS * WIN_BLOCK, 3 * WIN_BLOCK)


def _na_bias_tables(rpb):
    depth = rpb.shape[0]
    col = np.arange(GRID_W)
    cs = np.clip(col - NA_KW // 2, 0, GRID_W - NA_KW)
    inwin = (col[None, :] >= cs[:, None]) & (col[None, :] < cs[:, None] + NA_KW)
    dc = col[None, :] - col[:, None] + (NA_KW - 1)
    onehot = jnp.asarray(dc[None] == np.arange(2 * NA_KW - 1)[:, None, None], F32)
    toep = jnp.einsum("lhrc,cqk->lhrqk", rpb.astype(F32) * LOG2E, onehot, precision=lax.Precision.HIGHEST)
    toep = jnp.where(jnp.asarray(inwin)[None, None, None], toep, NEG)
    out = []
    for d in range(NA_KH):
        lo = NA_KH - 1 - d
        b = toep[:, :, lo:lo + NA_KH].transpose(0, 1, 3, 2, 4)
        out.append(b.reshape(depth, BR_HEADS * GRID_W, NA_KH * GRID_W))
    return jnp.stack(out, axis=1)


def _tile_gain(g, reps, scale=1.0):
    return jnp.tile(g.astype(F32), reps)[None, :] * scale


def kernel(x_prompt, x_sample, mem_prompt, mem_sample, norm_g, mem_norm_g, w_in, w_mem_kv, rel_bias, hgrn_lb,
           hgrn_norm_g, diff_qk_g, diff_lambda, diff_subln_g, win_qk_g, win_sink, na_qk_g, na_rpb, mem_qk_g,
           w_branch, w_out):
    depth = w_in.shape[0]
    bp, lp, _ = x_prompt.shape
    bs, ls, _ = x_sample.shape
    mem_len = mem_prompt.shape[1]
    layout = _Layout(bp, lp, bs, ls)
    t = layout.total
    assert t % PROJ_TM == 0 and t % MERGE_TM == 0 and lp % DF_T == 0 and ls % DF_T == 0
    assert lp % (NA_RB * GRID_W) == 0 and ls % (NA_RB * GRID_W) == 0 and lp % MEM_TM == 0 and ls % MEM_TM == 0
    assert lp % (WIN_NB * WIN_BLOCK) == 0 and ls % (WIN_NB * WIN_BLOCK) == 0

    x = jnp.concatenate([x_prompt.reshape(bp * lp, D_MODEL), x_sample.reshape(bs * ls, D_MODEL)], axis=0)
    mem = jnp.concatenate([mem_prompt.reshape(bp * mem_len, D_MODEL), mem_sample.reshape(bs * mem_len, D_MODEL)], axis=0)

    sm = jax.nn.softmax(hgrn_lb.astype(F32), axis=1)
    lb_all = jnp.clip(jnp.cumsum(sm, axis=1) - sm[:, :1], 0.0, 1.0 - 1e-6)
    lam_init = jnp.asarray([0.8 - 0.6 * math.exp(-0.3 * l) for l in range(depth)], F32)
    lam = diff_lambda.astype(F32)
    lmb = jnp.exp(jnp.sum(lam[:, 0] * lam[:, 1], axis=-1)) - jnp.exp(jnp.sum(lam[:, 2] * lam[:, 3], axis=-1)) + lam_init

    w_in_b = w_in.astype(BF16)

    def expand_kv(w):
        w = w.reshape(depth, D_MODEL, WIN_KV_HEADS, HEAD_DIM)
        return jnp.repeat(w, BR_HEADS // WIN_KV_HEADS, axis=2).reshape(depth, D_MODEL, BR_W)

    w1 = jnp.concatenate([w_in_b[:, :, A_Q:A_G], w_in_b[:, :, B_Q:B_G], w_in_b[:, :, C_Q:C_K],
                          expand_kv(w_in_b[:, :, C_K:C_V]), expand_kv(w_in_b[:, :, C_V:C_G]),
                          w_in_b[:, :, D_Q:D_G], w_in_b[:, :, E_Q:E_G]], axis=-1)
    wvt = w_in_b[:, :, B_Q + 2 * BR_W:B_G].transpose(0, 2, 1)
    wg = jnp.concatenate([w_in_b[:, :, A_G:B_Q], w_in_b[:, :, B_G:C_Q], w_in_b[:, :, C_G:D_Q],
                          w_in_b[:, :, D_G:E_Q], w_in_b[:, :, E_G:M_G]], axis=-1)
    wm = w_in_b[:, :, M_G:]
    wb = w_branch.astype(BF16)
    wo = w_out.astype(BF16)
    wmem = w_mem_kv.astype(BF16)

    sc_b = DF_DK ** -0.5 * LOG2E
    sc_h = HEAD_DIM ** -0.5 * LOG2E
    zrow = jnp.zeros((depth, 1, BR_W), F32)

    def per_layer(fn):
        return jnp.stack([fn(l) for l in range(depth)])

    ep = jnp.concatenate([
        per_layer(lambda l: _tile_gain(diff_qk_g[l, 0], 8, sc_b)),
        per_layer(lambda l: _tile_gain(diff_qk_g[l, 1], 8)),
        per_layer(lambda l: _tile_gain(win_qk_g[l, 0], 4, sc_h)),
        per_layer(lambda l: _tile_gain(win_qk_g[l, 1], 4)),
        per_layer(lambda l: _tile_gain(na_qk_g[l, 0], 4, sc_h)),
        per_layer(lambda l: _tile_gain(na_qk_g[l, 1], 4)),
        per_layer(lambda l: _tile_gain(mem_qk_g[l, 0], 4, sc_h)),
        lb_all[0][:, None, :], lb_all[1][:, None, :]] + [zrow] * 7, axis=1)
    gk_mem = per_layer(lambda l: _tile_gain(mem_qk_g[l, 1], 4))
    gn_hg = hgrn_norm_g.astype(F32)[:, None, :]
    gcol = per_layer(lambda l: (jnp.tile(diff_subln_g[l].astype(F32), BR_HEADS) * (1.0 - lam_init[l]))[:, None])
    sink = per_layer(lambda l: jnp.repeat(win_sink[l].astype(F32) * LOG2E, WIN_BLOCK)[:, None])
    na_bias = _na_bias_tables(na_rpb)

    diff_bias, diff_far = _diff_bias_tables(rel_bias)
    win_bias = _win_bias_table(rel_bias)
    sc = jnp.concatenate([jnp.broadcast_to(diff_far[None], (depth, 2, BR_HEADS)),
                          jnp.broadcast_to(lmb[:, None, None], (depth, 1, BR_HEADS))], axis=1)

    g32 = _group_matrix(DF_DK)
    g64 = _group_matrix(HEAD_DIM)
    consts_f = tuple(jnp.asarray(a, dt) for a, dt in zip(_hgrn_constants(HG_C, False), (BF16, F32, F32)))
    consts_b = tuple(jnp.asarray(a, dt) for a, dt in zip(_hgrn_constants(HG_C, True), (BF16, F32, F32)))

    def layer(x, p):
        ng = p["ng"]
        (a_q, a_v, a_lf, a_kk, b_q, b_k, b_v, c_q, c_k, c_v, d_q, d_k, d_v, e_q) = _project(
            x, ng, p["w1"], p["wvt"], p["ep"], g32, g64)
        mk, mv = _mem_kv(mem, p["mng"], p["wmem"], p["gk_mem"], g64)
        o_a = _hgrn(layout, a_q, a_v, a_kk, a_lf, consts_f, consts_b, g64, p["gn_hg"])
        o_b = _diff_attn(layout, b_q, b_k, b_v, diff_bias, p["sc"], p["gcol"])
        o_c = _win_attn(layout, c_q, c_k, c_v, win_bias, p["sink"])
        o_d = _na_attn(layout, d_q, d_k, d_v, p["na_bias"])
        o_e = _mem_attn(layout, e_q, mk.reshape(layout.nseq, mem_len, BR_W), mv.reshape(layout.nseq, mem_len, BR_W))
        y = _merge(x, ng, (o_a, o_b, o_c, o_d, o_e), p["wg"], p["wm"], p["wb"], p["wo"])
        return y, None

    params = dict(ng=norm_g.astype(F32)[:, None, :], mng=mem_norm_g.astype(F32)[:, None, :], w1=w1, wvt=wvt, ep=ep,
                  wmem=wmem, gk_mem=gk_mem, gn_hg=gn_hg, sc=sc, gcol=gcol, sink=sink, na_bias=na_bias,
                  wg=wg, wm=wm, wb=wb, wo=wo)
    x, _ = lax.scan(layer, x, params)
    y_prompt = x[:layout.off1].reshape(bp, lp, D_MODEL)
    y_sample = x[layout.off1:].reshape(bs, ls, D_MODEL)
    return (y_prompt, y_sample)
```

```python
import functools
import math

import numpy as np
import jax
import jax.numpy as jnp
from jax import lax
from jax.experimental import pallas as pl
from jax.experimental.pallas import tpu as pltpu

F32 = jnp.float32
BF16 = jnp.bfloat16

D_MODEL = 1024
HEAD_DIM = 64
BR_HEADS = 4
BR_W = BR_HEADS * HEAD_DIM
N_BRANCH = 5
DF_DK = HEAD_DIM // 2
WIN = 128
WIN_BLOCK = 128
WIN_KV_HEADS = 2
GRID_W = 64
NA_KH = 8
NA_KW = 16
N_BUCKETS = 32
MAX_DIST = 128
EPS = 1e-6
NEG = -1e30
LB_FLOOR = 1e-30
LOG2E = 1.4426950408889634

A_Q = 0
A_G = 4 * BR_W
B_Q = A_G + BR_W
B_G = B_Q + 3 * BR_W
C_Q = B_G + BR_W
C_K = C_Q + BR_W
C_V = C_K + WIN_KV_HEADS * HEAD_DIM
C_G = C_V + WIN_KV_HEADS * HEAD_DIM
D_Q = C_G + BR_W
D_G = D_Q + 3 * BR_W
E_Q = D_G + BR_W
E_G = E_Q + BR_W
M_G = E_G + BR_W

VMEM_LIMIT_BYTES = 56 * 1024 * 1024

PROJ_TM = 1024
MERGE_TM = 512
MEM_TM = 512
MEM_SPLIT = 4
HG_C = 128
HG_MAT_LEVELS = 2
DF_T = 256
NA_RB = 8
WIN_NB = 4
HG_NORM_TM = 2048
DF_VROWS = 80
DF_RC = 64
DF_KH = 128
DF_UNROLL = 4


def _params(sem):
    return pltpu.CompilerParams(dimension_semantics=sem, vmem_limit_bytes=VMEM_LIMIT_BYTES)


def _const_spec(shape):
    nd = len(shape)
    return pl.BlockSpec(shape, lambda *_: (0,) * nd, pipeline_mode=pl.Buffered(1))


def _dot(a, b):
    return jnp.dot(a, b, preferred_element_type=F32)


def _dot_nt(a, b):
    return lax.dot_general(a, b, (((1,), (1,)), ((), ())), preferred_element_type=F32)


def _rms_rows(x, g):
    ms = jnp.mean(x * x, axis=-1, keepdims=True)
    return x * lax.rsqrt(ms + EPS) * g


def _group_rms(x, gmat):
    x2 = x * x
    hi = x2.astype(BF16)
    lo = (x2 - hi.astype(F32)).astype(BF16)
    ms = _dot(hi, gmat) + _dot(lo, gmat)
    return x * lax.rsqrt(ms + EPS)


def _head_mask(width=BR_W):
    lane = lax.broadcasted_iota(jnp.int32, (1, width), 1)
    return [(lane >= h * HEAD_DIM) & (lane < (h + 1) * HEAD_DIM) for h in range(BR_HEADS)]


def _stack_heads(q, masks):
    zero = jnp.zeros_like(q)
    return jnp.concatenate([jnp.where(m, q, zero) for m in masks], axis=0)


def _unstack_heads(o_all, masks, m):
    out = jnp.zeros((m, BR_W), F32)
    for h, mk in enumerate(masks):
        out = out + jnp.where(mk, o_all[h * m:(h + 1) * m, :], 0.0)
    return out


class _Layout:
    def __init__(self, n_prompt, len_prompt, n_sample, len_sample):
        self.classes = ((0, n_prompt, len_prompt), (n_prompt * len_prompt, n_sample, len_sample))
        self.off1 = n_prompt * len_prompt
        self.lp = len_prompt
        self.ls = len_sample
        self.total = self.off1 + n_sample * len_sample
        self.nseq = n_prompt + n_sample
        self.n_prompt = n_prompt
        assert self.off1 % len_sample == 0

    def seq_bounds(self, tok):
        in_p = tok < self.off1
        start_p = (tok // self.lp) * self.lp
        start_s = self.off1 + ((tok - self.off1) // self.ls) * self.ls
        return jnp.where(in_p, start_p, start_s), jnp.where(in_p, self.lp, self.ls)

    def seq_index(self, tok):
        return jnp.where(tok < self.off1, tok // self.lp, self.n_prompt + (tok - self.off1) // self.ls)


def _proj_body(x_ref, ng_ref, w_ref, wvt_ref, ep_ref, g32_ref, g64_ref,
               aq_ref, av_ref, alf_ref, akk_ref, bq_ref, bk_ref, bv_ref,
               cq_ref, ck_ref, cv_ref, dq_ref, dk_ref, dv_ref, eq_ref):
    h = _rms_rows(x_ref[...], ng_ref[...]).astype(BF16)
    ep = ep_ref[...]

    def seg(i):
        return _dot(h, w_ref[:, i * BR_W:(i + 1) * BR_W])

    def row(r):
        return ep[r:r + 1, :]

    def silu_out(ref):
        def fin(a):
            ref[...] = (a * jax.nn.sigmoid(a)).astype(BF16)
        return fin

    def plain_out(ref):
        def fin(a):
            ref[...] = a.astype(BF16)
        return fin

    def normed_out(ref, g_ref, gain_row):
        def fin(a):
            ref[...] = (_group_rms(a, g_ref[...]) * row(gain_row)).astype(BF16)
        return fin

    def decay_out(d):
        def fin(z):
            lb = row(7 + d)
            e = jnp.exp(-jnp.abs(z))
            log_sig = jnp.minimum(z, 0.0) - jnp.log1p(e)
            t1 = jnp.log(jnp.maximum(lb, LB_FLOOR))
            t2 = jnp.log1p(-lb) + log_sig
            logf = jnp.maximum(t1, t2) + jnp.log1p(jnp.exp(-jnp.abs(t1 - t2)))
            sig_neg = jnp.where(z >= 0.0, e, 1.0) / (1.0 + e)
            alf_ref[:, d * BR_W:(d + 1) * BR_W] = logf
            akk_ref[:, d * BR_W:(d + 1) * BR_W] = ((1.0 - lb) * sig_neg).astype(BF16)
        return fin

    def vt_out(vt):
        vt = vt.astype(BF16)
        pad_row = lax.broadcasted_iota(jnp.int32, (DF_VROWS - HEAD_DIM, DF_T), 0)
        pad = jnp.where(pad_row == 0, 1.0, 0.0).astype(BF16)
        for j in range(PROJ_TM // DF_T):
            for hd in range(BR_HEADS):
                bv_ref[j, hd, 0:HEAD_DIM, :] = vt[hd * HEAD_DIM:(hd + 1) * HEAD_DIM, j * DF_T:(j + 1) * DF_T]
                bv_ref[j, hd, HEAD_DIM:DF_VROWS, :] = pad

    tasks = [
        (lambda: seg(0), silu_out(aq_ref)),
        (lambda: seg(1), plain_out(av_ref)),
        (lambda: seg(2), decay_out(0)),
        (lambda: seg(3), decay_out(1)),
        (lambda: seg(4), normed_out(bq_ref, g32_ref, 0)),
        (lambda: seg(5), normed_out(bk_ref, g32_ref, 1)),
        (lambda: _dot_nt(wvt_ref[...], h), vt_out),
        (lambda: seg(7), normed_out(cq_ref, g64_ref, 2)),
        (lambda: seg(8), normed_out(ck_ref, g64_ref, 3)),
        (lambda: seg(9), plain_out(cv_ref)),
        (lambda: seg(10), normed_out(dq_ref, g64_ref, 4)),
        (lambda: seg(11), normed_out(dk_ref, g64_ref, 5)),
        (lambda: seg(12), plain_out(dv_ref)),
        (lambda: seg(13), normed_out(eq_ref, g64_ref, 6)),
    ]
    cur = tasks[0][0]()
    for i, (_, fin) in enumerate(tasks):
        nxt = tasks[i + 1][0]() if i + 1 < len(tasks) else None
        fin(cur)
        cur = nxt


def _project(x, ng, w1, wvt, ep, g32, g64):
    t = x.shape[0]
    tm = PROJ_TM
    tile = lambda w: pl.BlockSpec((tm, w), lambda i: (i, 0))
    widths = [BR_W, BR_W, 2 * BR_W, 2 * BR_W] + [BR_W] * 10
    dtypes = [BF16, BF16, F32, BF16] + [BF16] * 10
    out_specs = [tile(w) for w in widths]
    out_shape = [jax.ShapeDtypeStruct((t, w), dt) for w, dt in zip(widths, dtypes)]
    kt = tm // DF_T
    out_specs[6] = pl.BlockSpec((kt, BR_HEADS, DF_VROWS, DF_T), lambda i: (i, 0, 0, 0))
    out_shape[6] = jax.ShapeDtypeStruct((t // DF_T, BR_HEADS, DF_VROWS, DF_T), BF16)
    return pl.pallas_call(
        _proj_body,
        grid=(t // tm,),
        in_specs=[tile(D_MODEL), _const_spec((1, D_MODEL)), _const_spec(w1.shape), _const_spec(wvt.shape),
                  _const_spec(ep.shape), _const_spec(g32.shape), _const_spec(g64.shape)],
        out_specs=out_specs,
        out_shape=out_shape,
        compiler_params=_params(("parallel",)),
        name="proj",
    )(x, ng, w1, wvt, ep, g32, g64)


def _memkv_body(m_ref, g_ref, w_ref, gk_ref, g64_ref, mk_ref, mv_ref):
    mh = _rms_rows(m_ref[...], g_ref[...]).astype(BF16)
    kv = _dot(mh, w_ref[...])
    mk_ref[...] = (_group_rms(kv[:, :BR_W], g64_ref[...]) * gk_ref[...]).astype(BF16)
    mv_ref[...] = kv[:, BR_W:].astype(BF16)


def _mem_kv(mem, g, w, gk, g64):
    t = mem.shape[0]
    tm = 256
    return pl.pallas_call(
        _memkv_body,
        grid=(t // tm,),
        in_specs=[pl.BlockSpec((tm, D_MODEL), lambda i: (i, 0)), _const_spec((1, D_MODEL)), _const_spec(w.shape),
                  _const_spec((1, BR_W)), _const_spec(g64.shape)],
        out_specs=[pl.BlockSpec((tm, BR_W), lambda i: (i, 0))] * 2,
        out_shape=[jax.ShapeDtypeStruct((t, BR_W), BF16)] * 2,
        compiler_params=_params(("parallel",)),
        name="memkv",
    )(mem, g, w, gk, g64)


def _hgrn_constants(c, reverse):
    nl = int(math.log2(c))
    idx = np.arange(c)
    t = idx[:, None]
    u = idx[None, :]
    incl = (u <= t).astype(np.float32)
    rest = (u > t).astype(np.float32)
    tot = np.ones((8, c), np.float32)
    mds, mes, lms = [], [], [np.eye(c, dtype=np.float32)]
    for lev in range(nl):
        w = 1 << lev
        blk = idx // w
        odd = (blk % 2 == 1)
        md = (odd[:, None] & (u >= (blk * w)[:, None]) & (u <= t)).astype(np.float32)
        me = ((~odd)[:, None] & (u > t) & (u <= ((blk + 1) * w - 1)[:, None])).astype(np.float32)
        lm = (odd[:, None] & (blk[None, :] == (blk - 1)[:, None])).astype(np.float32)
        mds.append(md)
        mes.append(me)
        lms.append(lm)
    del rest
    mats = [incl, tot] + [md + me for md, me in zip(mds[:HG_MAT_LEVELS], mes[:HG_MAT_LEVELS])]
    if reverse:
        mats = [m[::-1, ::-1] for m in mats]
        lms = [m[::-1, ::-1] for m in lms]
    mall = np.concatenate(mats, axis=0)
    lmst = np.stack([np.tile(m, (1, BR_HEADS)) for m in lms])
    bd = np.kron(np.eye(BR_HEADS, dtype=np.float32), np.ones((HEAD_DIM, HEAD_DIM), np.float32))
    return mall, lmst, bd


def _hgrn_chunk(layout, chunk, reverse, q_ref, k_ref, v_ref, lf_ref, mall_ref, lm_ref, bd_ref, o_ref, st_ref):
    c = HG_C
    nl = int(math.log2(c))
    tok = chunk * c
    start, length = layout.seq_bounds(tok)
    fresh = (tok + c == start + length) if reverse else (tok == start)

    masks = _head_mask()
    lf = lf_ref[...]
    hi = lf.astype(BF16)
    mid = (lf - hi.astype(F32)).astype(BF16)
    mall = mall_ref[...]
    cums = _dot(mall, hi) + _dot(mall, mid)
    b = cums[0:c, :]
    b_tot = cums[c:c + 1, :]
    b_rest = b_tot - b
    base = c + 8

    def level_decay(lev):
        if lev < HG_MAT_LEVELS:
            return cums[base + lev * c:base + (lev + 1) * c, :]
        w = 1 << lev
        b3 = b.reshape(c // (2 * w), 2 * w, BR_W)
        r = w if reverse else w - 1
        d3 = b3 - b3[:, r:r + 1, :]
        second = lax.broadcasted_iota(jnp.int32, (1, 2 * w, 1), 1) >= w
        query_half = jnp.logical_not(second) if reverse else second
        return jnp.where(query_half, d3, -d3).reshape(c, BR_W)

    qf = q_ref[...].astype(F32)
    kf = k_ref[...].astype(F32)
    v = v_ref[...]
    st = jnp.where(fresh, 0.0, st_ref[...])

    o = _dot_nt((qf * jnp.exp(b)).astype(BF16), st.astype(BF16))
    dim_row = lax.broadcasted_iota(jnp.int32, (BR_W, 1), 0)
    head_rows = [(dim_row >= h * HEAD_DIM) & (dim_row < (h + 1) * HEAD_DIM) for h in range(BR_HEADS)]

    def scores(ql, kl):
        kt = kl.T.astype(BF16)
        rhs = jnp.concatenate([jnp.where(hr, kt, jnp.zeros_like(kt)) for hr in head_rows], axis=1)
        return _dot(ql, rhs)

    a_all = scores(q_ref[...], kf) * lm_ref[0]
    for lev in range(nl):
        dec = jnp.exp(level_decay(lev))
        a_all = a_all + scores((qf * dec).astype(BF16), kf * dec) * lm_ref[lev + 1]
    a_bf = a_all.astype(BF16)
    zero = jnp.zeros_like(v)
    for h, mk in enumerate(masks):
        o = o + _dot(a_bf[:, h * c:(h + 1) * c], jnp.where(mk, v, zero))
    o_ref[...] = o

    kst = (kf * jnp.exp(b_rest)).astype(BF16)
    vt = v.astype(F32).T.astype(BF16)
    st_ref[...] = (st * jnp.exp(b_tot) + _dot(vt, kst)) * bd_ref[...]


def _hgrn_body(layout, nct, qf_ref, kf_ref, vf_ref, lff_ref, qb_ref, kb_ref, vb_ref, lfb_ref,
               mallf_ref, lmf_ref, mallb_ref, lmb_ref, bd_ref, of_ref, ob_ref, stf_ref, stb_ref):
    i = pl.program_id(0)
    _hgrn_chunk(layout, i, False, qf_ref, kf_ref, vf_ref, lff_ref, mallf_ref, lmf_ref, bd_ref, of_ref, stf_ref)
    _hgrn_chunk(layout, nct - 1 - i, True, qb_ref, kb_ref, vb_ref, lfb_ref, mallb_ref, lmb_ref, bd_ref,
                ob_ref, stb_ref)


def _hgrn_norm_body(of_ref, ob_ref, g64_ref, gn_ref, o_ref):
    o_ref[...] = (_group_rms(of_ref[...] + ob_ref[...], g64_ref[...]) * gn_ref[...]).astype(BF16)


def _hgrn(layout, qs, v, kk, logf, consts_f, consts_b, g64, gn):
    t = qs.shape[0]
    c = HG_C
    nct = t // c
    mall_f, lm_f, bd = consts_f
    mall_b, lm_b, _ = consts_b

    def specs(reverse):
        cm = (lambda i: nct - 1 - i) if reverse else (lambda i: i)
        d = 1 if reverse else 0
        tile = pl.BlockSpec((c, BR_W), lambda i: (cm(i), 0))
        half = pl.BlockSpec((c, BR_W), lambda i: (cm(i), d))
        return tile, half

    tile_f, half_f = specs(False)
    tile_b, half_b = specs(True)
    o_f, o_b = pl.pallas_call(
        functools.partial(_hgrn_body, layout, nct),
        grid=(nct,),
        in_specs=[tile_f, half_f, tile_f, half_f, tile_b, half_b, tile_b, half_b,
                  _const_spec(mall_f.shape), _const_spec(lm_f.shape), _const_spec(mall_b.shape),
                  _const_spec(lm_b.shape), _const_spec(bd.shape)],
        out_specs=[tile_f, tile_b],
        out_shape=[jax.ShapeDtypeStruct((t, BR_W), F32)] * 2,
        scratch_shapes=[pltpu.VMEM((BR_W, BR_W), F32)] * 2,
        compiler_params=_params(("arbitrary",)),
        name="hgrn_scan",
    )(qs, kk, v, logf, qs, kk, v, logf, mall_f, lm_f, mall_b, lm_b, bd)

    tm = math.gcd(t, HG_NORM_TM)
    tok = pl.BlockSpec((tm, BR_W), lambda i: (i, 0))
    return pl.pallas_call(
        _hgrn_norm_body,
        grid=(t // tm,),
        in_specs=[tok, tok, _const_spec(g64.shape), _const_spec((1, BR_W))],
        out_specs=tok,
        out_shape=jax.ShapeDtypeStruct((t, BR_W), BF16),
        compiler_params=_params(("parallel",)),
        name="hgrn_norm",
    )(o_f, o_b, g64, gn)


def _diff_body(nkt, q_ref, k_ref, vt_ref, bias_ref, sc_ref, gcol_ref, o_ref, qm_ref, m_ref, acc_ref,
               s0_ref, s1_ref, mx0_ref, mx1_ref, p_ref):
    tq = DF_T
    qi = pl.program_id(1)
    qt = q_ref[...].astype(F32).T.astype(BF16)
    rowid = lax.broadcasted_iota(jnp.int32, (DF_KH, tq), 0)
    for hc in range(2 * BR_HEADS):
        half, sub = divmod(hc * DF_DK, DF_KH)
        sel = (rowid >= sub) & (rowid < sub + DF_DK)
        qh = qt[half * DF_KH:(half + 1) * DF_KH, :]
        qm_ref[hc] = jnp.where(sel, qh, jnp.zeros_like(qh))
    m_ref[...] = jnp.full(m_ref.shape, NEG, F32)
    acc_ref[...] = jnp.zeros_like(acc_ref)

    nu = 2 * BR_HEADS
    rows = [(r, r + DF_RC) for r in range(0, tq, DF_RC)]

    def qk_unit(kj, hc, buf, near=None):
        sbuf, mxbuf = buf
        half = hc * DF_DK // DF_KH
        s = _dot(k_ref[kj, :, half * DF_KH:(half + 1) * DF_KH], qm_ref[hc])
        if near is not None:
            s = s + bias_ref[near, hc // 2]
        sbuf[hc] = s
        mxbuf[hc] = jnp.max(s.reshape(tq // 8, 8, tq), axis=0)

    def softmax_pv_unit(kj, hc, buf, const_bias):
        sbuf, mxbuf = buf
        c = const_bias(hc // 2)
        m_old = m_ref[hc]
        m_new = jnp.maximum(m_old, jnp.max(mxbuf[hc], axis=0, keepdims=True) + c)
        shift = m_new - c
        p = jnp.concatenate([jnp.exp2(sbuf[hc, r0:r1, :] - shift).astype(BF16) for r0, r1 in rows], axis=0)
        m_ref[hc] = m_new
        acc_ref[hc] = acc_ref[hc] * jnp.exp2(m_old - m_new) + _dot(vt_ref[kj, hc // 2], p)

    def step(kj, buf, const_bias, nxt):
        for hc in range(nu):
            qk_unit(nxt[0], hc, nxt[1], nxt[2])
            softmax_pv_unit(kj, hc, buf, const_bias)

    last = nkt - 1
    n_low = jnp.maximum(qi - 1, 0)
    n_far = n_low + jnp.maximum(nkt - qi - 2, 0)
    buf_a = (s0_ref, mx0_ref)
    buf_b = (s1_ref, mx1_ref)

    def far_tile(f):
        return jnp.minimum(jnp.where(f < n_low, f, f - n_low + qi + 2), last)

    def dead_if(cond):
        pen = jnp.where(cond, 2.0 * NEG, 0.0)
        return lambda h: pen

    near = [jnp.clip(qi + d - 1, 0, last) for d in range(3)]
    for hc in range(nu):
        qk_unit(near[0], hc, buf_a, 0)
    step(near[0], buf_a, dead_if(qi == 0), (near[1], buf_b, 1))
    step(near[1], buf_b, dead_if(False), (near[2], buf_a, 2))
    step(near[2], buf_a, dead_if(qi == last), (far_tile(0), buf_b, None))

    def far_step(f, cur, nxt):
        kj = far_tile(f)
        row = jnp.where(kj < qi, 0, 1)
        live = f < n_far
        step(kj, cur, lambda h: jnp.where(live, sc_ref[row, h], 2.0 * NEG), (far_tile(f + 1), nxt, None))

    far_step(0, buf_b, buf_a)

    def body(i, carry):
        f0 = 1 + DF_UNROLL * i
        for u in range(DF_UNROLL):
            cur, nxt = (buf_a, buf_b) if u % 2 == 0 else (buf_b, buf_a)
            far_step(f0 + u, cur, nxt)
        return carry
    lax.fori_loop(0, (n_far - 1 + DF_UNROLL - 1) // DF_UNROLL, body, 0)

    lmb = sc_ref[2, 0]
    outs = []
    for h in range(BR_HEADS):
        a0 = acc_ref[2 * h]
        a1 = acc_ref[2 * h + 1]
        o0 = a0[:HEAD_DIM, :] / a0[HEAD_DIM:HEAD_DIM + 1, :]
        o1 = a1[:HEAD_DIM, :] / a1[HEAD_DIM:HEAD_DIM + 1, :]
        o = o0 - lmb * o1
        ms = jnp.mean(o * o, axis=0, keepdims=True)
        outs.append(o * lax.rsqrt(ms + EPS))
    ot = jnp.concatenate(outs, axis=0) * gcol_ref[...]
    o_ref[...] = ot.T.astype(BF16)


def _diff_attn_class(off, nseq, n, qt, k3, vt4, bias, sc, gcol):
    t = DF_T
    nkt = n // t
    qb = off // t
    sb = off // n
    return pl.pallas_call(
        functools.partial(_diff_body, nkt),
        grid=(nseq, nkt),
        in_specs=[pl.BlockSpec((t, BR_W), lambda s, i: (qb + s * nkt + i, 0)),
                  pl.BlockSpec((nkt, t, BR_W), lambda s, i: (sb + s, 0, 0), pipeline_mode=pl.Buffered(1)),
                  pl.BlockSpec((nkt, BR_HEADS, DF_VROWS, t), lambda s, i: (sb + s, 0, 0, 0),
                               pipeline_mode=pl.Buffered(1)),
                  _const_spec(bias.shape),
                  pl.BlockSpec(memory_space=pltpu.SMEM),
                  _const_spec(gcol.shape)],
        out_specs=pl.BlockSpec((t, BR_W), lambda s, i: (s * nkt + i, 0)),
        out_shape=jax.ShapeDtypeStruct((nseq * n, BR_W), BF16),
        scratch_shapes=[pltpu.VMEM((2 * BR_HEADS, DF_KH, t), BF16),
                        pltpu.VMEM((2 * BR_HEADS, 1, t), F32),
                        pltpu.VMEM((2 * BR_HEADS, DF_VROWS, t), F32),
                        pltpu.VMEM((2 * BR_HEADS, t, t), F32),
                        pltpu.VMEM((2 * BR_HEADS, t, t), F32),
                        pltpu.VMEM((2 * BR_HEADS, 8, t), F32),
                        pltpu.VMEM((2 * BR_HEADS, 8, t), F32),
                        pltpu.VMEM((2 * BR_HEADS, t, t), BF16)],
        compiler_params=_params(("parallel", "parallel")),
        name="diff_attn",
    )(qt, k3, vt4, bias, sc, gcol)


def _diff_attn(layout, q, k, vt4, bias, sc, gcol):
    t = k.shape[0]
    k3 = k.reshape(t // DF_T, DF_T, BR_W)
    outs = [_diff_attn_class(off, nseq, n, q, k3, vt4, bias, sc, gcol) for off, nseq, n in layout.classes]
    return jnp.concatenate(outs, axis=0)


def _win_body(layout, q_ref, kp_ref, kc_ref, kn_ref, vp_ref, vc_ref, vn_ref, bias_ref, sink_ref, o_ref):
    blk = WIN_BLOCK
    i = pl.program_id(0)
    tok = i * (WIN_NB * blk)
    start, length = layout.seq_bounds(tok)
    masks = _head_mask()
    kcat = jnp.concatenate([kp_ref[...], kc_ref[...], kn_ref[...]], axis=0)
    vcat = jnp.concatenate([vp_ref[...], vc_ref[...], vn_ref[...]], axis=0)
    col = lax.broadcasted_iota(jnp.int32, (1, 3 * blk), 1)
    sink = sink_ref[...]
    bias = bias_ref[...]
    subs = range(WIN_NB)

    def dead_cols(j):
        has_prev = tok + j * blk > start
        has_next = tok + (j + 1) * blk < start + length
        return ((col < blk) & jnp.logical_not(has_prev)) | ((col >= 2 * blk) & jnp.logical_not(has_next))

    s = [_dot_nt(_stack_heads(q_ref[j * blk:(j + 1) * blk, :], masks), kcat[j * blk:(j + 3) * blk, :]) for j in subs]
    s = [s[j] + bias + jnp.where(dead_cols(j), NEG, 0.0) for j in subs]
    m = [jnp.maximum(jnp.max(s[j], axis=-1, keepdims=True), sink) for j in subs]
    p = [jnp.exp2(s[j] - m[j]) for j in subs]
    den = [jnp.sum(p[j], axis=-1, keepdims=True) + jnp.exp2(sink - m[j]) for j in subs]
    pn = [(p[j] * (1.0 / den[j])).astype(BF16) for j in subs]
    o_all = [_dot(pn[j], vcat[j * blk:(j + 3) * blk, :]) for j in subs]
    for j in subs:
        o_ref[j * blk:(j + 1) * blk, :] = _unstack_heads(o_all[j], masks, blk).astype(BF16)


def _win_attn(layout, q, k, v, bias, sink):
    t = q.shape[0]
    blk = WIN_BLOCK
    nb = t // blk
    cur = pl.BlockSpec((WIN_NB * blk, BR_W), lambda i: (i, 0))
    prev = pl.BlockSpec((blk, BR_W), lambda i: (jnp.maximum(i * WIN_NB - 1, 0), 0))
    nxt = pl.BlockSpec((blk, BR_W), lambda i: (jnp.minimum((i + 1) * WIN_NB, nb - 1), 0))
    return pl.pallas_call(
        functools.partial(_win_body, layout),
        grid=(nb // WIN_NB,),
        in_specs=[cur, prev, cur, nxt, prev, cur, nxt, _const_spec(bias.shape), _const_spec(sink.shape)],
        out_specs=cur,
        out_shape=jax.ShapeDtypeStruct((t, BR_W), BF16),
        compiler_params=_params(("parallel",)),
        name="win_attn",
    )(q, k, k, k, v, v, v, bias, sink)


def _na_body(rows, q_ref, k_ref, v_ref, bias_ref, o_ref):
    j = pl.program_id(1)
    masks = _head_mask()
    nk = NA_KH * GRID_W
    rr = range(NA_RB)
    r = [j * NA_RB + i for i in rr]
    rs = [jnp.clip(r[i] - NA_KH // 2, 0, rows - NA_KH) for i in rr]
    koff = [pl.multiple_of(rs[i] * GRID_W, GRID_W) for i in rr]
    s = [_dot_nt(_stack_heads(q_ref[i * GRID_W:(i + 1) * GRID_W, :], masks), k_ref[pl.ds(koff[i], nk), :])
         + bias_ref[r[i] - rs[i]] for i in rr]
    m = [jnp.max(s[i], axis=-1, keepdims=True) for i in rr]
    p = [jnp.exp2(s[i] - m[i]) for i in rr]
    pn = [(p[i] * (1.0 / jnp.sum(p[i], axis=-1, keepdims=True))).astype(BF16) for i in rr]
    o_all = [_dot(pn[i], v_ref[pl.ds(koff[i], nk), :]) for i in rr]
    for i in rr:
        o_ref[i * GRID_W:(i + 1) * GRID_W, :] = _unstack_heads(o_all[i], masks, GRID_W).astype(BF16)


def _na_class(off, nseq, n, q, k, v, bias):
    rows = n // GRID_W
    qt = NA_RB * GRID_W
    nj = n // qt
    qb = off // qt
    sb = off // n
    seq = pl.BlockSpec((n, BR_W), lambda s, j: (sb + s, 0), pipeline_mode=pl.Buffered(1))
    return pl.pallas_call(
        functools.partial(_na_body, rows),
        grid=(nseq, nj),
        in_specs=[pl.BlockSpec((qt, BR_W), lambda s, j: (qb + s * nj + j, 0)), seq, seq, _const_spec(bias.shape)],
        out_specs=pl.BlockSpec((qt, BR_W), lambda s, j: (s * nj + j, 0)),
        out_shape=jax.ShapeDtypeStruct((nseq * n, BR_W), BF16),
        compiler_params=_params(("parallel", "parallel")),
        name="na_attn",
    )(q, k, v, bias)


def _na_attn(layout, q, k, v, bias):
    return jnp.concatenate([_na_class(off, nseq, n, q, k, v, bias) for off, nseq, n in layout.classes], axis=0)


def _mem_body(q_ref, mk_ref, mv_ref, o_ref):
    masks = _head_mask()
    sub = MEM_TM // MEM_SPLIT
    parts = range(MEM_SPLIT)
    mk = mk_ref[0]
    mv = mv_ref[0]
    s = [_dot_nt(_stack_heads(q_ref[i * sub:(i + 1) * sub, :], masks), mk) for i in parts]
    m = [jnp.max(s[i], axis=-1, keepdims=True) for i in parts]
    p = [jnp.exp2(s[i] - m[i]) for i in parts]
    pn = [(p[i] * (1.0 / jnp.sum(p[i], axis=-1, keepdims=True))).astype(BF16) for i in parts]
    o_all = [_dot(pn[i], mv) for i in parts]
    for i in parts:
        o_ref[i * sub:(i + 1) * sub, :] = _unstack_heads(o_all[i], masks, sub).astype(BF16)


def _mem_attn(layout, q, mk, mv):
    t = q.shape[0]
    tm = MEM_TM
    mem_len = mk.shape[1]
    tile = pl.BlockSpec((tm, BR_W), lambda i: (i, 0))
    mem = pl.BlockSpec((1, mem_len, BR_W), lambda i: (layout.seq_index(i * tm), 0, 0))
    return pl.pallas_call(
        _mem_body,
        grid=(t // tm,),
        in_specs=[tile, mem, mem],
        out_specs=tile,
        out_shape=jax.ShapeDtypeStruct((t, BR_W), BF16),
        compiler_params=_params(("parallel",)),
        name="mem_attn",
    )(q, mk, mv)


def _merge_body(x_ref, ng_ref, oa_ref, ob_ref, oc_ref, od_ref, oe_ref, wg_ref, wm_ref, wb_ref, wo_ref, y_ref):
    x = x_ref[...]
    h = _rms_rows(x, ng_ref[...]).astype(BF16)
    merged = jnp.zeros((x.shape[0], D_MODEL), F32)
    for kb, o_ref in enumerate((oa_ref, ob_ref, oc_ref, od_ref, oe_ref)):
        g = _dot(h, wg_ref[:, kb * BR_W:(kb + 1) * BR_W])
        br = (o_ref[...].astype(F32) * (g * jax.nn.sigmoid(g))).astype(BF16)
        mg = jax.nn.sigmoid(_dot(h, wm_ref[:, kb * D_MODEL:(kb + 1) * D_MODEL]))
        merged = merged + mg * _dot(br, wb_ref[kb])
    y_ref[...] = x + _dot(merged.astype(BF16), wo_ref[...])


def _merge(x, ng, branches, wg, wm, wb, wo):
    t = x.shape[0]
    tm = MERGE_TM
    xt = pl.BlockSpec((tm, D_MODEL), lambda i: (i, 0))
    bt = pl.BlockSpec((tm, BR_W), lambda i: (i, 0))
    return pl.pallas_call(
        _merge_body,
        grid=(t // tm,),
        in_specs=[xt, _const_spec((1, D_MODEL))] + [bt] * N_BRANCH
        + [_const_spec(wg.shape), _const_spec(wm.shape), _const_spec(wb.shape), _const_spec(wo.shape)],
        out_specs=xt,
        out_shape=jax.ShapeDtypeStruct((t, D_MODEL), F32),
        input_output_aliases={0: 0},
        compiler_params=_params(("parallel",)),
        name="merge",
    )(x, ng, *branches, wg, wm, wb, wo)


def _t5_bucket(rel):
    half = N_BUCKETS // 2
    exact = half // 2
    n = jnp.abs(rel)
    nf = jnp.maximum(n, 1).astype(F32)
    large = exact + (jnp.log(nf / exact) / math.log(MAX_DIST / exact) * (half - exact)).astype(jnp.int32)
    large = jnp.clip(large, 0, half - 1)
    return jnp.where(rel > 0, half, 0) + jnp.where(n < exact, n, large)


def _group_matrix(group):
    return jnp.asarray(np.kron(np.eye(BR_W // group), np.full((group, group), 1.0 / group)), BF16)


def _lookup(table, idx):
    onehot = (idx[..., None] == jnp.arange(table.shape[0])).astype(F32)
    return jnp.dot(onehot, table, precision=lax.Precision.HIGHEST)


def _diff_bias_tables(rel_bias):
    t = DF_T
    table = rel_bias[:, :BR_HEADS].astype(F32) * LOG2E
    kl = jnp.arange(t)[:, None]
    ql = jnp.arange(t)[None, :]
    rel = jnp.stack([kl - ql + d * t for d in (-1, 0, 1)])
    tiles = _lookup(table, _t5_bucket(rel)).transpose(0, 3, 1, 2)
    far = _lookup(table, _t5_bucket(jnp.asarray([-2 * t, 2 * t], jnp.int32)))
    return tiles, far


def _win_bias_table(rel_bias):
    rel = jnp.arange(3 * WIN_BLOCK)[None, :] - WIN_BLOCK - jnp.arange(WIN_BLOCK)[:, None]
    bias = _lookup(rel_bias[:, BR_HEADS:].astype(F32) * LOG2E, _t5_bucket(rel)).transpose(2, 0, 1)
    bias = jnp.where((jnp.abs(rel) <= WIN)[None], bias, NEG)
    return bias.reshape(BR_HEADS * WIN_BLOCK, 3 * WIN_BLOCK)


def _na_bias_tables(rpb):
    depth = rpb.shape[0]
    col = np.arange(GRID_W)
    cs = np.clip(col - NA_KW // 2, 0, GRID_W - NA_KW)
    inwin = (col[None, :] >= cs[:, None]) & (col[None, :] < cs[:, None] + NA_KW)
    dc = col[None, :] - col[:, None] + (NA_KW - 1)
    onehot = jnp.asarray(dc[None] == np.arange(2 * NA_KW - 1)[:, None, None], F32)
    toep = jnp.einsum("lhrc,cqk->lhrqk", rpb.astype(F32) * LOG2E, onehot, precision=lax.Precision.HIGHEST)
    toep = jnp.where(jnp.asarray(inwin)[None, None, None], toep, NEG)
    out = []
    for d in range(NA_KH):
        lo = NA_KH - 1 - d
        b = toep[:, :, lo:lo + NA_KH].transpose(0, 1, 3, 2, 4)
        out.append(b.reshape(depth, BR_HEADS * GRID_W, NA_KH * GRID_W))
    return jnp.stack(out, axis=1)


def _tile_gain(g, reps, scale=1.0):
    return jnp.tile(g.astype(F32), reps)[None, :] * scale


def kernel(x_prompt, x_sample, mem_prompt, mem_sample, norm_g, mem_norm_g, w_in, w_mem_kv, rel_bias, hgrn_lb,
           hgrn_norm_g, diff_qk_g, diff_lambda, diff_subln_g, win_qk_g, win_sink, na_qk_g, na_rpb, mem_qk_g,
           w_branch, w_out):
    depth = w_in.shape[0]
    bp, lp, _ = x_prompt.shape
    bs, ls, _ = x_sample.shape
    mem_len = mem_prompt.shape[1]
    layout = _Layout(bp, lp, bs, ls)
    t = layout.total
    assert t % PROJ_TM == 0 and t % MERGE_TM == 0 and lp % DF_T == 0 and ls % DF_T == 0
    assert lp % (NA_RB * GRID_W) == 0 and ls % (NA_RB * GRID_W) == 0 and lp % MEM_TM == 0 and ls % MEM_TM == 0
    assert lp % (WIN_NB * WIN_BLOCK) == 0 and ls % (WIN_NB * WIN_BLOCK) == 0

    x = jnp.concatenate([x_prompt.reshape(bp * lp, D_MODEL), x_sample.reshape(bs * ls, D_MODEL)], axis=0)
    mem = jnp.concatenate([mem_prompt.reshape(bp * mem_len, D_MODEL), mem_sample.reshape(bs * mem_len, D_MODEL)], axis=0)

    sm = jax.nn.softmax(hgrn_lb.astype(F32), axis=1)
    lb_all = jnp.clip(jnp.cumsum(sm, axis=1) - sm[:, :1], 0.0, 1.0 - 1e-6)
    lam_init = jnp.asarray([0.8 - 0.6 * math.exp(-0.3 * l) for l in range(depth)], F32)
    lam = diff_lambda.astype(F32)
    lmb = jnp.exp(jnp.sum(lam[:, 0] * lam[:, 1], axis=-1)) - jnp.exp(jnp.sum(lam[:, 2] * lam[:, 3], axis=-1)) + lam_init

    w_in_b = w_in.astype(BF16)

    def expand_kv(w):
        w = w.reshape(depth, D_MODEL, WIN_KV_HEADS, HEAD_DIM)
        return jnp.repeat(w, BR_HEADS // WIN_KV_HEADS, axis=2).reshape(depth, D_MODEL, BR_W)

    w1 = jnp.concatenate([w_in_b[:, :, A_Q:A_G], w_in_b[:, :, B_Q:B_G], w_in_b[:, :, C_Q:C_K],
                          expand_kv(w_in_b[:, :, C_K:C_V]), expand_kv(w_in_b[:, :, C_V:C_G]),
                          w_in_b[:, :, D_Q:D_G], w_in_b[:, :, E_Q:E_G]], axis=-1)
    wvt = w_in_b[:, :, B_Q + 2 * BR_W:B_G].transpose(0, 2, 1)
    wg = jnp.concatenate([w_in_b[:, :, A_G:B_Q], w_in_b[:, :, B_G:C_Q], w_in_b[:, :, C_G:D_Q],
                          w_in_b[:, :, D_G:E_Q], w_in_b[:, :, E_G:M_G]], axis=-1)
    wm = w_in_b[:, :, M_G:]
    wb = w_branch.astype(BF16)
    wo = w_out.astype(BF16)
    wmem = w_mem_kv.astype(BF16)

    sc_b = DF_DK ** -0.5 * LOG2E
    sc_h = HEAD_DIM ** -0.5 * LOG2E
    zrow = jnp.zeros((depth, 1, BR_W), F32)

    def per_layer(fn):
        return jnp.stack([fn(l) for l in range(depth)])

    ep = jnp.concatenate([
        per_layer(lambda l: _tile_gain(diff_qk_g[l, 0], 8, sc_b)),
        per_layer(lambda l: _tile_gain(diff_qk_g[l, 1], 8)),
        per_layer(lambda l: _tile_gain(win_qk_g[l, 0], 4, sc_h)),
        per_layer(lambda l: _tile_gain(win_qk_g[l, 1], 4)),
        per_layer(lambda l: _tile_gain(na_qk_g[l, 0], 4, sc_h)),
        per_layer(lambda l: _tile_gain(na_qk_g[l, 1], 4)),
        per_layer(lambda l: _tile_gain(mem_qk_g[l, 0], 4, sc_h)),
        lb_all[0][:, None, :], lb_all[1][:, None, :]] + [zrow] * 7, axis=1)
    gk_mem = per_layer(lambda l: _tile_gain(mem_qk_g[l, 1], 4))
    gn_hg = hgrn_norm_g.astype(F32)[:, None, :]
    gcol = per_layer(lambda l: (jnp.tile(diff_subln_g[l].astype(F32), BR_HEADS) * (1.0 - lam_init[l]))[:, None])
    sink = per_layer(lambda l: jnp.repeat(win_sink[l].astype(F32) * LOG2E, WIN_BLOCK)[:, None])
    na_bias = _na_bias_tables(na_rpb)

    diff_bias, diff_far = _diff_bias_tables(rel_bias)
    win_bias = _win_bias_table(rel_bias)
    sc = jnp.concatenate([jnp.broadcast_to(diff_far[None], (depth, 2, BR_HEADS)),
                          jnp.broadcast_to(lmb[:, None, None], (depth, 1, BR_HEADS))], axis=1)

    g32 = _group_matrix(DF_DK)
    g64 = _group_matrix(HEAD_DIM)
    consts_f = tuple(jnp.asarray(a, dt) for a, dt in zip(_hgrn_constants(HG_C, False), (BF16, F32, F32)))
    consts_b = tuple(jnp.asarray(a, dt) for a, dt in zip(_hgrn_constants(HG_C, True), (BF16, F32, F32)))

    def layer(x, p):
        ng = p["ng"]
        (a_q, a_v, a_lf, a_kk, b_q, b_k, b_v, c_q, c_k, c_v, d_q, d_k, d_v, e_q) = _project(
            x, ng, p["w1"], p["wvt"], p["ep"], g32, g64)
        mk, mv = _mem_kv(mem, p["mng"], p["wmem"], p["gk_mem"], g64)
        o_a = _hgrn(layout, a_q, a_v, a_kk, a_lf, consts_f, consts_b, g64, p["gn_hg"])
        o_b = _diff_attn(layout, b_q, b_k, b_v, diff_bias, p["sc"], p["gcol"])
        o_c = _win_attn(layout, c_q, c_k, c_v, win_bias, p["sink"])
        o_d = _na_attn(layout, d_q, d_k, d_v, p["na_bias"])
        o_e = _mem_attn(layout, e_q, mk.reshape(layout.nseq, mem_len, BR_W), mv.reshape(layout.nseq, mem_len, BR_W))
        y = _merge(x, ng, (o_a, o_b, o_c, o_d, o_e), p["wg"], p["wm"], p["wb"], p["wo"])
        return y, None

    params = dict(ng=norm_g.astype(F32)[:, None, :], mng=mem_norm_g.astype(F32)[:, None, :], w1=w1, wvt=wvt, ep=ep,
                  wmem=wmem, gk_mem=gk_mem, gn_hg=gn_hg, sc=sc, gcol=gcol, sink=sink, na_bias=na_bias,
                  wg=wg, wm=wm, wb=wb, wo=wo)
    x, _ = lax.scan(layer, x, params)
    y_prompt = x[:layout.off1].reshape(bp, lp, D_MODEL)
    y_sample = x[layout.off1:].reshape(bs, ls, D_MODEL)
    return (y_prompt, y_sample)
```

```python
import functools
import math

import numpy as np
import jax
import jax.numpy as jnp
from jax import lax
from jax.experimental import pallas as pl
from jax.experimental.pallas import tpu as pltpu

F32 = jnp.float32
BF16 = jnp.bfloat16

D_MODEL = 1024
HEAD_DIM = 64
BR_HEADS = 4
BR_W = BR_HEADS * HEAD_DIM
N_BRANCH = 5
DF_DK = HEAD_DIM // 2
WIN = 128
WIN_BLOCK = 128
WIN_KV_HEADS = 2
GRID_W = 64
NA_KH = 8
NA_KW = 16
N_BUCKETS = 32
MAX_DIST = 128
EPS = 1e-6
NEG = -1e30
LB_FLOOR = 1e-30
LOG2E = 1.4426950408889634

A_Q = 0
A_G = 4 * BR_W
B_Q = A_G + BR_W
B_G = B_Q + 3 * BR_W
C_Q = B_G + BR_W
C_K = C_Q + BR_W
C_V = C_K + WIN_KV_HEADS * HEAD_DIM
C_G = C_V + WIN_KV_HEADS * HEAD_DIM
D_Q = C_G + BR_W
D_G = D_Q + 3 * BR_W
E_Q = D_G + BR_W
E_G = E_Q + BR_W
M_G = E_G + BR_W

VMEM_LIMIT_BYTES = 56 * 1024 * 1024

PROJ_TM = 1024
MERGE_TM = 512
MEM_TM = 512
MEM_SPLIT = 4
HG_C = 128
HG_MAT_LEVELS = 2
DF_T = 256
NA_RB = 8
WIN_NB = 4
HG_NORM_TM = 2048
DF_VROWS = 80
DF_RC = 64
DF_KH = 128
DF_UNROLL = 4


def _params(sem):
    return pltpu.CompilerParams(dimension_semantics=sem, vmem_limit_bytes=VMEM_LIMIT_BYTES)


def _const_spec(shape):
    nd = len(shape)
    return pl.BlockSpec(shape, lambda *_: (0,) * nd, pipeline_mode=pl.Buffered(1))


def _dot(a, b):
    return jnp.dot(a, b, preferred_element_type=F32)


def _dot_nt(a, b):
    return lax.dot_general(a, b, (((1,), (1,)), ((), ())), preferred_element_type=F32)


def _rms_rows(x, g):
    ms = jnp.mean(x * x, axis=-1, keepdims=True)
    return x * lax.rsqrt(ms + EPS) * g


def _group_rms(x, gmat):
    x2 = x * x
    hi = x2.astype(BF16)
    lo = (x2 - hi.astype(F32)).astype(BF16)
    ms = _dot(hi, gmat) + _dot(lo, gmat)
    return x * lax.rsqrt(ms + EPS)


def _head_mask(width=BR_W):
    lane = lax.broadcasted_iota(jnp.int32, (1, width), 1)
    return [(lane >= h * HEAD_DIM) & (lane < (h + 1) * HEAD_DIM) for h in range(BR_HEADS)]


def _stack_heads(q, masks):
    zero = jnp.zeros_like(q)
    return jnp.concatenate([jnp.where(m, q, zero) for m in masks], axis=0)


def _unstack_heads(o_all, masks, m):
    out = jnp.zeros((m, BR_W), F32)
    for h, mk in enumerate(masks):
        out = out + jnp.where(mk, o_all[h * m:(h + 1) * m, :], 0.0)
    return out


class _Layout:
    def __init__(self, n_prompt, len_prompt, n_sample, len_sample):
        self.classes = ((0, n_prompt, len_prompt), (n_prompt * len_prompt, n_sample, len_sample))
        self.off1 = n_prompt * len_prompt
        self.lp = len_prompt
        self.ls = len_sample
        self.total = self.off1 + n_sample * len_sample
        self.nseq = n_prompt + n_sample
        self.n_prompt = n_prompt
        assert self.off1 % len_sample == 0

    def seq_bounds(self, tok):
        in_p = tok < self.off1
        start_p = (tok // self.lp) * self.lp
        start_s = self.off1 + ((tok - self.off1) // self.ls) * self.ls
        return jnp.where(in_p, start_p, start_s), jnp.where(in_p, self.lp, self.ls)

    def seq_index(self, tok):
        return jnp.where(tok < self.off1, tok // self.lp, self.n_prompt + (tok - self.off1) // self.ls)


def _proj_body(x_ref, ng_ref, w_ref, wvt_ref, ep_ref, g32_ref, g64_ref,
               aq_ref, av_ref, alf_ref, akk_ref, bq_ref, bk_ref, bv_ref,
               cq_ref, ck_ref, cv_ref, dq_ref, dk_ref, dv_ref, eq_ref):
    h = _rms_rows(x_ref[...], ng_ref[...]).astype(BF16)
    ep = ep_ref[...]

    def seg(i):
        return _dot(h, w_ref[:, i * BR_W:(i + 1) * BR_W])

    def row(r):
        return ep[r:r + 1, :]

    def silu_out(ref):
        def fin(a):
            ref[...] = (a * jax.nn.sigmoid(a)).astype(BF16)
        return fin

    def plain_out(ref):
        def fin(a):
            ref[...] = a.astype(BF16)
        return fin

    def normed_out(ref, g_ref, gain_row):
        def fin(a):
            ref[...] = (_group_rms(a, g_ref[...]) * row(gain_row)).astype(BF16)
        return fin

    def decay_out(d):
        def fin(z):
            lb = row(7 + d)
            e = jnp.exp(-jnp.abs(z))
            log_sig = jnp.minimum(z, 0.0) - jnp.log1p(e)
            t1 = jnp.log(jnp.maximum(lb, LB_FLOOR))
            t2 = jnp.log1p(-lb) + log_sig
            logf = jnp.maximum(t1, t2) + jnp.log1p(jnp.exp(-jnp.abs(t1 - t2)))
            sig_neg = jnp.where(z >= 0.0, e, 1.0) / (1.0 + e)
            alf_ref[:, d * BR_W:(d + 1) * BR_W] = logf
            akk_ref[:, d * BR_W:(d + 1) * BR_W] = ((1.0 - lb) * sig_neg).astype(BF16)
        return fin

    def vt_out(vt):
        vt = vt.astype(BF16)
        pad_row = lax.broadcasted_iota(jnp.int32, (DF_VROWS - HEAD_DIM, DF_T), 0)
        pad = jnp.where(pad_row == 0, 1.0, 0.0).astype(BF16)
        for j in range(PROJ_TM // DF_T):
            for hd in range(BR_HEADS):
                bv_ref[j, hd, 0:HEAD_DIM, :] = vt[hd * HEAD_DIM:(hd + 1) * HEAD_DIM, j * DF_T:(j + 1) * DF_T]
                bv_ref[j, hd, HEAD_DIM:DF_VROWS, :] = pad

    tasks = [
        (lambda: seg(0), silu_out(aq_ref)),
        (lambda: seg(1), plain_out(av_ref)),
        (lambda: seg(2), decay_out(0)),
        (lambda: seg(3), decay_out(1)),
        (lambda: seg(4), normed_out(bq_ref, g32_ref, 0)),
        (lambda: seg(5), normed_out(bk_ref, g32_ref, 1)),
        (lambda: _dot_nt(wvt_ref[...], h), vt_out),
        (lambda: seg(7), normed_out(cq_ref, g64_ref, 2)),
        (lambda: seg(8), normed_out(ck_ref, g64_ref, 3)),
        (lambda: seg(9), plain_out(cv_ref)),
        (lambda: seg(10), normed_out(dq_ref, g64_ref, 4)),
        (lambda: seg(11), normed_out(dk_ref, g64_ref, 5)),
        (lambda: seg(12), plain_out(dv_ref)),
        (lambda: seg(13), normed_out(eq_ref, g64_ref, 6)),
    ]
    cur = tasks[0][0]()
    for i, (_, fin) in enumerate(tasks):
        nxt = tasks[i + 1][0]() if i + 1 < len(tasks) else None
        fin(cur)
        cur = nxt


def _project(x, ng, w1, wvt, ep, g32, g64):
    t = x.shape[0]
    tm = PROJ_TM
    tile = lambda w: pl.BlockSpec((tm, w), lambda i: (i, 0))
    widths = [BR_W, BR_W, 2 * BR_W, 2 * BR_W] + [BR_W] * 10
    dtypes = [BF16, BF16, F32, BF16] + [BF16] * 10
    out_specs = [tile(w) for w in widths]
    out_shape = [jax.ShapeDtypeStruct((t, w), dt) for w, dt in zip(widths, dtypes)]
    kt = tm // DF_T
    out_specs[6] = pl.BlockSpec((kt, BR_HEADS, DF_VROWS, DF_T), lambda i: (i, 0, 0, 0))
    out_shape[6] = jax.ShapeDtypeStruct((t // DF_T, BR_HEADS, DF_VROWS, DF_T), BF16)
    return pl.pallas_call(
        _proj_body,
        grid=(t // tm,),
        in_specs=[tile(D_MODEL), _const_spec((1, D_MODEL)), _const_spec(w1.shape), _const_spec(wvt.shape),
                  _const_spec(ep.shape), _const_spec(g32.shape), _const_spec(g64.shape)],
        out_specs=out_specs,
        out_shape=out_shape,
        compiler_params=_params(("parallel",)),
        name="proj",
    )(x, ng, w1, wvt, ep, g32, g64)


def _memkv_body(m_ref, g_ref, w_ref, gk_ref, g64_ref, mk_ref, mv_ref):
    mh = _rms_rows(m_ref[...], g_ref[...]).astype(BF16)
    kv = _dot(mh, w_ref[...])
    mk_ref[...] = (_group_rms(kv[:, :BR_W], g64_ref[...]) * gk_ref[...]).astype(BF16)
    mv_ref[...] = kv[:, BR_W:].astype(BF16)


def _mem_kv(mem, g, w, gk, g64):
    t = mem.shape[0]
    tm = 256
    return pl.pallas_call(
        _memkv_body,
        grid=(t // tm,),
        in_specs=[pl.BlockSpec((tm, D_MODEL), lambda i: (i, 0)), _const_spec((1, D_MODEL)), _const_spec(w.shape),
                  _const_spec((1, BR_W)), _const_spec(g64.shape)],
        out_specs=[pl.BlockSpec((tm, BR_W), lambda i: (i, 0))] * 2,
        out_shape=[jax.ShapeDtypeStruct((t, BR_W), BF16)] * 2,
        compiler_params=_params(("parallel",)),
        name="memkv",
    )(mem, g, w, gk, g64)


def _hgrn_constants(c, reverse):
    nl = int(math.log2(c))
    idx = np.arange(c)
    t = idx[:, None]
    u = idx[None, :]
    incl = (u <= t).astype(np.float32)
    rest = (u > t).astype(np.float32)
    tot = np.ones((8, c), np.float32)
    mds, mes, lms = [], [], [np.eye(c, dtype=np.float32)]
    for lev in range(nl):
        w = 1 << lev
        blk = idx // w
        odd = (blk % 2 == 1)
        md = (odd[:, None] & (u >= (blk * w)[:, None]) & (u <= t)).astype(np.float32)
        me = ((~odd)[:, None] & (u > t) & (u <= ((blk + 1) * w - 1)[:, None])).astype(np.float32)
        lm = (odd[:, None] & (blk[None, :] == (blk - 1)[:, None])).astype(np.float32)
        mds.append(md)
        mes.append(me)
        lms.append(lm)
    del rest
    mats = [incl, tot] + [md + me for md, me in zip(mds[:HG_MAT_LEVELS], mes[:HG_MAT_LEVELS])]
    if reverse:
        mats = [m[::-1, ::-1] for m in mats]
        lms = [m[::-1, ::-1] for m in lms]
    mall = np.concatenate(mats, axis=0)
    lmst = np.stack([np.tile(m, (1, BR_HEADS)) for m in lms])
    bd = np.kron(np.eye(BR_HEADS, dtype=np.float32), np.ones((HEAD_DIM, HEAD_DIM), np.float32))
    return mall, lmst, bd


def _hgrn_chunk(layout, chunk, reverse, q_ref, k_ref, v_ref, lf_ref, mall_ref, lm_ref, bd_ref, o_ref, st_ref):
    c = HG_C
    nl = int(math.log2(c))
    tok = chunk * c
    start, length = layout.seq_bounds(tok)
    fresh = (tok + c == start + length) if reverse else (tok == start)

    masks = _head_mask()
    lf = lf_ref[...]
    hi = lf.astype(BF16)
    mid = (lf - hi.astype(F32)).astype(BF16)
    mall = mall_ref[...]
    cums = _dot(mall, hi) + _dot(mall, mid)
    b = cums[0:c, :]
    b_tot = cums[c:c + 1, :]
    b_rest = b_tot - b
    base = c + 8

    def level_decay(lev):
        if lev < HG_MAT_LEVELS:
            return cums[base + lev * c:base + (lev + 1) * c, :]
        w = 1 << lev
        b3 = b.reshape(c // (2 * w), 2 * w, BR_W)
        r = w if reverse else w - 1
        d3 = b3 - b3[:, r:r + 1, :]
        second = lax.broadcasted_iota(jnp.int32, (1, 2 * w, 1), 1) >= w
        query_half = jnp.logical_not(second) if reverse else second
        return jnp.where(query_half, d3, -d3).reshape(c, BR_W)

    qf = q_ref[...].astype(F32)
    kf = k_ref[...].astype(F32)
    v = v_ref[...]
    st = jnp.where(fresh, 0.0, st_ref[...])

    o = _dot_nt((qf * jnp.exp(b)).astype(BF16), st.astype(BF16))
    dim_row = lax.broadcasted_iota(jnp.int32, (BR_W, 1), 0)
    head_rows = [(dim_row >= h * HEAD_DIM) & (dim_row < (h + 1) * HEAD_DIM) for h in range(BR_HEADS)]

    def scores(ql, kl):
        kt = kl.T.astype(BF16)
        rhs = jnp.concatenate([jnp.where(hr, kt, jnp.zeros_like(kt)) for hr in head_rows], axis=1)
        return _dot(ql, rhs)

    a_all = scores(q_ref[...], kf) * lm_ref[0]
    for lev in range(nl):
        dec = jnp.exp(level_decay(lev))
        a_all = a_all + scores((qf * dec).astype(BF16), kf * dec) * lm_ref[lev + 1]
    a_bf = a_all.astype(BF16)
    zero = jnp.zeros_like(v)
    for h, mk in enumerate(masks):
        o = o + _dot(a_bf[:, h * c:(h + 1) * c], jnp.where(mk, v, zero))
    o_ref[...] = o

    kst = (kf * jnp.exp(b_rest)).astype(BF16)
    vt = v.astype(F32).T.astype(BF16)
    st_ref[...] = (st * jnp.exp(b_tot) + _dot(vt, kst)) * bd_ref[...]


def _hgrn_body(layout, nct, qf_ref, kf_ref, vf_ref, lff_ref, qb_ref, kb_ref, vb_ref, lfb_ref,
               mallf_ref, lmf_ref, mallb_ref, lmb_ref, bd_ref, of_ref, ob_ref, stf_ref, stb_ref):
    i = pl.program_id(0)
    _hgrn_chunk(layout, i, False, qf_ref, kf_ref, vf_ref, lff_ref, mallf_ref, lmf_ref, bd_ref, of_ref, stf_ref)
    _hgrn_chunk(layout, nct - 1 - i, True, qb_ref, kb_ref, vb_ref, lfb_ref, mallb_ref, lmb_ref, bd_ref,
                ob_ref, stb_ref)


def _hgrn_norm_body(of_ref, ob_ref, g64_ref, gn_ref, o_ref):
    o_ref[...] = (_group_rms(of_ref[...] + ob_ref[...], g64_ref[...]) * gn_ref[...]).astype(BF16)


def _hgrn(layout, qs, v, kk, logf, consts_f, consts_b, g64, gn):
    t = qs.shape[0]
    c = HG_C
    nct = t // c
    mall_f, lm_f, bd = consts_f
    mall_b, lm_b, _ = consts_b

    def specs(reverse):
        cm = (lambda i: nct - 1 - i) if reverse else (lambda i: i)
        d = 1 if reverse else 0
        tile = pl.BlockSpec((c, BR_W), lambda i: (cm(i), 0))
        half = pl.BlockSpec((c, BR_W), lambda i: (cm(i), d))
        return tile, half

    tile_f, half_f = specs(False)
    tile_b, half_b = specs(True)
    o_f, o_b = pl.pallas_call(
        functools.partial(_hgrn_body, layout, nct),
        grid=(nct,),
        in_specs=[tile_f, half_f, tile_f, half_f, tile_b, half_b, tile_b, half_b,
                  _const_spec(mall_f.shape), _const_spec(lm_f.shape), _const_spec(mall_b.shape),
                  _const_spec(lm_b.shape), _const_spec(bd.shape)],
        out_specs=[tile_f, tile_b],
        out_shape=[jax.ShapeDtypeStruct((t, BR_W), F32)] * 2,
        scratch_shapes=[pltpu.VMEM((BR_W, BR_W), F32)] * 2,
        compiler_params=_params(("arbitrary",)),
        name="hgrn_scan",
    )(qs, kk, v, logf, qs, kk, v, logf, mall_f, lm_f, mall_b, lm_b, bd)

    tm = math.gcd(t, HG_NORM_TM)
    tok = pl.BlockSpec((tm, BR_W), lambda i: (i, 0))
    return pl.pallas_call(
        _hgrn_norm_body,
        grid=(t // tm,),
        in_specs=[tok, tok, _const_spec(g64.shape), _const_spec((1, BR_W))],
        out_specs=tok,
        out_shape=jax.ShapeDtypeStruct((t, BR_W), BF16),
        compiler_params=_params(("parallel",)),
        name="hgrn_norm",
    )(o_f, o_b, g64, gn)


def _diff_body(nkt, q_ref, k_ref, vt_ref, bias_ref, sc_ref, gcol_ref, o_ref, qm_ref, m_ref, acc_ref,
               s0_ref, s1_ref, mx0_ref, mx1_ref, p_ref):
    tq = DF_T
    qi = pl.program_id(1)
    qt = q_ref[...].astype(F32).T.astype(BF16)
    rowid = lax.broadcasted_iota(jnp.int32, (DF_KH, tq), 0)
    for hc in range(2 * BR_HEADS):
        half, sub = divmod(hc * DF_DK, DF_KH)
        sel = (rowid >= sub) & (rowid < sub + DF_DK)
        qh = qt[half * DF_KH:(half + 1) * DF_KH, :]
        qm_ref[hc] = jnp.where(sel, qh, jnp.zeros_like(qh))
    m_ref[...] = jnp.full(m_ref.shape, NEG, F32)
    acc_ref[...] = jnp.zeros_like(acc_ref)

    nu = 2 * BR_HEADS
    rows = [(r, r + DF_RC) for r in range(0, tq, DF_RC)]

    def qk_unit(kj, hc, buf, near=None):
        sbuf, mxbuf = buf
        half = hc * DF_DK // DF_KH
        s = _dot(k_ref[kj, :, half * DF_KH:(half + 1) * DF_KH], qm_ref[hc])
        if near is not None:
            s = s + bias_ref[near, hc // 2]
        sbuf[hc] = s
        mxbuf[hc] = jnp.max(s.reshape(tq // 8, 8, tq), axis=0)

    def softmax_pv_unit(kj, hc, buf, const_bias):
        sbuf, mxbuf = buf
        c = const_bias(hc // 2)
        m_old = m_ref[hc]
        m_new = jnp.maximum(m_old, jnp.max(mxbuf[hc], axis=0, keepdims=True) + c)
        shift = m_new - c
        p = jnp.concatenate([jnp.exp2(sbuf[hc, r0:r1, :] - shift).astype(BF16) for r0, r1 in rows], axis=0)
        m_ref[hc] = m_new
        acc_ref[hc] = acc_ref[hc] * jnp.exp2(m_old - m_new) + _dot(vt_ref[kj, hc // 2], p)

    def step(kj, buf, const_bias, nxt):
        for hc in range(nu):
            softmax_pv_unit(kj, hc, buf, const_bias)
            qk_unit(nxt[0], hc, nxt[1], nxt[2])

    last = nkt - 1
    n_low = jnp.maximum(qi - 1, 0)
    n_far = n_low + jnp.maximum(nkt - qi - 2, 0)
    buf_a = (s0_ref, mx0_ref)
    buf_b = (s1_ref, mx1_ref)

    def far_tile(f):
        return jnp.minimum(jnp.where(f < n_low, f, f - n_low + qi + 2), last)

    def dead_if(cond):
        pen = jnp.where(cond, 2.0 * NEG, 0.0)
        return lambda h: pen

    near = [jnp.clip(qi + d - 1, 0, last) for d in range(3)]
    for hc in range(nu):
        qk_unit(near[0], hc, buf_a, 0)
    step(near[0], buf_a, dead_if(qi == 0), (near[1], buf_b, 1))
    step(near[1], buf_b, dead_if(False), (near[2], buf_a, 2))
    step(near[2], buf_a, dead_if(qi == last), (far_tile(0), buf_b, None))

    def far_step(f, cur, nxt):
        kj = far_tile(f)
        row = jnp.where(kj < qi, 0, 1)
        live = f < n_far
        step(kj, cur, lambda h: jnp.where(live, sc_ref[row, h], 2.0 * NEG), (far_tile(f + 1), nxt, None))

    far_step(0, buf_b, buf_a)

    def body(i, carry):
        f0 = 1 + DF_UNROLL * i
        for u in range(DF_UNROLL):
            cur, nxt = (buf_a, buf_b) if u % 2 == 0 else (buf_b, buf_a)
            far_step(f0 + u, cur, nxt)
        return carry
    lax.fori_loop(0, (n_far - 1 + DF_UNROLL - 1) // DF_UNROLL, body, 0)

    lmb = sc_ref[2, 0]
    outs = []
    for h in range(BR_HEADS):
        a0 = acc_ref[2 * h]
        a1 = acc_ref[2 * h + 1]
        o0 = a0[:HEAD_DIM, :] / a0[HEAD_DIM:HEAD_DIM + 1, :]
        o1 = a1[:HEAD_DIM, :] / a1[HEAD_DIM:HEAD_DIM + 1, :]
        o = o0 - lmb * o1
        ms = jnp.mean(o * o, axis=0, keepdims=True)
        outs.append(o * lax.rsqrt(ms + EPS))
    ot = jnp.concatenate(outs, axis=0) * gcol_ref[...]
    o_ref[...] = ot.T.astype(BF16)


def _diff_attn_class(off, nseq, n, qt, k3, vt4, bias, sc, gcol):
    t = DF_T
    nkt = n // t
    qb = off // t
    sb = off // n
    return pl.pallas_call(
        functools.partial(_diff_body, nkt),
        grid=(nseq, nkt),
        in_specs=[pl.BlockSpec((t, BR_W), lambda s, i: (qb + s * nkt + i, 0)),
                  pl.BlockSpec((nkt, t, BR_W), lambda s, i: (sb + s, 0, 0), pipeline_mode=pl.Buffered(1)),
                  pl.BlockSpec((nkt, BR_HEADS, DF_VROWS, t), lambda s, i: (sb + s, 0, 0, 0),
                               pipeline_mode=pl.Buffered(1)),
                  _const_spec(bias.shape),
                  pl.BlockSpec(memory_space=pltpu.SMEM),
                  _const_spec(gcol.shape)],
        out_specs=pl.BlockSpec((t, BR_W), lambda s, i: (s * nkt + i, 0)),
        out_shape=jax.ShapeDtypeStruct((nseq * n, BR_W), BF16),
        scratch_shapes=[pltpu.VMEM((2 * BR_HEADS, DF_KH, t), BF16),
                        pltpu.VMEM((2 * BR_HEADS, 1, t), F32),
                        pltpu.VMEM((2 * BR_HEADS, DF_VROWS, t), F32),
                        pltpu.VMEM((2 * BR_HEADS, t, t), F32),
                        pltpu.VMEM((2 * BR_HEADS, t, t), F32),
                        pltpu.VMEM((2 * BR_HEADS, 8, t), F32),
                        pltpu.VMEM((2 * BR_HEADS, 8, t), F32),
                        pltpu.VMEM((2 * BR_HEADS, t, t), BF16)],
        compiler_params=_params(("parallel", "parallel")),
        name="diff_attn",
    )(qt, k3, vt4, bias, sc, gcol)


def _diff_attn(layout, q, k, vt4, bias, sc, gcol):
    t = k.shape[0]
    k3 = k.reshape(t // DF_T, DF_T, BR_W)
    outs = [_diff_attn_class(off, nseq, n, q, k3, vt4, bias, sc, gcol) for off, nseq, n in layout.classes]
    return jnp.concatenate(outs, axis=0)


def _win_body(layout, q_ref, kp_ref, kc_ref, kn_ref, vp_ref, vc_ref, vn_ref, bias_ref, sink_ref, o_ref):
    blk = WIN_BLOCK
    i = pl.program_id(0)
    tok = i * (WIN_NB * blk)
    start, length = layout.seq_bounds(tok)
    masks = _head_mask()
    kcat = jnp.concatenate([kp_ref[...], kc_ref[...], kn_ref[...]], axis=0)
    vcat = jnp.concatenate([vp_ref[...], vc_ref[...], vn_ref[...]], axis=0)
    col = lax.broadcasted_iota(jnp.int32, (1, 3 * blk), 1)
    sink = sink_ref[...]
    bias = bias_ref[...]
    subs = range(WIN_NB)

    def dead_cols(j):
        has_prev = tok + j * blk > start
        has_next = tok + (j + 1) * blk < start + length
        return ((col < blk) & jnp.logical_not(has_prev)) | ((col >= 2 * blk) & jnp.logical_not(has_next))

    s = [_dot_nt(_stack_heads(q_ref[j * blk:(j + 1) * blk, :], masks), kcat[j * blk:(j + 3) * blk, :]) for j in subs]
    s = [s[j] + bias + jnp.where(dead_cols(j), NEG, 0.0) for j in subs]
    m = [jnp.maximum(jnp.max(s[j], axis=-1, keepdims=True), sink) for j in subs]
    p = [jnp.exp2(s[j] - m[j]) for j in subs]
    den = [jnp.sum(p[j], axis=-1, keepdims=True) + jnp.exp2(sink - m[j]) for j in subs]
    pn = [(p[j] * (1.0 / den[j])).astype(BF16) for j in subs]
    o_all = [_dot(pn[j], vcat[j * blk:(j + 3) * blk, :]) for j in subs]
    for j in subs:
        o_ref[j * blk:(j + 1) * blk, :] = _unstack_heads(o_all[j], masks, blk).astype(BF16)


def _win_attn(layout, q, k, v, bias, sink):
    t = q.shape[0]
    blk = WIN_BLOCK
    nb = t // blk
    cur = pl.BlockSpec((WIN_NB * blk, BR_W), lambda i: (i, 0))
    prev = pl.BlockSpec((blk, BR_W), lambda i: (jnp.maximum(i * WIN_NB - 1, 0), 0))
    nxt = pl.BlockSpec((blk, BR_W), lambda i: (jnp.minimum((i + 1) * WIN_NB, nb - 1), 0))
    return pl.pallas_call(
        functools.partial(_win_body, layout),
        grid=(nb // WIN_NB,),
        in_specs=[cur, prev, cur, nxt, prev, cur, nxt, _const_spec(bias.shape), _const_spec(sink.shape)],
        out_specs=cur,
        out_shape=jax.ShapeDtypeStruct((t, BR_W), BF16),
        compiler_params=_params(("parallel",)),
        name="win_attn",
    )(q, k, k, k, v, v, v, bias, sink)


def _na_body(rows, q_ref, k_ref, v_ref, bias_ref, o_ref):
    j = pl.program_id(1)
    masks = _head_mask()
    nk = NA_KH * GRID_W
    rr = range(NA_RB)
    r = [j * NA_RB + i for i in rr]
    rs = [jnp.clip(r[i] - NA_KH // 2, 0, rows - NA_KH) for i in rr]
    koff = [pl.multiple_of(rs[i] * GRID_W, GRID_W) for i in rr]
    s = [_dot_nt(_stack_heads(q_ref[i * GRID_W:(i + 1) * GRID_W, :], masks), k_ref[pl.ds(koff[i], nk), :])
         + bias_ref[r[i] - rs[i]] for i in rr]
    m = [jnp.max(s[i], axis=-1, keepdims=True) for i in rr]
    p = [jnp.exp2(s[i] - m[i]) for i in rr]
    pn = [(p[i] * (1.0 / jnp.sum(p[i], axis=-1, keepdims=True))).astype(BF16) for i in rr]
    o_all = [_dot(pn[i], v_ref[pl.ds(koff[i], nk), :]) for i in rr]
    for i in rr:
        o_ref[i * GRID_W:(i + 1) * GRID_W, :] = _unstack_heads(o_all[i], masks, GRID_W).astype(BF16)


def _na_class(off, nseq, n, q, k, v, bias):
    rows = n // GRID_W
    qt = NA_RB * GRID_W
    nj = n // qt
    qb = off // qt
    sb = off // n
    seq = pl.BlockSpec((n, BR_W), lambda s, j: (sb + s, 0), pipeline_mode=pl.Buffered(1))
    return pl.pallas_call(
        functools.partial(_na_body, rows),
        grid=(nseq, nj),
        in_specs=[pl.BlockSpec((qt, BR_W), lambda s, j: (qb + s * nj + j, 0)), seq, seq, _const_spec(bias.shape)],
        out_specs=pl.BlockSpec((qt, BR_W), lambda s, j: (s * nj + j, 0)),
        out_shape=jax.ShapeDtypeStruct((nseq * n, BR_W), BF16),
        compiler_params=_params(("parallel", "parallel")),
        name="na_attn",
    )(q, k, v, bias)


def _na_attn(layout, q, k, v, bias):
    return jnp.concatenate([_na_class(off, nseq, n, q, k, v, bias) for off, nseq, n in layout.classes], axis=0)


def _mem_body(q_ref, mk_ref, mv_ref, o_ref):
    masks = _head_mask()
    sub = MEM_TM // MEM_SPLIT
    parts = range(MEM_SPLIT)
    mk = mk_ref[0]
    mv = mv_ref[0]
    s = [_dot_nt(_stack_heads(q_ref[i * sub:(i + 1) * sub, :], masks), mk) for i in parts]
    m = [jnp.max(s[i], axis=-1, keepdims=True) for i in parts]
    p = [jnp.exp2(s[i] - m[i]) for i in parts]
    pn = [(p[i] * (1.0 / jnp.sum(p[i], axis=-1, keepdims=True))).astype(BF16) for i in parts]
    o_all = [_dot(pn[i], mv) for i in parts]
    for i in parts:
        o_ref[i * sub:(i + 1) * sub, :] = _unstack_heads(o_all[i], masks, sub).astype(BF16)


def _mem_attn(layout, q, mk, mv):
    t = q.shape[0]
    tm = MEM_TM
    mem_len = mk.shape[1]
    tile = pl.BlockSpec((tm, BR_W), lambda i: (i, 0))
    mem = pl.BlockSpec((1, mem_len, BR_W), lambda i: (layout.seq_index(i * tm), 0, 0))
    return pl.pallas_call(
        _mem_body,
        grid=(t // tm,),
        in_specs=[tile, mem, mem],
        out_specs=tile,
        out_shape=jax.ShapeDtypeStruct((t, BR_W), BF16),
        compiler_params=_params(("parallel",)),
        name="mem_attn",
    )(q, mk, mv)


def _merge_body(x_ref, ng_ref, oa_ref, ob_ref, oc_ref, od_ref, oe_ref, wg_ref, wm_ref, wb_ref, wo_ref, y_ref):
    x = x_ref[...]
    h = _rms_rows(x, ng_ref[...]).astype(BF16)
    merged = jnp.zeros((x.shape[0], D_MODEL), F32)
    for kb, o_ref in enumerate((oa_ref, ob_ref, oc_ref, od_ref, oe_ref)):
        g = _dot(h, wg_ref[:, kb * BR_W:(kb + 1) * BR_W])
        br = (o_ref[...].astype(F32) * (g * jax.nn.sigmoid(g))).astype(BF16)
        mg = jax.nn.sigmoid(_dot(h, wm_ref[:, kb * D_MODEL:(kb + 1) * D_MODEL]))
        merged = merged + mg * _dot(br, wb_ref[kb])
    y_ref[...] = x + _dot(merged.astype(BF16), wo_ref[...])


def _merge(x, ng, branches, wg, wm, wb, wo):
    t = x.shape[0]
    tm = MERGE_TM
    xt = pl.BlockSpec((tm, D_MODEL), lambda i: (i, 0))
    bt = pl.BlockSpec((tm, BR_W), lambda i: (i, 0))
    return pl.pallas_call(
        _merge_body,
        grid=(t // tm,),
        in_specs=[xt, _const_spec((1, D_MODEL))] + [bt] * N_BRANCH
        + [_const_spec(wg.shape), _const_spec(wm.shape), _const_spec(wb.shape), _const_spec(wo.shape)],
        out_specs=xt,
        out_shape=jax.ShapeDtypeStruct((t, D_MODEL), F32),
        input_output_aliases={0: 0},
        compiler_params=_params(("parallel",)),
        name="merge",
    )(x, ng, *branches, wg, wm, wb, wo)


def _t5_bucket(rel):
    half = N_BUCKETS // 2
    exact = half // 2
    n = jnp.abs(rel)
    nf = jnp.maximum(n, 1).astype(F32)
    large = exact + (jnp.log(nf / exact) / math.log(MAX_DIST / exact) * (half - exact)).astype(jnp.int32)
    large = jnp.clip(large, 0, half - 1)
    return jnp.where(rel > 0, half, 0) + jnp.where(n < exact, n, large)


def _group_matrix(group):
    return jnp.asarray(np.kron(np.eye(BR_W // group), np.full((group, group), 1.0 / group)), BF16)


def _lookup(table, idx):
    onehot = (idx[..., None] == jnp.arange(table.shape[0])).astype(F32)
    return jnp.dot(onehot, table, precision=lax.Precision.HIGHEST)


def _diff_bias_tables(rel_bias):
    t = DF_T
    table = rel_bias[:, :BR_HEADS].astype(F32) * LOG2E
    kl = jnp.arange(t)[:, None]
    ql = jnp.arange(t)[None, :]
    rel = jnp.stack([kl - ql + d * t for d in (-1, 0, 1)])
    tiles = _lookup(table, _t5_bucket(rel)).transpose(0, 3, 1, 2)
    far = _lookup(table, _t5_bucket(jnp.asarray([-2 * t, 2 * t], jnp.int32)))
    return tiles, far


def _win_bias_table(rel_bias):
    rel = jnp.arange(3 * WIN_BLOCK)[None, :] - WIN_BLOCK - jnp.arange(WIN_BLOCK)[:, None]
    bias = _lookup(rel_bias[:, BR_HEADS:].astype(F32) * LOG2E, _t5_bucket(rel)).transpose(2, 0, 1)
    bias = jnp.where((jnp.abs(rel) <= WIN)[None], bias, NEG)
    return bias.reshape(BR_HEADS * WIN_BLOCK, 3 * WIN_BLOCK)


def _na_bias_tables(rpb):
    depth = rpb.shape[0]
    col = np.arange(GRID_W)
    cs = np.clip(col - NA_KW // 2, 0, GRID_W - NA_KW)
    inwin = (col[None, :] >= cs[:, None]) & (col[None, :] < cs[:, None] + NA_KW)
    dc = col[None, :] - col[:, None] + (NA_KW - 1)
    onehot = jnp.asarray(dc[None] == np.arange(2 * NA_KW - 1)[:, None, None], F32)
    toep = jnp.einsum("lhrc,cqk->lhrqk", rpb.astype(F32) * LOG2E, onehot, precision=lax.Precision.HIGHEST)
    toep = jnp.where(jnp.asarray(inwin)[None, None, None], toep, NEG)
    out = []
    for d in range(NA_KH):
        lo = NA_KH - 1 - d
        b = toep[:, :, lo:lo + NA_KH].transpose(0, 1, 3, 2, 4)
        out.append(b.reshape(depth, BR_HEADS * GRID_W, NA_KH * GRID_W))
    return jnp.stack(out, axis=1)


def _tile_gain(g, reps, scale=1.0):
    return jnp.tile(g.astype(F32), reps)[None, :] * scale


def kernel(x_prompt, x_sample, mem_prompt, mem_sample, norm_g, mem_norm_g, w_in, w_mem_kv, rel_bias, hgrn_lb,
           hgrn_norm_g, diff_qk_g, diff_lambda, diff_subln_g, win_qk_g, win_sink, na_qk_g, na_rpb, mem_qk_g,
           w_branch, w_out):
    depth = w_in.shape[0]
    bp, lp, _ = x_prompt.shape
    bs, ls, _ = x_sample.shape
    mem_len = mem_prompt.shape[1]
    layout = _Layout(bp, lp, bs, ls)
    t = layout.total
    assert t % PROJ_TM == 0 and t % MERGE_TM == 0 and lp % DF_T == 0 and ls % DF_T == 0
    assert lp % (NA_RB * GRID_W) == 0 and ls % (NA_RB * GRID_W) == 0 and lp % MEM_TM == 0 and ls % MEM_TM == 0
    assert lp % (WIN_NB * WIN_BLOCK) == 0 and ls % (WIN_NB * WIN_BLOCK) == 0

    x = jnp.concatenate([x_prompt.reshape(bp * lp, D_MODEL), x_sample.reshape(bs * ls, D_MODEL)], axis=0)
    mem = jnp.concatenate([mem_prompt.reshape(bp * mem_len, D_MODEL), mem_sample.reshape(bs * mem_len, D_MODEL)], axis=0)

    sm = jax.nn.softmax(hgrn_lb.astype(F32), axis=1)
    lb_all = jnp.clip(jnp.cumsum(sm, axis=1) - sm[:, :1], 0.0, 1.0 - 1e-6)
    lam_init = jnp.asarray([0.8 - 0.6 * math.exp(-0.3 * l) for l in range(depth)], F32)
    lam = diff_lambda.astype(F32)
    lmb = jnp.exp(jnp.sum(lam[:, 0] * lam[:, 1], axis=-1)) - jnp.exp(jnp.sum(lam[:, 2] * lam[:, 3], axis=-1)) + lam_init

    w_in_b = w_in.astype(BF16)

    def expand_kv(w):
        w = w.reshape(depth, D_MODEL, WIN_KV_HEADS, HEAD_DIM)
        return jnp.repeat(w, BR_HEADS // WIN_KV_HEADS, axis=2).reshape(depth, D_MODEL, BR_W)

    w1 = jnp.concatenate([w_in_b[:, :, A_Q:A_G], w_in_b[:, :, B_Q:B_G], w_in_b[:, :, C_Q:C_K],
                          expand_kv(w_in_b[:, :, C_K:C_V]), expand_kv(w_in_b[:, :, C_V:C_G]),
                          w_in_b[:, :, D_Q:D_G], w_in_b[:, :, E_Q:E_G]], axis=-1)
    wvt = w_in_b[:, :, B_Q + 2 * BR_W:B_G].transpose(0, 2, 1)
    wg = jnp.concatenate([w_in_b[:, :, A_G:B_Q], w_in_b[:, :, B_G:C_Q], w_in_b[:, :, C_G:D_Q],
                          w_in_b[:, :, D_G:E_Q], w_in_b[:, :, E_G:M_G]], axis=-1)
    wm = w_in_b[:, :, M_G:]
    wb = w_branch.astype(BF16)
    wo = w_out.astype(BF16)
    wmem = w_mem_kv.astype(BF16)

    sc_b = DF_DK ** -0.5 * LOG2E
    sc_h = HEAD_DIM ** -0.5 * LOG2E
    zrow = jnp.zeros((depth, 1, BR_W), F32)

    def per_layer(fn):
        return jnp.stack([fn(l) for l in range(depth)])

    ep = jnp.concatenate([
        per_layer(lambda l: _tile_gain(diff_qk_g[l, 0], 8, sc_b)),
        per_layer(lambda l: _tile_gain(diff_qk_g[l, 1], 8)),
        per_layer(lambda l: _tile_gain(win_qk_g[l, 0], 4, sc_h)),
        per_layer(lambda l: _tile_gain(win_qk_g[l, 1], 4)),
        per_layer(lambda l: _tile_gain(na_qk_g[l, 0], 4, sc_h)),
        per_layer(lambda l: _tile_gain(na_qk_g[l, 1], 4)),
        per_layer(lambda l: _tile_gain(mem_qk_g[l, 0], 4, sc_h)),
        lb_all[0][:, None, :], lb_all[1][:, None, :]] + [zrow] * 7, axis=1)
    gk_mem = per_layer(lambda l: _tile_gain(mem_qk_g[l, 1], 4))
    gn_hg = hgrn_norm_g.astype(F32)[:, None, :]
    gcol = per_layer(lambda l: (jnp.tile(diff_subln_g[l].astype(F32), BR_HEADS) * (1.0 - lam_init[l]))[:, None])
    sink = per_layer(lambda l: jnp.repeat(win_sink[l].astype(F32) * LOG2E, WIN_BLOCK)[:, None])
    na_bias = _na_bias_tables(na_rpb)

    diff_bias, diff_far = _diff_bias_tables(rel_bias)
    win_bias = _win_bias_table(rel_bias)
    sc = jnp.concatenate([jnp.broadcast_to(diff_far[None], (depth, 2, BR_HEADS)),
                          jnp.broadcast_to(lmb[:, None, None], (depth, 1, BR_HEADS))], axis=1)

    g32 = _group_matrix(DF_DK)
    g64 = _group_matrix(HEAD_DIM)
    consts_f = tuple(jnp.asarray(a, dt) for a, dt in zip(_hgrn_constants(HG_C, False), (BF16, F32, F32)))
    consts_b = tuple(jnp.asarray(a, dt) for a, dt in zip(_hgrn_constants(HG_C, True), (BF16, F32, F32)))

    def layer(x, p):
        ng = p["ng"]
        (a_q, a_v, a_lf, a_kk, b_q, b_k, b_v, c_q, c_k, c_v, d_q, d_k, d_v, e_q) = _project(
            x, ng, p["w1"], p["wvt"], p["ep"], g32, g64)
        mk, mv = _mem_kv(mem, p["mng"], p["wmem"], p["gk_mem"], g64)
        o_a = _hgrn(layout, a_q, a_v, a_kk, a_lf, consts_f, consts_b, g64, p["gn_hg"])
        o_b = _diff_attn(layout, b_q, b_k, b_v, diff_bias, p["sc"], p["gcol"])
        o_c = _win_attn(layout, c_q, c_k, c_v, win_bias, p["sink"])
        o_d = _na_attn(layout, d_q, d_k, d_v, p["na_bias"])
        o_e = _mem_attn(layout, e_q, mk.reshape(layout.nseq, mem_len, BR_W), mv.reshape(layout.nseq, mem_len, BR_W))
        y = _merge(x, ng, (o_a, o_b, o_c, o_d, o_e), p["wg"], p["wm"], p["wb"], p["wo"])
        return y, None

    params = dict(ng=norm_g.astype(F32)[:, None, :], mng=mem_norm_g.astype(F32)[:, None, :], w1=w1, wvt=wvt, ep=ep,
                  wmem=wmem, gk_mem=gk_mem, gn_hg=gn_hg, sc=sc, gcol=gcol, sink=sink, na_bias=na_bias,
                  wg=wg, wm=wm, wb=wb, wo=wo)
    x, _ = lax.scan(layer, x, params)
    y_prompt = x[:layout.off1].reshape(bp, lp, D_MODEL)
    y_sample = x[layout.off1:].reshape(bs, ls, D_MODEL)
    return (y_prompt, y_sample)
```

```python
import functools
import math

import numpy as np
import jax
import jax.numpy as jnp
from jax import lax
from jax.experimental import pallas as pl
from jax.experimental.pallas import tpu as pltpu

F32 = jnp.float32
BF16 = jnp.bfloat16

D_MODEL = 1024
HEAD_DIM = 64
BR_HEADS = 4
BR_W = BR_HEADS * HEAD_DIM
N_BRANCH = 5
DF_DK = HEAD_DIM // 2
WIN = 128
WIN_BLOCK = 128
WIN_KV_HEADS = 2
GRID_W = 64
NA_KH = 8
NA_KW = 16
N_BUCKETS = 32
MAX_DIST = 128
EPS = 1e-6
NEG = -1e30
LB_FLOOR = 1e-30
LOG2E = 1.4426950408889634

A_Q = 0
A_G = 4 * BR_W
B_Q = A_G + BR_W
B_G = B_Q + 3 * BR_W
C_Q = B_G + BR_W
C_K = C_Q + BR_W
C_V = C_K + WIN_KV_HEADS * HEAD_DIM
C_G = C_V + WIN_KV_HEADS * HEAD_DIM
D_Q = C_G + BR_W
D_G = D_Q + 3 * BR_W
E_Q = D_G + BR_W
E_G = E_Q + BR_W
M_G = E_G + BR_W

VMEM_LIMIT_BYTES = 56 * 1024 * 1024

PROJ_TM = 1024
MERGE_TM = 512
MEM_TM = 512
MEM_SPLIT = 4
HG_C = 128
HG_MAT_LEVELS = 2
DF_T = 256
NA_RB = 8
WIN_NB = 4
HG_NORM_TM = 2048
DF_VROWS = 80
DF_RC = 64
DF_KH = 128
DF_UNROLL = 4


def _params(sem):
    return pltpu.CompilerParams(dimension_semantics=sem, vmem_limit_bytes=VMEM_LIMIT_BYTES)


def _const_spec(shape):
    nd = len(shape)
    return pl.BlockSpec(shape, lambda *_: (0,) * nd, pipeline_mode=pl.Buffered(1))


def _dot(a, b):
    return jnp.dot(a, b, preferred_element_type=F32)


def _dot_nt(a, b):
    return lax.dot_general(a, b, (((1,), (1,)), ((), ())), preferred_element_type=F32)


def _rms_rows(x, g):
    ms = jnp.mean(x * x, axis=-1, keepdims=True)
    return x * lax.rsqrt(ms + EPS) * g


def _group_rms(x, gmat):
    x2 = x * x
    hi = x2.astype(BF16)
    lo = (x2 - hi.astype(F32)).astype(BF16)
    ms = _dot(hi, gmat) + _dot(lo, gmat)
    return x * lax.rsqrt(ms + EPS)


def _head_mask(width=BR_W):
    lane = lax.broadcasted_iota(jnp.int32, (1, width), 1)
    return [(lane >= h * HEAD_DIM) & (lane < (h + 1) * HEAD_DIM) for h in range(BR_HEADS)]


def _stack_heads(q, masks):
    zero = jnp.zeros_like(q)
    return jnp.concatenate([jnp.where(m, q, zero) for m in masks], axis=0)


def _unstack_heads(o_all, masks, m):
    out = jnp.zeros((m, BR_W), F32)
    for h, mk in enumerate(masks):
        out = out + jnp.where(mk, o_all[h * m:(h + 1) * m, :], 0.0)
    return out


class _Layout:
    def __init__(self, n_prompt, len_prompt, n_sample, len_sample):
        self.classes = ((0, n_prompt, len_prompt), (n_prompt * len_prompt, n_sample, len_sample))
        self.off1 = n_prompt * len_prompt
        self.lp = len_prompt
        self.ls = len_sample
        self.total = self.off1 + n_sample * len_sample
        self.nseq = n_prompt + n_sample
        self.n_prompt = n_prompt
        assert self.off1 % len_sample == 0

    def seq_bounds(self, tok):
        in_p = tok < self.off1
        start_p = (tok // self.lp) * self.lp
        start_s = self.off1 + ((tok - self.off1) // self.ls) * self.ls
        return jnp.where(in_p, start_p, start_s), jnp.where(in_p, self.lp, self.ls)

    def seq_index(self, tok):
        return jnp.where(tok < self.off1, tok // self.lp, self.n_prompt + (tok - self.off1) // self.ls)


def _proj_body(x_ref, ng_ref, w_ref, wvt_ref, ep_ref, g32_ref, g64_ref,
               aq_ref, av_ref, alf_ref, akk_ref, bq_ref, bk_ref, bv_ref,
               cq_ref, ck_ref, cv_ref, dq_ref, dk_ref, dv_ref, eq_ref):
    h = _rms_rows(x_ref[...], ng_ref[...]).astype(BF16)
    ep = ep_ref[...]

    def seg(i):
        return _dot(h, w_ref[:, i * BR_W:(i + 1) * BR_W])

    def row(r):
        return ep[r:r + 1, :]

    def silu_out(ref):
        def fin(a):
            ref[...] = (a * jax.nn.sigmoid(a)).astype(BF16)
        return fin

    def plain_out(ref):
        def fin(a):
            ref[...] = a.astype(BF16)
        return fin

    def normed_out(ref, g_ref, gain_row):
        def fin(a):
            ref[...] = (_group_rms(a, g_ref[...]) * row(gain_row)).astype(BF16)
        return fin

    def decay_out(d):
        def fin(z):
            lb = row(7 + d)
            e = jnp.exp(-jnp.abs(z))
            log_sig = jnp.minimum(z, 0.0) - jnp.log1p(e)
            t1 = jnp.log(jnp.maximum(lb, LB_FLOOR))
            t2 = jnp.log1p(-lb) + log_sig
            logf = jnp.maximum(t1, t2) + jnp.log1p(jnp.exp(-jnp.abs(t1 - t2)))
            sig_neg = jnp.where(z >= 0.0, e, 1.0) / (1.0 + e)
            alf_ref[:, d * BR_W:(d + 1) * BR_W] = logf
            akk_ref[:, d * BR_W:(d + 1) * BR_W] = ((1.0 - lb) * sig_neg).astype(BF16)
        return fin

    def vt_out(vt):
        vt = vt.astype(BF16)
        pad_row = lax.broadcasted_iota(jnp.int32, (DF_VROWS - HEAD_DIM, DF_T), 0)
        pad = jnp.where(pad_row == 0, 1.0, 0.0).astype(BF16)
        for j in range(PROJ_TM // DF_T):
            for hd in range(BR_HEADS):
                bv_ref[j, hd, 0:HEAD_DIM, :] = vt[hd * HEAD_DIM:(hd + 1) * HEAD_DIM, j * DF_T:(j + 1) * DF_T]
                bv_ref[j, hd, HEAD_DIM:DF_VROWS, :] = pad

    tasks = [
        (lambda: seg(0), silu_out(aq_ref)),
        (lambda: seg(1), plain_out(av_ref)),
        (lambda: seg(2), decay_out(0)),
        (lambda: seg(3), decay_out(1)),
        (lambda: seg(4), normed_out(bq_ref, g32_ref, 0)),
        (lambda: seg(5), normed_out(bk_ref, g32_ref, 1)),
        (lambda: _dot_nt(wvt_ref[...], h), vt_out),
        (lambda: seg(7), normed_out(cq_ref, g64_ref, 2)),
        (lambda: seg(8), normed_out(ck_ref, g64_ref, 3)),
        (lambda: seg(9), plain_out(cv_ref)),
        (lambda: seg(10), normed_out(dq_ref, g64_ref, 4)),
        (lambda: seg(11), normed_out(dk_ref, g64_ref, 5)),
        (lambda: seg(12), plain_out(dv_ref)),
        (lambda: seg(13), normed_out(eq_ref, g64_ref, 6)),
    ]
    cur = tasks[0][0]()
    for i, (_, fin) in enumerate(tasks):
        nxt = tasks[i + 1][0]() if i + 1 < len(tasks) else None
        fin(cur)
        cur = nxt


def _project(x, ng, w1, wvt, ep, g32, g64):
    t = x.shape[0]
    tm = PROJ_TM
    tile = lambda w: pl.BlockSpec((tm, w), lambda i: (i, 0))
    widths = [BR_W, BR_W, 2 * BR_W, 2 * BR_W] + [BR_W] * 10
    dtypes = [BF16, BF16, F32, BF16] + [BF16] * 10
    out_specs = [tile(w) for w in widths]
    out_shape = [jax.ShapeDtypeStruct((t, w), dt) for w, dt in zip(widths, dtypes)]
    kt = tm // DF_T
    out_specs[6] = pl.BlockSpec((kt, BR_HEADS, DF_VROWS, DF_T), lambda i: (i, 0, 0, 0))
    out_shape[6] = jax.ShapeDtypeStruct((t // DF_T, BR_HEADS, DF_VROWS, DF_T), BF16)
    return pl.pallas_call(
        _proj_body,
        grid=(t // tm,),
        in_specs=[tile(D_MODEL), _const_spec((1, D_MODEL)), _const_spec(w1.shape), _const_spec(wvt.shape),
                  _const_spec(ep.shape), _const_spec(g32.shape), _const_spec(g64.shape)],
        out_specs=out_specs,
        out_shape=out_shape,
        compiler_params=_params(("parallel",)),
        name="proj",
    )(x, ng, w1, wvt, ep, g32, g64)


def _memkv_body(m_ref, g_ref, w_ref, gk_ref, g64_ref, mk_ref, mv_ref):
    mh = _rms_rows(m_ref[...], g_ref[...]).astype(BF16)
    kv = _dot(mh, w_ref[...])
    mk_ref[...] = (_group_rms(kv[:, :BR_W], g64_ref[...]) * gk_ref[...]).astype(BF16)
    mv_ref[...] = kv[:, BR_W:].astype(BF16)


def _mem_kv(mem, g, w, gk, g64):
    t = mem.shape[0]
    tm = 256
    return pl.pallas_call(
        _memkv_body,
        grid=(t // tm,),
        in_specs=[pl.BlockSpec((tm, D_MODEL), lambda i: (i, 0)), _const_spec((1, D_MODEL)), _const_spec(w.shape),
                  _const_spec((1, BR_W)), _const_spec(g64.shape)],
        out_specs=[pl.BlockSpec((tm, BR_W), lambda i: (i, 0))] * 2,
        out_shape=[jax.ShapeDtypeStruct((t, BR_W), BF16)] * 2,
        compiler_params=_params(("parallel",)),
        name="memkv",
    )(mem, g, w, gk, g64)


def _hgrn_constants(c, reverse):
    nl = int(math.log2(c))
    idx = np.arange(c)
    t = idx[:, None]
    u = idx[None, :]
    incl = (u <= t).astype(np.float32)
    tot = np.ones((8, c), np.float32)
    mds, mes, lms = [], [], [np.eye(c, dtype=np.float32)]
    for lev in range(nl):
        w = 1 << lev
        blk = idx // w
        odd = (blk % 2 == 1)
        md = (odd[:, None] & (u >= (blk * w)[:, None]) & (u <= t)).astype(np.float32)
        me = ((~odd)[:, None] & (u > t) & (u <= ((blk + 1) * w - 1)[:, None])).astype(np.float32)
        lm = (odd[:, None] & (blk[None, :] == (blk - 1)[:, None])).astype(np.float32)
        mds.append(md)
        mes.append(me)
        lms.append(lm)
    mats = [incl, tot] + [md + me for md, me in zip(mds[:HG_MAT_LEVELS], mes[:HG_MAT_LEVELS])]
    if reverse:
        mats = [m[::-1, ::-1] for m in mats]
        lms = [m[::-1, ::-1] for m in lms]
    mall = np.concatenate(mats, axis=0)
    lmst = np.stack([np.tile(m, (1, BR_HEADS)) for m in lms])
    bd = np.kron(np.eye(BR_HEADS, dtype=np.float32), np.ones((HEAD_DIM, HEAD_DIM), np.float32))
    return mall, lmst, bd


def _hgrn_chunk(layout, chunk, reverse, q_ref, k_ref, v_ref, lf_ref, mall_ref, lm_ref, bd_ref, o_ref, st_ref):
    c = HG_C
    nl = int(math.log2(c))
    tok = chunk * c
    start, length = layout.seq_bounds(tok)
    fresh = (tok + c == start + length) if reverse else (tok == start)

    masks = _head_mask()
    lf = lf_ref[...]
    hi = lf.astype(BF16)
    mid = (lf - hi.astype(F32)).astype(BF16)
    mall = mall_ref[...]
    cums = _dot(mall, hi) + _dot(mall, mid)
    b = cums[0:c, :]
    b_tot = cums[c:c + 1, :]
    b_rest = b_tot - b
    base = c + 8

    def level_decay(lev):
        if lev < HG_MAT_LEVELS:
            return cums[base + lev * c:base + (lev + 1) * c, :]
        w = 1 << lev
        b3 = b.reshape(c // (2 * w), 2 * w, BR_W)
        r = w if reverse else w - 1
        d3 = b3 - b3[:, r:r + 1, :]
        second = lax.broadcasted_iota(jnp.int32, (1, 2 * w, 1), 1) >= w
        query_half = jnp.logical_not(second) if reverse else second
        return jnp.where(query_half, d3, -d3).reshape(c, BR_W)

    qf = q_ref[...].astype(F32)
    kf = k_ref[...].astype(F32)
    v = v_ref[...]
    st = jnp.where(fresh, 0.0, st_ref[...])

    o = _dot_nt((qf * jnp.exp(b)).astype(BF16), st.astype(BF16))
    dim_row = lax.broadcasted_iota(jnp.int32, (BR_W, 1), 0)
    head_rows = [(dim_row >= h * HEAD_DIM) & (dim_row < (h + 1) * HEAD_DIM) for h in range(BR_HEADS)]

    def scores(ql, kl):
        kt = kl.T.astype(BF16)
        rhs = jnp.concatenate([jnp.where(hr, kt, jnp.zeros_like(kt)) for hr in head_rows], axis=1)
        return _dot(ql, rhs)

    a_all = scores(q_ref[...], kf) * lm_ref[0]
    for lev in range(nl):
        dec = jnp.exp(level_decay(lev))
        a_all = a_all + scores((qf * dec).astype(BF16), kf * dec) * lm_ref[lev + 1]
    a_bf = a_all.astype(BF16)
    zero = jnp.zeros_like(v)
    for h, mk in enumerate(masks):
        o = o + _dot(a_bf[:, h * c:(h + 1) * c], jnp.where(mk, v, zero))
    o_ref[...] = o

    kst = (kf * jnp.exp(b_rest)).astype(BF16)
    vt = v.astype(F32).T.astype(BF16)
    st_ref[...] = (st * jnp.exp(b_tot) + _dot(vt, kst)) * bd_ref[...]


def _hgrn_body(layout, nct, qf_ref, kf_ref, vf_ref, lff_ref, qb_ref, kb_ref, vb_ref, lfb_ref,
               mallf_ref, lmf_ref, mallb_ref, lmb_ref, bd_ref, of_ref, ob_ref, stf_ref, stb_ref):
    i = pl.program_id(0)
    _hgrn_chunk(layout, i, False, qf_ref, kf_ref, vf_ref, lff_ref, mallf_ref, lmf_ref, bd_ref, of_ref, stf_ref)
    _hgrn_chunk(layout, nct - 1 - i, True, qb_ref, kb_ref, vb_ref, lfb_ref, mallb_ref, lmb_ref, bd_ref,
                ob_ref, stb_ref)


def _hgrn_norm_body(of_ref, ob_ref, g64_ref, gn_ref, o_ref):
    o_ref[...] = (_group_rms(of_ref[...] + ob_ref[...], g64_ref[...]) * gn_ref[...]).astype(BF16)


def _hgrn(layout, qs, v, kk, logf, consts_f, consts_b, g64, gn):
    t = qs.shape[0]
    c = HG_C
    nct = t // c
    mall_f, lm_f, bd = consts_f
    mall_b, lm_b, _ = consts_b

    def specs(reverse):
        cm = (lambda i: nct - 1 - i) if reverse else (lambda i: i)
        d = 1 if reverse else 0
        tile = pl.BlockSpec((c, BR_W), lambda i: (cm(i), 0))
        half = pl.BlockSpec((c, BR_W), lambda i: (cm(i), d))
        return tile, half

    tile_f, half_f = specs(False)
    tile_b, half_b = specs(True)
    o_f, o_b = pl.pallas_call(
        functools.partial(_hgrn_body, layout, nct),
        grid=(nct,),
        in_specs=[tile_f, half_f, tile_f, half_f, tile_b, half_b, tile_b, half_b,
                  _const_spec(mall_f.shape), _const_spec(lm_f.shape), _const_spec(mall_b.shape),
                  _const_spec(lm_b.shape), _const_spec(bd.shape)],
        out_specs=[tile_f, tile_b],
        out_shape=[jax.ShapeDtypeStruct((t, BR_W), F32)] * 2,
        scratch_shapes=[pltpu.VMEM((BR_W, BR_W), F32)] * 2,
        compiler_params=_params(("arbitrary",)),
        name="hgrn_scan",
    )(qs, kk, v, logf, qs, kk, v, logf, mall_f, lm_f, mall_b, lm_b, bd)

    tm = math.gcd(t, HG_NORM_TM)
    tok = pl.BlockSpec((tm, BR_W), lambda i: (i, 0))
    return pl.pallas_call(
        _hgrn_norm_body,
        grid=(t // tm,),
        in_specs=[tok, tok, _const_spec(g64.shape), _const_spec((1, BR_W))],
        out_specs=tok,
        out_shape=jax.ShapeDtypeStruct((t, BR_W), BF16),
        compiler_params=_params(("parallel",)),
        name="hgrn_norm",
    )(o_f, o_b, g64, gn)


def _diff_body(nkt, q_ref, k_ref, vt_ref, bias_ref, sc_ref, gcol_ref, o_ref, qm_ref, m_ref, acc_ref,
               s0_ref, s1_ref, mx0_ref, mx1_ref):
    tq = DF_T
    qi = pl.program_id(1)
    qt = q_ref[...].astype(F32).T.astype(BF16)
    rowid = lax.broadcasted_iota(jnp.int32, (DF_KH, tq), 0)
    for hc in range(2 * BR_HEADS):
        half, sub = divmod(hc * DF_DK, DF_KH)
        sel = (rowid >= sub) & (rowid < sub + DF_DK)
        qh = qt[half * DF_KH:(half + 1) * DF_KH, :]
        qm_ref[hc] = jnp.where(sel, qh, jnp.zeros_like(qh))
    m_ref[...] = jnp.full(m_ref.shape, NEG, F32)
    acc_ref[...] = jnp.zeros_like(acc_ref)

    nu = 2 * BR_HEADS
    rows = [(r, r + DF_RC) for r in range(0, tq, DF_RC)]

    def qk_unit(kj, hc, buf, near=None):
        sbuf, mxbuf = buf
        half = hc * DF_DK // DF_KH
        s = _dot(k_ref[kj, :, half * DF_KH:(half + 1) * DF_KH], qm_ref[hc])
        if near is not None:
            s = s + bias_ref[near, hc // 2]
        sbuf[hc] = s
        mxbuf[hc] = jnp.max(s.reshape(tq // 8, 8, tq), axis=0)

    def softmax_pv_unit(kj, hc, buf, const_bias):
        sbuf, mxbuf = buf
        c = const_bias(hc // 2)
        m_old = m_ref[hc]
        m_new = jnp.maximum(m_old, jnp.max(mxbuf[hc], axis=0, keepdims=True) + c)
        shift = m_new - c
        p = jnp.concatenate([jnp.exp2(sbuf[hc, r0:r1, :] - shift).astype(BF16) for r0, r1 in rows], axis=0)
        m_ref[hc] = m_new
        acc_ref[hc] = acc_ref[hc] * jnp.exp2(m_old - m_new) + _dot(vt_ref[kj, hc // 2], p)

    def step(kj, buf, const_bias, nxt):
        for hc in range(nu):
            softmax_pv_unit(kj, hc, buf, const_bias)
            qk_unit(nxt[0], hc, nxt[1], nxt[2])

    last = nkt - 1
    n_low = jnp.maximum(qi - 1, 0)
    n_far = n_low + jnp.maximum(nkt - qi - 2, 0)
    buf_a = (s0_ref, mx0_ref)
    buf_b = (s1_ref, mx1_ref)

    def far_tile(f):
        return jnp.minimum(jnp.where(f < n_low, f, f - n_low + qi + 2), last)

    def dead_if(cond):
        pen = jnp.where(cond, 2.0 * NEG, 0.0)
        return lambda h: pen

    near = [jnp.clip(qi + d - 1, 0, last) for d in range(3)]
    for hc in range(nu):
        qk_unit(near[0], hc, buf_a, 0)
    step(near[0], buf_a, dead_if(qi == 0), (near[1], buf_b, 1))
    step(near[1], buf_b, dead_if(False), (near[2], buf_a, 2))
    step(near[2], buf_a, dead_if(qi == last), (far_tile(0), buf_b, None))

    def far_step(f, cur, nxt):
        kj = far_tile(f)
        row = jnp.where(kj < qi, 0, 1)
        live = f < n_far
        step(kj, cur, lambda h: jnp.where(live, sc_ref[row, h], 2.0 * NEG), (far_tile(f + 1), nxt, None))

    far_step(0, buf_b, buf_a)

    def body(i, carry):
        f0 = 1 + DF_UNROLL * i
        for u in range(DF_UNROLL):
            cur, nxt = (buf_a, buf_b) if u % 2 == 0 else (buf_b, buf_a)
            far_step(f0 + u, cur, nxt)
        return carry
    lax.fori_loop(0, (n_far - 1 + DF_UNROLL - 1) // DF_UNROLL, body, 0)

    lmb = sc_ref[2, 0]
    outs = []
    for h in range(BR_HEADS):
        a0 = acc_ref[2 * h]
        a1 = acc_ref[2 * h + 1]
        o0 = a0[:HEAD_DIM, :] / a0[HEAD_DIM:HEAD_DIM + 1, :]
        o1 = a1[:HEAD_DIM, :] / a1[HEAD_DIM:HEAD_DIM + 1, :]
        o = o0 - lmb * o1
        ms = jnp.mean(o * o, axis=0, keepdims=True)
        outs.append(o * lax.rsqrt(ms + EPS))
    ot = jnp.concatenate(outs, axis=0) * gcol_ref[...]
    o_ref[...] = ot.T.astype(BF16)


def _diff_attn_class(off, nseq, n, qt, k3, vt4, bias, sc, gcol):
    t = DF_T
    nkt = n // t
    qb = off // t
    sb = off // n
    return pl.pallas_call(
        functools.partial(_diff_body, nkt),
        grid=(nseq, nkt),
        in_specs=[pl.BlockSpec((t, BR_W), lambda s, i: (qb + s * nkt + i, 0)),
                  pl.BlockSpec((nkt, t, BR_W), lambda s, i: (sb + s, 0, 0), pipeline_mode=pl.Buffered(1)),
                  pl.BlockSpec((nkt, BR_HEADS, DF_VROWS, t), lambda s, i: (sb + s, 0, 0, 0),
                               pipeline_mode=pl.Buffered(1)),
                  _const_spec(bias.shape),
                  pl.BlockSpec(memory_space=pltpu.SMEM),
                  _const_spec(gcol.shape)],
        out_specs=pl.BlockSpec((t, BR_W), lambda s, i: (s * nkt + i, 0)),
        out_shape=jax.ShapeDtypeStruct((nseq * n, BR_W), BF16),
        scratch_shapes=[pltpu.VMEM((2 * BR_HEADS, DF_KH, t), BF16),
                        pltpu.VMEM((2 * BR_HEADS, 1, t), F32),
                        pltpu.VMEM((2 * BR_HEADS, DF_VROWS, t), F32),
                        pltpu.VMEM((2 * BR_HEADS, t, t), F32),
                        pltpu.VMEM((2 * BR_HEADS, t, t), F32),
                        pltpu.VMEM((2 * BR_HEADS, 8, t), F32),
                        pltpu.VMEM((2 * BR_HEADS, 8, t), F32)],
        compiler_params=_params(("parallel", "parallel")),
        name="diff_attn",
    )(qt, k3, vt4, bias, sc, gcol)


def _diff_attn(layout, q, k, vt4, bias, sc, gcol):
    t = k.shape[0]
    k3 = k.reshape(t // DF_T, DF_T, BR_W)
    outs = [_diff_attn_class(off, nseq, n, q, k3, vt4, bias, sc, gcol) for off, nseq, n in layout.classes]
    return jnp.concatenate(outs, axis=0)


def _win_body(layout, q_ref, kp_ref, kc_ref, kn_ref, vp_ref, vc_ref, vn_ref, bias_ref, sink_ref, o_ref):
    blk = WIN_BLOCK
    i = pl.program_id(0)
    tok = i * (WIN_NB * blk)
    start, length = layout.seq_bounds(tok)
    masks = _head_mask()
    kcat = jnp.concatenate([kp_ref[...], kc_ref[...], kn_ref[...]], axis=0)
    vcat = jnp.concatenate([vp_ref[...], vc_ref[...], vn_ref[...]], axis=0)
    col = lax.broadcasted_iota(jnp.int32, (1, 3 * blk), 1)
    sink = sink_ref[...]
    bias = bias_ref[...]
    subs = range(WIN_NB)

    def dead_cols(j):
        has_prev = tok + j * blk > start
        has_next = tok + (j + 1) * blk < start + length
        return ((col < blk) & jnp.logical_not(has_prev)) | ((col >= 2 * blk) & jnp.logical_not(has_next))

    s = [_dot_nt(_stack_heads(q_ref[j * blk:(j + 1) * blk, :], masks), kcat[j * blk:(j + 3) * blk, :]) for j in subs]
    s = [s[j] + bias + jnp.where(dead_cols(j), NEG, 0.0) for j in subs]
    m = [jnp.maximum(jnp.max(s[j], axis=-1, keepdims=True), sink) for j in subs]
    p = [jnp.exp2(s[j] - m[j]) for j in subs]
    den = [jnp.sum(p[j], axis=-1, keepdims=True) + jnp.exp2(sink - m[j]) for j in subs]
    pn = [(p[j] * (1.0 / den[j])).astype(BF16) for j in subs]
    o_all = [_dot(pn[j], vcat[j * blk:(j + 3) * blk, :]) for j in subs]
    for j in subs:
        o_ref[j * blk:(j + 1) * blk, :] = _unstack_heads(o_all[j], masks, blk).astype(BF16)


def _win_attn(layout, q, k, v, bias, sink):
    t = q.shape[0]
    blk = WIN_BLOCK
    nb = t // blk
    cur = pl.BlockSpec((WIN_NB * blk, BR_W), lambda i: (i, 0))
    prev = pl.BlockSpec((blk, BR_W), lambda i: (jnp.maximum(i * WIN_NB - 1, 0), 0))
    nxt = pl.BlockSpec((blk, BR_W), lambda i: (jnp.minimum((i + 1) * WIN_NB, nb - 1), 0))
    return pl.pallas_call(
        functools.partial(_win_body, layout),
        grid=(nb // WIN_NB,),
        in_specs=[cur, prev, cur, nxt, prev, cur, nxt, _const_spec(bias.shape), _const_spec(sink.shape)],
        out_specs=cur,
        out_shape=jax.ShapeDtypeStruct((t, BR_W), BF16),
        compiler_params=_params(("parallel",)),
        name="win_attn",
    )(q, k, k, k, v, v, v, bias, sink)


def _na_body(rows, q_ref, k_ref, v_ref, bias_ref, o_ref):
    j = pl.program_id(1)
    masks = _head_mask()
    nk = NA_KH * GRID_W
    rr = range(NA_RB)
    r = [j * NA_RB + i for i in rr]
    rs = [jnp.clip(r[i] - NA_KH // 2, 0, rows - NA_KH) for i in rr]
    koff = [pl.multiple_of(rs[i] * GRID_W, GRID_W) for i in rr]
    s = [_dot_nt(_stack_heads(q_ref[i * GRID_W:(i + 1) * GRID_W, :], masks), k_ref[pl.ds(koff[i], nk), :])
         + bias_ref[r[i] - rs[i]] for i in rr]
    m = [jnp.max(s[i], axis=-1, keepdims=True) for i in rr]
    p = [jnp.exp2(s[i] - m[i]) for i in rr]
    pn = [(p[i] * (1.0 / jnp.sum(p[i], axis=-1, keepdims=True))).astype(BF16) for i in rr]
    o_all = [_dot(pn[i], v_ref[pl.ds(koff[i], nk), :]) for i in rr]
    for i in rr:
        o_ref[i * GRID_W:(i + 1) * GRID_W, :] = _unstack_heads(o_all[i], masks, GRID_W).astype(BF16)


def _na_class(off, nseq, n, q, k, v, bias):
    rows = n // GRID_W
    qt = NA_RB * GRID_W
    nj = n // qt
    qb = off // qt
    sb = off // n
    seq = pl.BlockSpec((n, BR_W), lambda s, j: (sb + s, 0), pipeline_mode=pl.Buffered(1))
    return pl.pallas_call(
        functools.partial(_na_body, rows),
        grid=(nseq, nj),
        in_specs=[pl.BlockSpec((qt, BR_W), lambda s, j: (qb + s * nj + j, 0)), seq, seq, _const_spec(bias.shape)],
        out_specs=pl.BlockSpec((qt, BR_W), lambda s, j: (s * nj + j, 0)),
        out_shape=jax.ShapeDtypeStruct((nseq * n, BR_W), BF16),
        compiler_params=_params(("parallel", "parallel")),
        name="na_attn",
    )(q, k, v, bias)


def _na_attn(layout, q, k, v, bias):
    return jnp.concatenate([_na_class(off, nseq, n, q, k, v, bias) for off, nseq, n in layout.classes], axis=0)


def _mem_body(q_ref, mk_ref, mv_ref, o_ref):
    masks = _head_mask()
    sub = MEM_TM // MEM_SPLIT
    parts = range(MEM_SPLIT)
    mk = mk_ref[0]
    mv = mv_ref[0]
    s = [_dot_nt(_stack_heads(q_ref[i * sub:(i + 1) * sub, :], masks), mk) for i in parts]
    m = [jnp.max(s[i], axis=-1, keepdims=True) for i in parts]
    p = [jnp.exp2(s[i] - m[i]) for i in parts]
    pn = [(p[i] * (1.0 / jnp.sum(p[i], axis=-1, keepdims=True))).astype(BF16) for i in parts]
    o_all = [_dot(pn[i], mv) for i in parts]
    for i in parts:
        o_ref[i * sub:(i + 1) * sub, :] = _unstack_heads(o_all[i], masks, sub).astype(BF16)


def _mem_attn(layout, q, mk, mv):
    t = q.shape[0]
    tm = MEM_TM
    mem_len = mk.shape[1]
    tile = pl.BlockSpec((tm, BR_W), lambda i: (i, 0))
    mem = pl.BlockSpec((1, mem_len, BR_W), lambda i: (layout.seq_index(i * tm), 0, 0))
    return pl.pallas_call(
        _mem_body,
        grid=(t // tm,),
        in_specs=[tile, mem, mem],
        out_specs=tile,
        out_shape=jax.ShapeDtypeStruct((t, BR_W), BF16),
        compiler_params=_params(("parallel",)),
        name="mem_attn",
    )(q, mk, mv)


def _merge_body(x_ref, ng_ref, oa_ref, ob_ref, oc_ref, od_ref, oe_ref, wg_ref, wm_ref, wb_ref, wo_ref, y_ref):
    x = x_ref[...]
    h = _rms_rows(x, ng_ref[...]).astype(BF16)
    merged = jnp.zeros((x.shape[0], D_MODEL), F32)
    for kb, o_ref in enumerate((oa_ref, ob_ref, oc_ref, od_ref, oe_ref)):
        g = _dot(h, wg_ref[:, kb * BR_W:(kb + 1) * BR_W])
        br = (o_ref[...].astype(F32) * (g * jax.nn.sigmoid(g))).astype(BF16)
        mg = jax.nn.sigmoid(_dot(h, wm_ref[:, kb * D_MODEL:(kb + 1) * D_MODEL]))
        merged = merged + mg * _dot(br, wb_ref[kb])
    y_ref[...] = x + _dot(merged.astype(BF16), wo_ref[...])


def _merge(x, ng, branches, wg, wm, wb, wo):
    t = x.shape[0]
    tm = MERGE_TM
    xt = pl.BlockSpec((tm, D_MODEL), lambda i: (i, 0))
    bt = pl.BlockSpec((tm, BR_W), lambda i: (i, 0))
    return pl.pallas_call(
        _merge_body,
        grid=(t // tm,),
        in_specs=[xt, _const_spec((1, D_MODEL))] + [bt] * N_BRANCH
        + [_const_spec(wg.shape), _const_spec(wm.shape), _const_spec(wb.shape), _const_spec(wo.shape)],
        out_specs=xt,
        out_shape=jax.ShapeDtypeStruct((t, D_MODEL), F32),
        input_output_aliases={0: 0},
        compiler_params=_params(("parallel",)),
        name="merge",
    )(x, ng, *branches, wg, wm, wb, wo)


def _t5_bucket(rel):
    half = N_BUCKETS // 2
    exact = half // 2
    n = jnp.abs(rel)
    nf = jnp.maximum(n, 1).astype(F32)
    large = exact + (jnp.log(nf / exact) / math.log(MAX_DIST / exact) * (half - exact)).astype(jnp.int32)
    large = jnp.clip(large, 0, half - 1)
    return jnp.where(rel > 0, half, 0) + jnp.where(n < exact, n, large)


def _group_matrix(group):
    return jnp.asarray(np.kron(np.eye(BR_W // group), np.full((group, group), 1.0 / group)), BF16)


def _lookup(table, idx):
    onehot = (idx[..., None] == jnp.arange(table.shape[0])).astype(F32)
    return jnp.dot(onehot, table, precision=lax.Precision.HIGHEST)


def _diff_bias_tables(rel_bias):
    t = DF_T
    table = rel_bias[:, :BR_HEADS].astype(F32) * LOG2E
    kl = jnp.arange(t)[:, None]
    ql = jnp.arange(t)[None, :]
    rel = jnp.stack([kl - ql + d * t for d in (-1, 0, 1)])
    tiles = _lookup(table, _t5_bucket(rel)).transpose(0, 3, 1, 2)
    far = _lookup(table, _t5_bucket(jnp.asarray([-2 * t, 2 * t], jnp.int32)))
    return tiles, far


def _win_bias_table(rel_bias):
    rel = jnp.arange(3 * WIN_BLOCK)[None, :] - WIN_BLOCK - jnp.arange(WIN_BLOCK)[:, None]
    bias = _lookup(rel_bias[:, BR_HEADS:].astype(F32) * LOG2E, _t5_bucket(rel)).transpose(2, 0, 1)
    bias = jnp.where((jnp.abs(rel) <= WIN)[None], bias, NEG)
    return bias.reshape(BR_HEADS * WIN_BLOCK, 3 * WIN_BLOCK)


def _na_bias_tables(rpb):
    depth = rpb.shape[0]
    col = np.arange(GRID_W)
    cs = np.clip(col - NA_KW // 2, 0, GRID_W - NA_KW)
    inwin = (col[None, :] >= cs[:, None]) & (col[None, :] < cs[:, None] + NA_KW)
    dc = col[None, :] - col[:, None] + (NA_KW - 1)
    onehot = jnp.asarray(dc[None] == np.arange(2 * NA_KW - 1)[:, None, None], F32)
    toep = jnp.einsum("lhrc,cqk->lhrqk", rpb.astype(F32) * LOG2E, onehot, precision=lax.Precision.HIGHEST)
    toep = jnp.where(jnp.asarray(inwin)[None, None, None], toep, NEG)
    out = []
    for d in range(NA_KH):
        lo = NA_KH - 1 - d
        b = toep[:, :, lo:lo + NA_KH].transpose(0, 1, 3, 2, 4)
        out.append(b.reshape(depth, BR_HEADS * GRID_W, NA_KH * GRID_W))
    return jnp.stack(out, axis=1)


def _tile_gain(g, reps, scale=1.0):
    return jnp.tile(g.astype(F32), reps)[None, :] * scale


def kernel(x_prompt, x_sample, mem_prompt, mem_sample, norm_g, mem_norm_g, w_in, w_mem_kv, rel_bias, hgrn_lb,
           hgrn_norm_g, diff_qk_g, diff_lambda, diff_subln_g, win_qk_g, win_sink, na_qk_g, na_rpb, mem_qk_g,
           w_branch, w_out):
    depth = w_in.shape[0]
    bp, lp, _ = x_prompt.shape
    bs, ls, _ = x_sample.shape
    mem_len = mem_prompt.shape[1]
    layout = _Layout(bp, lp, bs, ls)
    t = layout.total
    assert t % PROJ_TM == 0 and t % MERGE_TM == 0 and lp % DF_T == 0 and ls % DF_T == 0
    assert lp % (NA_RB * GRID_W) == 0 and ls % (NA_RB * GRID_W) == 0 and lp % MEM_TM == 0 and ls % MEM_TM == 0
    assert lp % (WIN_NB * WIN_BLOCK) == 0 and ls % (WIN_NB * WIN_BLOCK) == 0

    x = jnp.concatenate([x_prompt.reshape(bp * lp, D_MODEL), x_sample.reshape(bs * ls, D_MODEL)], axis=0)
    mem = jnp.concatenate([mem_prompt.reshape(bp * mem_len, D_MODEL), mem_sample.reshape(bs * mem_len, D_MODEL)], axis=0)

    sm = jax.nn.softmax(hgrn_lb.astype(F32), axis=1)
    lb_all = jnp.clip(jnp.cumsum(sm, axis=1) - sm[:, :1], 0.0, 1.0 - 1e-6)
    lam_init = jnp.asarray([0.8 - 0.6 * math.exp(-0.3 * l) for l in range(depth)], F32)
    lam = diff_lambda.astype(F32)
    lmb = jnp.exp(jnp.sum(lam[:, 0] * lam[:, 1], axis=-1)) - jnp.exp(jnp.sum(lam[:, 2] * lam[:, 3], axis=-1)) + lam_init

    w_in_b = w_in.astype(BF16)

    def expand_kv(w):
        w = w.reshape(depth, D_MODEL, WIN_KV_HEADS, HEAD_DIM)
        return jnp.repeat(w, BR_HEADS // WIN_KV_HEADS, axis=2).reshape(depth, D_MODEL, BR_W)

    w1 = jnp.concatenate([w_in_b[:, :, A_Q:A_G], w_in_b[:, :, B_Q:B_G], w_in_b[:, :, C_Q:C_K],
                          expand_kv(w_in_b[:, :, C_K:C_V]), expand_kv(w_in_b[:, :, C_V:C_G]),
                          w_in_b[:, :, D_Q:D_G], w_in_b[:, :, E_Q:E_G]], axis=-1)
    wvt = w_in_b[:, :, B_Q + 2 * BR_W:B_G].transpose(0, 2, 1)
    wg = jnp.concatenate([w_in_b[:, :, A_G:B_Q], w_in_b[:, :, B_G:C_Q], w_in_b[:, :, C_G:D_Q],
                          w_in_b[:, :, D_G:E_Q], w_in_b[:, :, E_G:M_G]], axis=-1)
    wm = w_in_b[:, :, M_G:]
    wb = w_branch.astype(BF16)
    wo = w_out.astype(BF16)
    wmem = w_mem_kv.astype(BF16)

    sc_b = DF_DK ** -0.5 * LOG2E
    sc_h = HEAD_DIM ** -0.5 * LOG2E
    zrow = jnp.zeros((depth, 1, BR_W), F32)

    def per_layer(fn):
        return jnp.stack([fn(l) for l in range(depth)])

    ep = jnp.concatenate([
        per_layer(lambda l: _tile_gain(diff_qk_g[l, 0], 8, sc_b)),
        per_layer(lambda l: _tile_gain(diff_qk_g[l, 1], 8)),
        per_layer(lambda l: _tile_gain(win_qk_g[l, 0], 4, sc_h)),
        per_layer(lambda l: _tile_gain(win_qk_g[l, 1], 4)),
        per_layer(lambda l: _tile_gain(na_qk_g[l, 0], 4, sc_h)),
        per_layer(lambda l: _tile_gain(na_qk_g[l, 1], 4)),
        per_layer(lambda l: _tile_gain(mem_qk_g[l, 0], 4, sc_h)),
        lb_all[0][:, None, :], lb_all[1][:, None, :]] + [zrow] * 7, axis=1)
    gk_mem = per_layer(lambda l: _tile_gain(mem_qk_g[l, 1], 4))
    gn_hg = hgrn_norm_g.astype(F32)[:, None, :]
    gcol = per_layer(lambda l: (jnp.tile(diff_subln_g[l].astype(F32), BR_HEADS) * (1.0 - lam_init[l]))[:, None])
    sink = per_layer(lambda l: jnp.repeat(win_sink[l].astype(F32) * LOG2E, WIN_BLOCK)[:, None])
    na_bias = _na_bias_tables(na_rpb)

    diff_bias, diff_far = _diff_bias_tables(rel_bias)
    win_bias = _win_bias_table(rel_bias)
    sc = jnp.concatenate([jnp.broadcast_to(diff_far[None], (depth, 2, BR_HEADS)),
                          jnp.broadcast_to(lmb[:, None, None], (depth, 1, BR_HEADS))], axis=1)

    g32 = _group_matrix(DF_DK)
    g64 = _group_matrix(HEAD_DIM)
    consts_f = tuple(jnp.asarray(a, dt) for a, dt in zip(_hgrn_constants(HG_C, False), (BF16, F32, F32)))
    consts_b = tuple(jnp.asarray(a, dt) for a, dt in zip(_hgrn_constants(HG_C, True), (BF16, F32, F32)))

    def layer(x, p):
        ng = p["ng"]
        (a_q, a_v, a_lf, a_kk, b_q, b_k, b_v, c_q, c_k, c_v, d_q, d_k, d_v, e_q) = _project(
            x, ng, p["w1"], p["wvt"], p["ep"], g32, g64)
        mk, mv = _mem_kv(mem, p["mng"], p["wmem"], p["gk_mem"], g64)
        o_a = _hgrn(layout, a_q, a_v, a_kk, a_lf, consts_f, consts_b, g64, p["gn_hg"])
        o_b = _diff_attn(layout, b_q, b_k, b_v, diff_bias, p["sc"], p["gcol"])
        o_c = _win_attn(layout, c_q, c_k, c_v, win_bias, p["sink"])
        o_d = _na_attn(layout, d_q, d_k, d_v, p["na_bias"])
        o_e = _mem_attn(layout, e_q, mk.reshape(layout.nseq, mem_len, BR_W), mv.reshape(layout.nseq, mem_len, BR_W))
        y = _merge(x, ng, (o_a, o_b, o_c, o_d, o_e), p["wg"], p["wm"], p["wb"], p["wo"])
        return y, None

    params = dict(ng=norm_g.astype(F32)[:, None, :], mng=mem_norm_g.astype(F32)[:, None, :], w1=w1, wvt=wvt, ep=ep,
                  wmem=wmem, gk_mem=gk_mem, gn_hg=gn_hg, sc=sc, gcol=gcol, sink=sink, na_bias=na_bias,
                  wg=wg, wm=wm, wb=wb, wo=wo)
    x, _ = lax.scan(layer, x, params)
    y_prompt = x[:layout.off1].reshape(bp, lp, D_MODEL)
    y_sample = x[layout.off1:].reshape(bs, ls, D_MODEL)
    return (y_prompt, y_sample)
```

```python
import functools
import math

import numpy as np
import jax
import jax.numpy as jnp
from jax import lax
from jax.experimental import pallas as pl
from jax.experimental.pallas import tpu as pltpu

F32 = jnp.float32
BF16 = jnp.bfloat16

D_MODEL = 1024
HEAD_DIM = 64
BR_HEADS = 4
BR_W = BR_HEADS * HEAD_DIM
N_BRANCH = 5
DF_DK = HEAD_DIM // 2
WIN = 128
WIN_BLOCK = 128
WIN_KV_HEADS = 2
GRID_W = 64
NA_KH = 8
NA_KW = 16
N_BUCKETS = 32
MAX_DIST = 128
EPS = 1e-6
NEG = -1e30
LB_FLOOR = 1e-30
LOG2E = 1.4426950408889634

A_Q = 0
A_G = 4 * BR_W
B_Q = A_G + BR_W
B_G = B_Q + 3 * BR_W
C_Q = B_G + BR_W
C_K = C_Q + BR_W
C_V = C_K + WIN_KV_HEADS * HEAD_DIM
C_G = C_V + WIN_KV_HEADS * HEAD_DIM
D_Q = C_G + BR_W
D_G = D_Q + 3 * BR_W
E_Q = D_G + BR_W
E_G = E_Q + BR_W
M_G = E_G + BR_W

VMEM_LIMIT_BYTES = 56 * 1024 * 1024

PROJ_TM = 1024
MERGE_TM = 512
MEM_TM = 512
MEM_SPLIT = 4
HG_C = 128
HG_MAT_LEVELS = 2
DF_T = 256
NA_RB = 8
WIN_NB = 4
HG_NORM_TM = 2048
DF_VROWS = 80
DF_RC = 64
DF_KH = 128
DF_UNROLL = 4


def _params(sem):
    return pltpu.CompilerParams(dimension_semantics=sem, vmem_limit_bytes=VMEM_LIMIT_BYTES)


def _const_spec(shape):
    nd = len(shape)
    return pl.BlockSpec(shape, lambda *_: (0,) * nd, pipeline_mode=pl.Buffered(1))


def _dot(a, b):
    return jnp.dot(a, b, preferred_element_type=F32)


def _dot_nt(a, b):
    return lax.dot_general(a, b, (((1,), (1,)), ((), ())), preferred_element_type=F32)


def _rms_rows(x, g):
    ms = jnp.mean(x * x, axis=-1, keepdims=True)
    return x * lax.rsqrt(ms + EPS) * g


def _group_rms(x, gmat):
    x2 = x * x
    hi = x2.astype(BF16)
    lo = (x2 - hi.astype(F32)).astype(BF16)
    ms = _dot(hi, gmat) + _dot(lo, gmat)
    return x * lax.rsqrt(ms + EPS)


def _head_mask(width=BR_W):
    lane = lax.broadcasted_iota(jnp.int32, (1, width), 1)
    return [(lane >= h * HEAD_DIM) & (lane < (h + 1) * HEAD_DIM) for h in range(BR_HEADS)]


def _stack_heads(q, masks):
    zero = jnp.zeros_like(q)
    return jnp.concatenate([jnp.where(m, q, zero) for m in masks], axis=0)


def _unstack_heads(o_all, masks, m):
    out = jnp.zeros((m, BR_W), F32)
    for h, mk in enumerate(masks):
        out = out + jnp.where(mk, o_all[h * m:(h + 1) * m, :], 0.0)
    return out


class _Layout:
    def __init__(self, n_prompt, len_prompt, n_sample, len_sample):
        self.classes = ((0, n_prompt, len_prompt), (n_prompt * len_prompt, n_sample, len_sample))
        self.off1 = n_prompt * len_prompt
        self.lp = len_prompt
        self.ls = len_sample
        self.total = self.off1 + n_sample * len_sample
        self.nseq = n_prompt + n_sample
        self.n_prompt = n_prompt
        assert self.off1 % len_sample == 0

    def seq_bounds(self, tok):
        in_p = tok < self.off1
        start_p = (tok // self.lp) * self.lp
        start_s = self.off1 + ((tok - self.off1) // self.ls) * self.ls
        return jnp.where(in_p, start_p, start_s), jnp.where(in_p, self.lp, self.ls)

    def seq_index(self, tok):
        return jnp.where(tok < self.off1, tok // self.lp, self.n_prompt + (tok - self.off1) // self.ls)


def _proj_body(x_ref, ng_ref, w_ref, wvt_ref, ep_ref, g32_ref, g64_ref,
               aq_ref, av_ref, alf_ref, akk_ref, bq_ref, bk_ref, bv_ref,
               cq_ref, ck_ref, cv_ref, dq_ref, dk_ref, dv_ref, eq_ref):
    h = _rms_rows(x_ref[...], ng_ref[...]).astype(BF16)
    ep = ep_ref[...]

    def seg(i):
        return _dot(h, w_ref[:, i * BR_W:(i + 1) * BR_W])

    def row(r):
        return ep[r:r + 1, :]

    def silu_out(ref):
        def fin(a):
            ref[...] = (a * jax.nn.sigmoid(a)).astype(BF16)
        return fin

    def plain_out(ref):
        def fin(a):
            ref[...] = a.astype(BF16)
        return fin

    def normed_out(ref, g_ref, gain_row):
        def fin(a):
            ref[...] = (_group_rms(a, g_ref[...]) * row(gain_row)).astype(BF16)
        return fin

    def decay_out(d):
        def fin(z):
            lb = row(7 + d)
            e = jnp.exp(-jnp.abs(z))
            log_sig = jnp.minimum(z, 0.0) - jnp.log1p(e)
            t1 = jnp.log(jnp.maximum(lb, LB_FLOOR))
            t2 = jnp.log1p(-lb) + log_sig
            logf = jnp.maximum(t1, t2) + jnp.log1p(jnp.exp(-jnp.abs(t1 - t2)))
            sig_neg = jnp.where(z >= 0.0, e, 1.0) / (1.0 + e)
            alf_ref[:, d * BR_W:(d + 1) * BR_W] = logf
            akk_ref[:, d * BR_W:(d + 1) * BR_W] = ((1.0 - lb) * sig_neg).astype(BF16)
        return fin

    def vt_out(vt):
        vt = vt.astype(BF16)
        pad_row = lax.broadcasted_iota(jnp.int32, (DF_VROWS - HEAD_DIM, DF_T), 0)
        pad = jnp.where(pad_row == 0, 1.0, 0.0).astype(BF16)
        for j in range(PROJ_TM // DF_T):
            for hd in range(BR_HEADS):
                bv_ref[j, hd, 0:HEAD_DIM, :] = vt[hd * HEAD_DIM:(hd + 1) * HEAD_DIM, j * DF_T:(j + 1) * DF_T]
                bv_ref[j, hd, HEAD_DIM:DF_VROWS, :] = pad

    tasks = [
        (lambda: seg(0), silu_out(aq_ref)),
        (lambda: seg(1), plain_out(av_ref)),
        (lambda: seg(2), decay_out(0)),
        (lambda: seg(3), decay_out(1)),
        (lambda: seg(4), normed_out(bq_ref, g32_ref, 0)),
        (lambda: seg(5), normed_out(bk_ref, g32_ref, 1)),
        (lambda: _dot_nt(wvt_ref[...], h), vt_out),
        (lambda: seg(7), normed_out(cq_ref, g64_ref, 2)),
        (lambda: seg(8), normed_out(ck_ref, g64_ref, 3)),
        (lambda: seg(9), plain_out(cv_ref)),
        (lambda: seg(10), normed_out(dq_ref, g64_ref, 4)),
        (lambda: seg(11), normed_out(dk_ref, g64_ref, 5)),
        (lambda: seg(12), plain_out(dv_ref)),
        (lambda: seg(13), normed_out(eq_ref, g64_ref, 6)),
    ]
    cur = tasks[0][0]()
    for i, (_, fin) in enumerate(tasks):
        nxt = tasks[i + 1][0]() if i + 1 < len(tasks) else None
        fin(cur)
        cur = nxt


def _project(x, ng, w1, wvt, ep, g32, g64):
    t = x.shape[0]
    tm = PROJ_TM
    tile = lambda w: pl.BlockSpec((tm, w), lambda i: (i, 0))
    widths = [BR_W, BR_W, 2 * BR_W, 2 * BR_W] + [BR_W] * 10
    dtypes = [BF16, BF16, F32, BF16] + [BF16] * 10
    out_specs = [tile(w) for w in widths]
    out_shape = [jax.ShapeDtypeStruct((t, w), dt) for w, dt in zip(widths, dtypes)]
    kt = tm // DF_T
    out_specs[6] = pl.BlockSpec((kt, BR_HEADS, DF_VROWS, DF_T), lambda i: (i, 0, 0, 0))
    out_shape[6] = jax.ShapeDtypeStruct((t // DF_T, BR_HEADS, DF_VROWS, DF_T), BF16)
    return pl.pallas_call(
        _proj_body,
        grid=(t // tm,),
        in_specs=[tile(D_MODEL), _const_spec((1, D_MODEL)), _const_spec(w1.shape), _const_spec(wvt.shape),
                  _const_spec(ep.shape), _const_spec(g32.shape), _const_spec(g64.shape)],
        out_specs=out_specs,
        out_shape=out_shape,
        compiler_params=_params(("parallel",)),
        name="proj",
    )(x, ng, w1, wvt, ep, g32, g64)


def _memkv_body(m_ref, g_ref, w_ref, gk_ref, g64_ref, mk_ref, mv_ref):
    mh = _rms_rows(m_ref[...], g_ref[...]).astype(BF16)
    kv = _dot(mh, w_ref[...])
    mk_ref[...] = (_group_rms(kv[:, :BR_W], g64_ref[...]) * gk_ref[...]).astype(BF16)
    mv_ref[...] = kv[:, BR_W:].astype(BF16)


def _mem_kv(mem, g, w, gk, g64):
    t = mem.shape[0]
    tm = 256
    return pl.pallas_call(
        _memkv_body,
        grid=(t // tm,),
        in_specs=[pl.BlockSpec((tm, D_MODEL), lambda i: (i, 0)), _const_spec((1, D_MODEL)), _const_spec(w.shape),
                  _const_spec((1, BR_W)), _const_spec(g64.shape)],
        out_specs=[pl.BlockSpec((tm, BR_W), lambda i: (i, 0))] * 2,
        out_shape=[jax.ShapeDtypeStruct((t, BR_W), BF16)] * 2,
        compiler_params=_params(("parallel",)),
        name="memkv",
    )(mem, g, w, gk, g64)


def _hgrn_constants(c, reverse):
    nl = int(math.log2(c))
    idx = np.arange(c)
    t = idx[:, None]
    u = idx[None, :]
    incl = (u <= t).astype(np.float32)
    tot = np.ones((8, c), np.float32)
    mds, mes, lms = [], [], [np.eye(c, dtype=np.float32)]
    for lev in range(nl):
        w = 1 << lev
        blk = idx // w
        odd = (blk % 2 == 1)
        md = (odd[:, None] & (u >= (blk * w)[:, None]) & (u <= t)).astype(np.float32)
        me = ((~odd)[:, None] & (u > t) & (u <= ((blk + 1) * w - 1)[:, None])).astype(np.float32)
        lm = (odd[:, None] & (blk[None, :] == (blk - 1)[:, None])).astype(np.float32)
        mds.append(md)
        mes.append(me)
        lms.append(lm)
    mats = [incl, tot] + [md + me for md, me in zip(mds[:HG_MAT_LEVELS], mes[:HG_MAT_LEVELS])]
    if reverse:
        mats = [m[::-1, ::-1] for m in mats]
        lms = [m[::-1, ::-1] for m in lms]
    mall = np.concatenate(mats, axis=0)
    lmst = np.stack([np.tile(m, (1, BR_HEADS)) for m in lms])
    bd = np.kron(np.eye(BR_HEADS, dtype=np.float32), np.ones((HEAD_DIM, HEAD_DIM), np.float32))
    return mall, lmst, bd


def _hgrn_chunk(layout, chunk, reverse, q_ref, k_ref, v_ref, lf_ref, mall_ref, lm_ref, bd_ref, o_ref, st_ref):
    c = HG_C
    nl = int(math.log2(c))
    tok = chunk * c
    start, length = layout.seq_bounds(tok)
    fresh = (tok + c == start + length) if reverse else (tok == start)

    masks = _head_mask()
    lf = lf_ref[...]
    hi = lf.astype(BF16)
    mid = (lf - hi.astype(F32)).astype(BF16)
    mall = mall_ref[...]
    cums = _dot(mall, hi) + _dot(mall, mid)
    b = cums[0:c, :]
    b_tot = cums[c:c + 1, :]
    b_rest = b_tot - b
    base = c + 8

    def level_decay(lev):
        if lev < HG_MAT_LEVELS:
            return cums[base + lev * c:base + (lev + 1) * c, :]
        w = 1 << lev
        b3 = b.reshape(c // (2 * w), 2 * w, BR_W)
        r = w if reverse else w - 1
        d3 = b3 - b3[:, r:r + 1, :]
        second = lax.broadcasted_iota(jnp.int32, (1, 2 * w, 1), 1) >= w
        query_half = jnp.logical_not(second) if reverse else second
        return jnp.where(query_half, d3, -d3).reshape(c, BR_W)

    qf = q_ref[...].astype(F32)
    kf = k_ref[...].astype(F32)
    v = v_ref[...]
    st = jnp.where(fresh, 0.0, st_ref[...])

    o = _dot_nt((qf * jnp.exp(b)).astype(BF16), st.astype(BF16))
    dim_row = lax.broadcasted_iota(jnp.int32, (BR_W, 1), 0)
    head_rows = [(dim_row >= h * HEAD_DIM) & (dim_row < (h + 1) * HEAD_DIM) for h in range(BR_HEADS)]

    def scores(ql, kl):
        kt = kl.T.astype(BF16)
        rhs = jnp.concatenate([jnp.where(hr, kt, jnp.zeros_like(kt)) for hr in head_rows], axis=1)
        return _dot(ql, rhs)

    a_all = scores(q_ref[...], kf) * lm_ref[0]
    for lev in range(nl):
        dec = jnp.exp(level_decay(lev))
        a_all = a_all + scores((qf * dec).astype(BF16), kf * dec) * lm_ref[lev + 1]
    a_bf = a_all.astype(BF16)
    zero = jnp.zeros_like(v)
    for h, mk in enumerate(masks):
        o = o + _dot(a_bf[:, h * c:(h + 1) * c], jnp.where(mk, v, zero))
    o_ref[...] = o

    kst = (kf * jnp.exp(b_rest)).astype(BF16)
    vt = v.astype(F32).T.astype(BF16)
    st_ref[...] = (st * jnp.exp(b_tot) + _dot(vt, kst)) * bd_ref[...]


def _hgrn_body(layout, nct, qf_ref, kf_ref, vf_ref, lff_ref, qb_ref, kb_ref, vb_ref, lfb_ref,
               mallf_ref, lmf_ref, mallb_ref, lmb_ref, bd_ref, of_ref, ob_ref, stf_ref, stb_ref):
    i = pl.program_id(0)
    _hgrn_chunk(layout, i, False, qf_ref, kf_ref, vf_ref, lff_ref, mallf_ref, lmf_ref, bd_ref, of_ref, stf_ref)
    _hgrn_chunk(layout, nct - 1 - i, True, qb_ref, kb_ref, vb_ref, lfb_ref, mallb_ref, lmb_ref, bd_ref,
                ob_ref, stb_ref)


def _hgrn_norm_body(of_ref, ob_ref, g64_ref, gn_ref, o_ref):
    o_ref[...] = (_group_rms(of_ref[...] + ob_ref[...], g64_ref[...]) * gn_ref[...]).astype(BF16)


def _hgrn(layout, qs, v, kk, logf, consts_f, consts_b, g64, gn):
    t = qs.shape[0]
    c = HG_C
    nct = t // c
    mall_f, lm_f, bd = consts_f
    mall_b, lm_b, _ = consts_b

    def specs(reverse):
        cm = (lambda i: nct - 1 - i) if reverse else (lambda i: i)
        d = 1 if reverse else 0
        tile = pl.BlockSpec((c, BR_W), lambda i: (cm(i), 0))
        half = pl.BlockSpec((c, BR_W), lambda i: (cm(i), d))
        return tile, half

    tile_f, half_f = specs(False)
    tile_b, half_b = specs(True)
    o_f, o_b = pl.pallas_call(
        functools.partial(_hgrn_body, layout, nct),
        grid=(nct,),
        in_specs=[tile_f, half_f, tile_f, half_f, tile_b, half_b, tile_b, half_b,
                  _const_spec(mall_f.shape), _const_spec(lm_f.shape), _const_spec(mall_b.shape),
                  _const_spec(lm_b.shape), _const_spec(bd.shape)],
        out_specs=[tile_f, tile_b],
        out_shape=[jax.ShapeDtypeStruct((t, BR_W), F32)] * 2,
        scratch_shapes=[pltpu.VMEM((BR_W, BR_W), F32)] * 2,
        compiler_params=_params(("arbitrary",)),
        name="hgrn_scan",
    )(qs, kk, v, logf, qs, kk, v, logf, mall_f, lm_f, mall_b, lm_b, bd)

    tm = math.gcd(t, HG_NORM_TM)
    tok = pl.BlockSpec((tm, BR_W), lambda i: (i, 0))
    return pl.pallas_call(
        _hgrn_norm_body,
        grid=(t // tm,),
        in_specs=[tok, tok, _const_spec(g64.shape), _const_spec((1, BR_W))],
        out_specs=tok,
        out_shape=jax.ShapeDtypeStruct((t, BR_W), BF16),
        compiler_params=_params(("parallel",)),
        name="hgrn_norm",
    )(o_f, o_b, g64, gn)


def _diff_body(nkt, q_ref, k_ref, vt_ref, bias_ref, sc_ref, gcol_ref, o_ref, qm_ref, m_ref, acc_ref,
               s0_ref, s1_ref, s2_ref, s3_ref, mx0_ref, mx1_ref, mx2_ref, mx3_ref):
    tq = DF_T
    qi = pl.program_id(1)
    qt = q_ref[...].astype(F32).T.astype(BF16)
    rowid = lax.broadcasted_iota(jnp.int32, (DF_KH, tq), 0)
    for hc in range(2 * BR_HEADS):
        half, sub = divmod(hc * DF_DK, DF_KH)
        sel = (rowid >= sub) & (rowid < sub + DF_DK)
        qh = qt[half * DF_KH:(half + 1) * DF_KH, :]
        qm_ref[hc] = jnp.where(sel, qh, jnp.zeros_like(qh))
    m_ref[...] = jnp.full(m_ref.shape, NEG, F32)
    acc_ref[...] = jnp.zeros_like(acc_ref)

    nu = 2 * BR_HEADS
    rows = [(r, r + DF_RC) for r in range(0, tq, DF_RC)]

    def qk_unit(kj, hc, buf, near=None):
        sbuf, mxbuf = buf
        half = hc * DF_DK // DF_KH
        s = _dot(k_ref[kj, :, half * DF_KH:(half + 1) * DF_KH], qm_ref[hc])
        if near is not None:
            s = s + bias_ref[near, hc // 2]
        sbuf[hc] = s
        mxbuf[hc] = jnp.max(s.reshape(tq // 8, 8, tq), axis=0)

    def softmax_pv_pair(kjs, hc, bufs, consts):
        cs = [const(hc // 2) for const in consts]
        m_old = m_ref[hc]
        m_new = m_old
        for (_, mxbuf), c in zip(bufs, cs):
            m_new = jnp.maximum(m_new, jnp.max(mxbuf[hc], axis=0, keepdims=True) + c)
        acc = acc_ref[hc] * jnp.exp2(m_old - m_new)
        for kj, (sbuf, _), c in zip(kjs, bufs, cs):
            shift = m_new - c
            p = jnp.concatenate([jnp.exp2(sbuf[hc, r0:r1, :] - shift).astype(BF16) for r0, r1 in rows], axis=0)
            acc = acc + _dot(vt_ref[kj, hc // 2], p)
        m_ref[hc] = m_new
        acc_ref[hc] = acc

    def step(tiles, bufs, nxts):
        for hc in range(nu):
            softmax_pv_pair([t[0] for t in tiles], hc, bufs, [t[1] for t in tiles])
            for kj, buf, near_idx in nxts:
                qk_unit(kj, hc, buf, near_idx)

    last = nkt - 1
    n_low = jnp.maximum(qi - 1, 0)
    n_far = n_low + jnp.maximum(nkt - qi - 2, 0)
    bufs_a = ((s0_ref, mx0_ref), (s1_ref, mx1_ref))
    bufs_b = ((s2_ref, mx2_ref), (s3_ref, mx3_ref))

    def far_tile(f):
        return jnp.minimum(jnp.where(f < n_low, f, f - n_low + qi + 2), last)

    def far(f):
        kj = far_tile(f)
        row = jnp.where(kj < qi, 0, 1)
        live = f < n_far
        return kj, lambda h: jnp.where(live, sc_ref[row, h], 2.0 * NEG)

    def dead_if(cond):
        pen = jnp.where(cond, 2.0 * NEG, 0.0)
        return lambda h: pen

    near = [jnp.clip(qi + d - 1, 0, last) for d in range(3)]
    for hc in range(nu):
        qk_unit(near[0], hc, bufs_a[0], 0)
        qk_unit(near[1], hc, bufs_a[1], 1)
    step([(near[0], dead_if(qi == 0)), (near[1], dead_if(False))], bufs_a,
         [(near[2], bufs_b[0], 2), (far_tile(0), bufs_b[1], None)])
    step([(near[2], dead_if(qi == last)), far(0)], bufs_b,
         [(far_tile(1), bufs_a[0], None), (far_tile(2), bufs_a[1], None)])

    def body(i, carry):
        f0 = 1 + DF_UNROLL * i
        for u in range(0, DF_UNROLL, 2):
            cur, nxt = (bufs_a, bufs_b) if u % 4 == 0 else (bufs_b, bufs_a)
            step([far(f0 + u), far(f0 + u + 1)], cur,
                 [(far_tile(f0 + u + 2), nxt[0], None), (far_tile(f0 + u + 3), nxt[1], None)])
        return carry
    lax.fori_loop(0, (n_far - 1 + DF_UNROLL - 1) // DF_UNROLL, body, 0)

    lmb = sc_ref[2, 0]
    outs = []
    for h in range(BR_HEADS):
        a0 = acc_ref[2 * h]
        a1 = acc_ref[2 * h + 1]
        o0 = a0[:HEAD_DIM, :] / a0[HEAD_DIM:HEAD_DIM + 1, :]
        o1 = a1[:HEAD_DIM, :] / a1[HEAD_DIM:HEAD_DIM + 1, :]
        o = o0 - lmb * o1
        ms = jnp.mean(o * o, axis=0, keepdims=True)
        outs.append(o * lax.rsqrt(ms + EPS))
    ot = jnp.concatenate(outs, axis=0) * gcol_ref[...]
    o_ref[...] = ot.T.astype(BF16)


def _diff_attn_class(off, nseq, n, qt, k3, vt4, bias, sc, gcol):
    t = DF_T
    nkt = n // t
    qb = off // t
    sb = off // n
    return pl.pallas_call(
        functools.partial(_diff_body, nkt),
        grid=(nseq, nkt),
        in_specs=[pl.BlockSpec((t, BR_W), lambda s, i: (qb + s * nkt + i, 0)),
                  pl.BlockSpec((nkt, t, BR_W), lambda s, i: (sb + s, 0, 0), pipeline_mode=pl.Buffered(1)),
                  pl.BlockSpec((nkt, BR_HEADS, DF_VROWS, t), lambda s, i: (sb + s, 0, 0, 0),
                               pipeline_mode=pl.Buffered(1)),
                  _const_spec(bias.shape),
                  pl.BlockSpec(memory_space=pltpu.SMEM),
                  _const_spec(gcol.shape)],
        out_specs=pl.BlockSpec((t, BR_W), lambda s, i: (s * nkt + i, 0)),
        out_shape=jax.ShapeDtypeStruct((nseq * n, BR_W), BF16),
        scratch_shapes=[pltpu.VMEM((2 * BR_HEADS, DF_KH, t), BF16),
                        pltpu.VMEM((2 * BR_HEADS, 1, t), F32),
                        pltpu.VMEM((2 * BR_HEADS, DF_VROWS, t), F32),
                        ] + [pltpu.VMEM((2 * BR_HEADS, t, t), F32)] * 4
        + [pltpu.VMEM((2 * BR_HEADS, 8, t), F32)] * 4,
        compiler_params=_params(("parallel", "parallel")),
        name="diff_attn",
    )(qt, k3, vt4, bias, sc, gcol)


def _diff_attn(layout, q, k, vt4, bias, sc, gcol):
    t = k.shape[0]
    k3 = k.reshape(t // DF_T, DF_T, BR_W)
    outs = [_diff_attn_class(off, nseq, n, q, k3, vt4, bias, sc, gcol) for off, nseq, n in layout.classes]
    return jnp.concatenate(outs, axis=0)


def _win_body(layout, q_ref, kp_ref, kc_ref, kn_ref, vp_ref, vc_ref, vn_ref, bias_ref, sink_ref, o_ref):
    blk = WIN_BLOCK
    i = pl.program_id(0)
    tok = i * (WIN_NB * blk)
    start, length = layout.seq_bounds(tok)
    masks = _head_mask()
    kcat = jnp.concatenate([kp_ref[...], kc_ref[...], kn_ref[...]], axis=0)
    vcat = jnp.concatenate([vp_ref[...], vc_ref[...], vn_ref[...]], axis=0)
    col = lax.broadcasted_iota(jnp.int32, (1, 3 * blk), 1)
    sink = sink_ref[...]
    bias = bias_ref[...]
    subs = range(WIN_NB)

    def dead_cols(j):
        has_prev = tok + j * blk > start
        has_next = tok + (j + 1) * blk < start + length
        return ((col < blk) & jnp.logical_not(has_prev)) | ((col >= 2 * blk) & jnp.logical_not(has_next))

    s = [_dot_nt(_stack_heads(q_ref[j * blk:(j + 1) * blk, :], masks), kcat[j * blk:(j + 3) * blk, :]) for j in subs]
    s = [s[j] + bias + jnp.where(dead_cols(j), NEG, 0.0) for j in subs]
    m = [jnp.maximum(jnp.max(s[j], axis=-1, keepdims=True), sink) for j in subs]
    p = [jnp.exp2(s[j] - m[j]) for j in subs]
    den = [jnp.sum(p[j], axis=-1, keepdims=True) + jnp.exp2(sink - m[j]) for j in subs]
    pn = [(p[j] * (1.0 / den[j])).astype(BF16) for j in subs]
    o_all = [_dot(pn[j], vcat[j * blk:(j + 3) * blk, :]) for j in subs]
    for j in subs:
        o_ref[j * blk:(j + 1) * blk, :] = _unstack_heads(o_all[j], masks, blk).astype(BF16)


def _win_attn(layout, q, k, v, bias, sink):
    t = q.shape[0]
    blk = WIN_BLOCK
    nb = t // blk
    cur = pl.BlockSpec((WIN_NB * blk, BR_W), lambda i: (i, 0))
    prev = pl.BlockSpec((blk, BR_W), lambda i: (jnp.maximum(i * WIN_NB - 1, 0), 0))
    nxt = pl.BlockSpec((blk, BR_W), lambda i: (jnp.minimum((i + 1) * WIN_NB, nb - 1), 0))
    return pl.pallas_call(
        functools.partial(_win_body, layout),
        grid=(nb // WIN_NB,),
        in_specs=[cur, prev, cur, nxt, prev, cur, nxt, _const_spec(bias.shape), _const_spec(sink.shape)],
        out_specs=cur,
        out_shape=jax.ShapeDtypeStruct((t, BR_W), BF16),
        compiler_params=_params(("parallel",)),
        name="win_attn",
    )(q, k, k, k, v, v, v, bias, sink)


def _na_body(rows, q_ref, k_ref, v_ref, bias_ref, o_ref):
    j = pl.program_id(1)
    masks = _head_mask()
    nk = NA_KH * GRID_W
    rr = range(NA_RB)
    r = [j * NA_RB + i for i in rr]
    rs = [jnp.clip(r[i] - NA_KH // 2, 0, rows - NA_KH) for i in rr]
    koff = [pl.multiple_of(rs[i] * GRID_W, GRID_W) for i in rr]
    s = [_dot_nt(_stack_heads(q_ref[i * GRID_W:(i + 1) * GRID_W, :], masks), k_ref[pl.ds(koff[i], nk), :])
         + bias_ref[r[i] - rs[i]] for i in rr]
    m = [jnp.max(s[i], axis=-1, keepdims=True) for i in rr]
    p = [jnp.exp2(s[i] - m[i]) for i in rr]
    pn = [(p[i] * (1.0 / jnp.sum(p[i], axis=-1, keepdims=True))).astype(BF16) for i in rr]
    o_all = [_dot(pn[i], v_ref[pl.ds(koff[i], nk), :]) for i in rr]
    for i in rr:
        o_ref[i * GRID_W:(i + 1) * GRID_W, :] = _unstack_heads(o_all[i], masks, GRID_W).astype(BF16)


def _na_class(off, nseq, n, q, k, v, bias):
    rows = n // GRID_W
    qt = NA_RB * GRID_W
    nj = n // qt
    qb = off // qt
    sb = off // n
    seq = pl.BlockSpec((n, BR_W), lambda s, j: (sb + s, 0), pipeline_mode=pl.Buffered(1))
    return pl.pallas_call(
        functools.partial(_na_body, rows),
        grid=(nseq, nj),
        in_specs=[pl.BlockSpec((qt, BR_W), lambda s, j: (qb + s * nj + j, 0)), seq, seq, _const_spec(bias.shape)],
        out_specs=pl.BlockSpec((qt, BR_W), lambda s, j: (s * nj + j, 0)),
        out_shape=jax.ShapeDtypeStruct((nseq * n, BR_W), BF16),
        compiler_params=_params(("parallel", "parallel")),
        name="na_attn",
    )(q, k, v, bias)


def _na_attn(layout, q, k, v, bias):
    return jnp.concatenate([_na_class(off, nseq, n, q, k, v, bias) for off, nseq, n in layout.classes], axis=0)


def _mem_body(q_ref, mk_ref, mv_ref, o_ref):
    masks = _head_mask()
    sub = MEM_TM // MEM_SPLIT
    parts = range(MEM_SPLIT)
    mk = mk_ref[0]
    mv = mv_ref[0]
    s = [_dot_nt(_stack_heads(q_ref[i * sub:(i + 1) * sub, :], masks), mk) for i in parts]
    m = [jnp.max(s[i], axis=-1, keepdims=True) for i in parts]
    p = [jnp.exp2(s[i] - m[i]) for i in parts]
    pn = [(p[i] * (1.0 / jnp.sum(p[i], axis=-1, keepdims=True))).astype(BF16) for i in parts]
    o_all = [_dot(pn[i], mv) for i in parts]
    for i in parts:
        o_ref[i * sub:(i + 1) * sub, :] = _unstack_heads(o_all[i], masks, sub).astype(BF16)


def _mem_attn(layout, q, mk, mv):
    t = q.shape[0]
    tm = MEM_TM
    mem_len = mk.shape[1]
    tile = pl.BlockSpec((tm, BR_W), lambda i: (i, 0))
    mem = pl.BlockSpec((1, mem_len, BR_W), lambda i: (layout.seq_index(i * tm), 0, 0))
    return pl.pallas_call(
        _mem_body,
        grid=(t // tm,),
        in_specs=[tile, mem, mem],
        out_specs=tile,
        out_shape=jax.ShapeDtypeStruct((t, BR_W), BF16),
        compiler_params=_params(("parallel",)),
        name="mem_attn",
    )(q, mk, mv)


def _merge_body(x_ref, ng_ref, oa_ref, ob_ref, oc_ref, od_ref, oe_ref, wg_ref, wm_ref, wb_ref, wo_ref, y_ref):
    x = x_ref[...]
    h = _rms_rows(x, ng_ref[...]).astype(BF16)
    merged = jnp.zeros((x.shape[0], D_MODEL), F32)
    for kb, o_ref in enumerate((oa_ref, ob_ref, oc_ref, od_ref, oe_ref)):
        g = _dot(h, wg_ref[:, kb * BR_W:(kb + 1) * BR_W])
        br = (o_ref[...].astype(F32) * (g * jax.nn.sigmoid(g))).astype(BF16)
        mg = jax.nn.sigmoid(_dot(h, wm_ref[:, kb * D_MODEL:(kb + 1) * D_MODEL]))
        merged = merged + mg * _dot(br, wb_ref[kb])
    y_ref[...] = x + _dot(merged.astype(BF16), wo_ref[...])


def _merge(x, ng, branches, wg, wm, wb, wo):
    t = x.shape[0]
    tm = MERGE_TM
    xt = pl.BlockSpec((tm, D_MODEL), lambda i: (i, 0))
    bt = pl.BlockSpec((tm, BR_W), lambda i: (i, 0))
    return pl.pallas_call(
        _merge_body,
        grid=(t // tm,),
        in_specs=[xt, _const_spec((1, D_MODEL))] + [bt] * N_BRANCH
        + [_const_spec(wg.shape), _const_spec(wm.shape), _const_spec(wb.shape), _const_spec(wo.shape)],
        out_specs=xt,
        out_shape=jax.ShapeDtypeStruct((t, D_MODEL), F32),
        input_output_aliases={0: 0},
        compiler_params=_params(("parallel",)),
        name="merge",
    )(x, ng, *branches, wg, wm, wb, wo)


def _t5_bucket(rel):
    half = N_BUCKETS // 2
    exact = half // 2
    n = jnp.abs(rel)
    nf = jnp.maximum(n, 1).astype(F32)
    large = exact + (jnp.log(nf / exact) / math.log(MAX_DIST / exact) * (half - exact)).astype(jnp.int32)
    large = jnp.clip(large, 0, half - 1)
    return jnp.where(rel > 0, half, 0) + jnp.where(n < exact, n, large)


def _group_matrix(group):
    return jnp.asarray(np.kron(np.eye(BR_W // group), np.full((group, group), 1.0 / group)), BF16)


def _lookup(table, idx):
    onehot = (idx[..., None] == jnp.arange(table.shape[0])).astype(F32)
    return jnp.dot(onehot, table, precision=lax.Precision.HIGHEST)


def _diff_bias_tables(rel_bias):
    t = DF_T
    table = rel_bias[:, :BR_HEADS].astype(F32) * LOG2E
    kl = jnp.arange(t)[:, None]
    ql = jnp.arange(t)[None, :]
    rel = jnp.stack([kl - ql + d * t for d in (-1, 0, 1)])
    tiles = _lookup(table, _t5_bucket(rel)).transpose(0, 3, 1, 2)
    far = _lookup(table, _t5_bucket(jnp.asarray([-2 * t, 2 * t], jnp.int32)))
    return tiles, far


def _win_bias_table(rel_bias):
    rel = jnp.arange(3 * WIN_BLOCK)[None, :] - WIN_BLOCK - jnp.arange(WIN_BLOCK)[:, None]
    bias = _lookup(rel_bias[:, BR_HEADS:].astype(F32) * LOG2E, _t5_bucket(rel)).transpose(2, 0, 1)
    bias = jnp.where((jnp.abs(rel) <= WIN)[None], bias, NEG)
    return bias.reshape(BR_HEADS * WIN_BLOCK, 3 * WIN_BLOCK)


def _na_bias_tables(rpb):
    depth = rpb.shape[0]
    col = np.arange(GRID_W)
    cs = np.clip(col - NA_KW // 2, 0, GRID_W - NA_KW)
    inwin = (col[None, :] >= cs[:, None]) & (col[None, :] < cs[:, None] + NA_KW)
    dc = col[None, :] - col[:, None] + (NA_KW - 1)
    onehot = jnp.asarray(dc[None] == np.arange(2 * NA_KW - 1)[:, None, None], F32)
    toep = jnp.einsum("lhrc,cqk->lhrqk", rpb.astype(F32) * LOG2E, onehot, precision=lax.Precision.HIGHEST)
    toep = jnp.where(jnp.asarray(inwin)[None, None, None], toep, NEG)
    out = []
    for d in range(NA_KH):
        lo = NA_KH - 1 - d
        b = toep[:, :, lo:lo + NA_KH].transpose(0, 1, 3, 2, 4)
        out.append(b.reshape(depth, BR_HEADS * GRID_W, NA_KH * GRID_W))
    return jnp.stack(out, axis=1)


def _tile_gain(g, reps, scale=1.0):
    return jnp.tile(g.astype(F32), reps)[None, :] * scale


def kernel(x_prompt, x_sample, mem_prompt, mem_sample, norm_g, mem_norm_g, w_in, w_mem_kv, rel_bias, hgrn_lb,
           hgrn_norm_g, diff_qk_g, diff_lambda, diff_subln_g, win_qk_g, win_sink, na_qk_g, na_rpb, mem_qk_g,
           w_branch, w_out):
    depth = w_in.shape[0]
    bp, lp, _ = x_prompt.shape
    bs, ls, _ = x_sample.shape
    mem_len = mem_prompt.shape[1]
    layout = _Layout(bp, lp, bs, ls)
    t = layout.total
    assert t % PROJ_TM == 0 and t % MERGE_TM == 0 and lp % DF_T == 0 and ls % DF_T == 0
    assert lp % (NA_RB * GRID_W) == 0 and ls % (NA_RB * GRID_W) == 0 and lp % MEM_TM == 0 and ls % MEM_TM == 0
    assert lp % (WIN_NB * WIN_BLOCK) == 0 and ls % (WIN_NB * WIN_BLOCK) == 0

    x = jnp.concatenate([x_prompt.reshape(bp * lp, D_MODEL), x_sample.reshape(bs * ls, D_MODEL)], axis=0)
    mem = jnp.concatenate([mem_prompt.reshape(bp * mem_len, D_MODEL), mem_sample.reshape(bs * mem_len, D_MODEL)], axis=0)

    sm = jax.nn.softmax(hgrn_lb.astype(F32), axis=1)
    lb_all = jnp.clip(jnp.cumsum(sm, axis=1) - sm[:, :1], 0.0, 1.0 - 1e-6)
    lam_init = jnp.asarray([0.8 - 0.6 * math.exp(-0.3 * l) for l in range(depth)], F32)
    lam = diff_lambda.astype(F32)
    lmb = jnp.exp(jnp.sum(lam[:, 0] * lam[:, 1], axis=-1)) - jnp.exp(jnp.sum(lam[:, 2] * lam[:, 3], axis=-1)) + lam_init

    w_in_b = w_in.astype(BF16)

    def expand_kv(w):
        w = w.reshape(depth, D_MODEL, WIN_KV_HEADS, HEAD_DIM)
        return jnp.repeat(w, BR_HEADS // WIN_KV_HEADS, axis=2).reshape(depth, D_MODEL, BR_W)

    w1 = jnp.concatenate([w_in_b[:, :, A_Q:A_G], w_in_b[:, :, B_Q:B_G], w_in_b[:, :, C_Q:C_K],
                          expand_kv(w_in_b[:, :, C_K:C_V]), expand_kv(w_in_b[:, :, C_V:C_G]),
                          w_in_b[:, :, D_Q:D_G], w_in_b[:, :, E_Q:E_G]], axis=-1)
    wvt = w_in_b[:, :, B_Q + 2 * BR_W:B_G].transpose(0, 2, 1)
    wg = jnp.concatenate([w_in_b[:, :, A_G:B_Q], w_in_b[:, :, B_G:C_Q], w_in_b[:, :, C_G:D_Q],
                          w_in_b[:, :, D_G:E_Q], w_in_b[:, :, E_G:M_G]], axis=-1)
    wm = w_in_b[:, :, M_G:]
    wb = w_branch.astype(BF16)
    wo = w_out.astype(BF16)
    wmem = w_mem_kv.astype(BF16)

    sc_b = DF_DK ** -0.5 * LOG2E
    sc_h = HEAD_DIM ** -0.5 * LOG2E
    zrow = jnp.zeros((depth, 1, BR_W), F32)

    def per_layer(fn):
        return jnp.stack([fn(l) for l in range(depth)])

    ep = jnp.concatenate([
        per_layer(lambda l: _tile_gain(diff_qk_g[l, 0], 8, sc_b)),
        per_layer(lambda l: _tile_gain(diff_qk_g[l, 1], 8)),
        per_layer(lambda l: _tile_gain(win_qk_g[l, 0], 4, sc_h)),
        per_layer(lambda l: _tile_gain(win_qk_g[l, 1], 4)),
        per_layer(lambda l: _tile_gain(na_qk_g[l, 0], 4, sc_h)),
        per_layer(lambda l: _tile_gain(na_qk_g[l, 1], 4)),
        per_layer(lambda l: _tile_gain(mem_qk_g[l, 0], 4, sc_h)),
        lb_all[0][:, None, :], lb_all[1][:, None, :]] + [zrow] * 7, axis=1)
    gk_mem = per_layer(lambda l: _tile_gain(mem_qk_g[l, 1], 4))
    gn_hg = hgrn_norm_g.astype(F32)[:, None, :]
    gcol = per_layer(lambda l: (jnp.tile(diff_subln_g[l].astype(F32), BR_HEADS) * (1.0 - lam_init[l]))[:, None])
    sink = per_layer(lambda l: jnp.repeat(win_sink[l].astype(F32) * LOG2E, WIN_BLOCK)[:, None])
    na_bias = _na_bias_tables(na_rpb)

    diff_bias, diff_far = _diff_bias_tables(rel_bias)
    win_bias = _win_bias_table(rel_bias)
    sc = jnp.concatenate([jnp.broadcast_to(diff_far[None], (depth, 2, BR_HEADS)),
                          jnp.broadcast_to(lmb[:, None, None], (depth, 1, BR_HEADS))], axis=1)

    g32 = _group_matrix(DF_DK)
    g64 = _group_matrix(HEAD_DIM)
    consts_f = tuple(jnp.asarray(a, dt) for a, dt in zip(_hgrn_constants(HG_C, False), (BF16, F32, F32)))
    consts_b = tuple(jnp.asarray(a, dt) for a, dt in zip(_hgrn_constants(HG_C, True), (BF16, F32, F32)))

    def layer(x, p):
        ng = p["ng"]
        (a_q, a_v, a_lf, a_kk, b_q, b_k, b_v, c_q, c_k, c_v, d_q, d_k, d_v, e_q) = _project(
            x, ng, p["w1"], p["wvt"], p["ep"], g32, g64)
        mk, mv = _mem_kv(mem, p["mng"], p["wmem"], p["gk_mem"], g64)
        o_a = _hgrn(layout, a_q, a_v, a_kk, a_lf, consts_f, consts_b, g64, p["gn_hg"])
        o_b = _diff_attn(layout, b_q, b_k, b_v, diff_bias, p["sc"], p["gcol"])
        o_c = _win_attn(layout, c_q, c_k, c_v, win_bias, p["sink"])
        o_d = _na_attn(layout, d_q, d_k, d_v, p["na_bias"])
        o_e = _mem_attn(layout, e_q, mk.reshape(layout.nseq, mem_len, BR_W), mv.reshape(layout.nseq, mem_len, BR_W))
        y = _merge(x, ng, (o_a, o_b, o_c, o_d, o_e), p["wg"], p["wm"], p["wb"], p["wo"])
        return y, None

    params = dict(ng=norm_g.astype(F32)[:, None, :], mng=mem_norm_g.astype(F32)[:, None, :], w1=w1, wvt=wvt, ep=ep,
                  wmem=wmem, gk_mem=gk_mem, gn_hg=gn_hg, sc=sc, gcol=gcol, sink=sink, na_bias=na_bias,
                  wg=wg, wm=wm, wb=wb, wo=wo)
    x, _ = lax.scan(layer, x, params)
    y_prompt = x[:layout.off1].reshape(bp, lp, D_MODEL)
    y_sample = x[layout.off1:].reshape(bs, ls, D_MODEL)
    return (y_prompt, y_sample)
```

```python
import functools
import math

import numpy as np
import jax
import jax.numpy as jnp
from jax import lax
from jax.experimental import pallas as pl
from jax.experimental.pallas import tpu as pltpu

F32 = jnp.float32
BF16 = jnp.bfloat16

D_MODEL = 1024
HEAD_DIM = 64
BR_HEADS = 4
BR_W = BR_HEADS * HEAD_DIM
N_BRANCH = 5
DF_DK = HEAD_DIM // 2
WIN = 128
WIN_BLOCK = 128
WIN_KV_HEADS = 2
GRID_W = 64
NA_KH = 8
NA_KW = 16
N_BUCKETS = 32
MAX_DIST = 128
EPS = 1e-6
NEG = -1e30
LB_FLOOR = 1e-30
LOG2E = 1.4426950408889634

A_Q = 0
A_G = 4 * BR_W
B_Q = A_G + BR_W
B_G = B_Q + 3 * BR_W
C_Q = B_G + BR_W
C_K = C_Q + BR_W
C_V = C_K + WIN_KV_HEADS * HEAD_DIM
C_G = C_V + WIN_KV_HEADS * HEAD_DIM
D_Q = C_G + BR_W
D_G = D_Q + 3 * BR_W
E_Q = D_G + BR_W
E_G = E_Q + BR_W
M_G = E_G + BR_W

VMEM_LIMIT_BYTES = 56 * 1024 * 1024

PROJ_TM = 1024
MERGE_TM = 1024
MEM_TM = 512
MEM_SPLIT = 4
HG_C = 128
HG_MAT_LEVELS = 2
DF_T = 256
NA_RB = 8
WIN_NB = 4
HG_NORM_TM = 2048
DF_VROWS = 80
DF_RC = 64
DF_KH = 128
DF_UNROLL = 4


def _params(sem):
    return pltpu.CompilerParams(dimension_semantics=sem, vmem_limit_bytes=VMEM_LIMIT_BYTES)


def _const_spec(shape):
    nd = len(shape)
    return pl.BlockSpec(shape, lambda *_: (0,) * nd, pipeline_mode=pl.Buffered(1))


def _dot(a, b):
    return jnp.dot(a, b, preferred_element_type=F32)


def _dot_nt(a, b):
    return lax.dot_general(a, b, (((1,), (1,)), ((), ())), preferred_element_type=F32)


def _rms_rows(x, g):
    ms = jnp.mean(x * x, axis=-1, keepdims=True)
    return x * lax.rsqrt(ms + EPS) * g


def _group_rms(x, gmat):
    x2 = x * x
    hi = x2.astype(BF16)
    lo = (x2 - hi.astype(F32)).astype(BF16)
    ms = _dot(hi, gmat) + _dot(lo, gmat)
    return x * lax.rsqrt(ms + EPS)


def _head_mask(width=BR_W):
    lane = lax.broadcasted_iota(jnp.int32, (1, width), 1)
    return [(lane >= h * HEAD_DIM) & (lane < (h + 1) * HEAD_DIM) for h in range(BR_HEADS)]


def _stack_heads(q, masks):
    zero = jnp.zeros_like(q)
    return jnp.concatenate([jnp.where(m, q, zero) for m in masks], axis=0)


def _unstack_heads(o_all, masks, m):
    out = jnp.zeros((m, BR_W), F32)
    for h, mk in enumerate(masks):
        out = out + jnp.where(mk, o_all[h * m:(h + 1) * m, :], 0.0)
    return out


class _Layout:
    def __init__(self, n_prompt, len_prompt, n_sample, len_sample):
        self.classes = ((0, n_prompt, len_prompt), (n_prompt * len_prompt, n_sample, len_sample))
        self.off1 = n_prompt * len_prompt
        self.lp = len_prompt
        self.ls = len_sample
        self.total = self.off1 + n_sample * len_sample
        self.nseq = n_prompt + n_sample
        self.n_prompt = n_prompt
        assert self.off1 % len_sample == 0

    def seq_bounds(self, tok):
        in_p = tok < self.off1
        start_p = (tok // self.lp) * self.lp
        start_s = self.off1 + ((tok - self.off1) // self.ls) * self.ls
        return jnp.where(in_p, start_p, start_s), jnp.where(in_p, self.lp, self.ls)

    def seq_index(self, tok):
        return jnp.where(tok < self.off1, tok // self.lp, self.n_prompt + (tok - self.off1) // self.ls)


def _proj_body(x_ref, ng_ref, w_ref, wvt_ref, ep_ref, g32_ref, g64_ref,
               aq_ref, av_ref, alf_ref, akk_ref, bq_ref, bk_ref, bv_ref,
               cq_ref, ck_ref, cv_ref, dq_ref, dk_ref, dv_ref, eq_ref):
    h = _rms_rows(x_ref[...], ng_ref[...]).astype(BF16)
    ep = ep_ref[...]

    def seg(i):
        return _dot(h, w_ref[:, i * BR_W:(i + 1) * BR_W])

    def row(r):
        return ep[r:r + 1, :]

    def silu_out(ref):
        def fin(a):
            ref[...] = (a * jax.nn.sigmoid(a)).astype(BF16)
        return fin

    def plain_out(ref):
        def fin(a):
            ref[...] = a.astype(BF16)
        return fin

    def normed_out(ref, g_ref, gain_row):
        def fin(a):
            ref[...] = (_group_rms(a, g_ref[...]) * row(gain_row)).astype(BF16)
        return fin

    def decay_out(d):
        def fin(z):
            lb = row(7 + d)
            e = jnp.exp(-jnp.abs(z))
            log_sig = jnp.minimum(z, 0.0) - jnp.log1p(e)
            t1 = jnp.log(jnp.maximum(lb, LB_FLOOR))
            t2 = jnp.log1p(-lb) + log_sig
            logf = jnp.maximum(t1, t2) + jnp.log1p(jnp.exp(-jnp.abs(t1 - t2)))
            sig_neg = jnp.where(z >= 0.0, e, 1.0) / (1.0 + e)
            alf_ref[:, d * BR_W:(d + 1) * BR_W] = logf
            akk_ref[:, d * BR_W:(d + 1) * BR_W] = ((1.0 - lb) * sig_neg).astype(BF16)
        return fin

    def vt_out(vt):
        vt = vt.astype(BF16)
        pad_row = lax.broadcasted_iota(jnp.int32, (DF_VROWS - HEAD_DIM, DF_T), 0)
        pad = jnp.where(pad_row == 0, 1.0, 0.0).astype(BF16)
        for j in range(PROJ_TM // DF_T):
            for hd in range(BR_HEADS):
                bv_ref[j, hd, 0:HEAD_DIM, :] = vt[hd * HEAD_DIM:(hd + 1) * HEAD_DIM, j * DF_T:(j + 1) * DF_T]
                bv_ref[j, hd, HEAD_DIM:DF_VROWS, :] = pad

    tasks = [
        (lambda: seg(0), silu_out(aq_ref)),
        (lambda: seg(1), plain_out(av_ref)),
        (lambda: seg(2), decay_out(0)),
        (lambda: seg(3), decay_out(1)),
        (lambda: seg(4), normed_out(bq_ref, g32_ref, 0)),
        (lambda: seg(5), normed_out(bk_ref, g32_ref, 1)),
        (lambda: _dot_nt(wvt_ref[...], h), vt_out),
        (lambda: seg(7), normed_out(cq_ref, g64_ref, 2)),
        (lambda: seg(8), normed_out(ck_ref, g64_ref, 3)),
        (lambda: seg(9), plain_out(cv_ref)),
        (lambda: seg(10), normed_out(dq_ref, g64_ref, 4)),
        (lambda: seg(11), normed_out(dk_ref, g64_ref, 5)),
        (lambda: seg(12), plain_out(dv_ref)),
        (lambda: seg(13), normed_out(eq_ref, g64_ref, 6)),
    ]
    cur = tasks[0][0]()
    for i, (_, fin) in enumerate(tasks):
        nxt = tasks[i + 1][0]() if i + 1 < len(tasks) else None
        fin(cur)
        cur = nxt


def _project(x, ng, w1, wvt, ep, g32, g64):
    t = x.shape[0]
    tm = PROJ_TM
    tile = lambda w: pl.BlockSpec((tm, w), lambda i: (i, 0))
    widths = [BR_W, BR_W, 2 * BR_W, 2 * BR_W] + [BR_W] * 10
    dtypes = [BF16, BF16, F32, BF16] + [BF16] * 10
    out_specs = [tile(w) for w in widths]
    out_shape = [jax.ShapeDtypeStruct((t, w), dt) for w, dt in zip(widths, dtypes)]
    kt = tm // DF_T
    out_specs[6] = pl.BlockSpec((kt, BR_HEADS, DF_VROWS, DF_T), lambda i: (i, 0, 0, 0))
    out_shape[6] = jax.ShapeDtypeStruct((t // DF_T, BR_HEADS, DF_VROWS, DF_T), BF16)
    return pl.pallas_call(
        _proj_body,
        grid=(t // tm,),
        in_specs=[tile(D_MODEL), _const_spec((1, D_MODEL)), _const_spec(w1.shape), _const_spec(wvt.shape),
                  _const_spec(ep.shape), _const_spec(g32.shape), _const_spec(g64.shape)],
        out_specs=out_specs,
        out_shape=out_shape,
        compiler_params=_params(("parallel",)),
        name="proj",
    )(x, ng, w1, wvt, ep, g32, g64)


def _memkv_body(m_ref, g_ref, w_ref, gk_ref, g64_ref, mk_ref, mv_ref):
    mh = _rms_rows(m_ref[...], g_ref[...]).astype(BF16)
    kv = _dot(mh, w_ref[...])
    mk_ref[...] = (_group_rms(kv[:, :BR_W], g64_ref[...]) * gk_ref[...]).astype(BF16)
    mv_ref[...] = kv[:, BR_W:].astype(BF16)


def _mem_kv(mem, g, w, gk, g64):
    t = mem.shape[0]
    tm = 256
    return pl.pallas_call(
        _memkv_body,
        grid=(t // tm,),
        in_specs=[pl.BlockSpec((tm, D_MODEL), lambda i: (i, 0)), _const_spec((1, D_MODEL)), _const_spec(w.shape),
                  _const_spec((1, BR_W)), _const_spec(g64.shape)],
        out_specs=[pl.BlockSpec((tm, BR_W), lambda i: (i, 0))] * 2,
        out_shape=[jax.ShapeDtypeStruct((t, BR_W), BF16)] * 2,
        compiler_params=_params(("parallel",)),
        name="memkv",
    )(mem, g, w, gk, g64)


def _hgrn_constants(c, reverse):
    nl = int(math.log2(c))
    idx = np.arange(c)
    t = idx[:, None]
    u = idx[None, :]
    incl = (u <= t).astype(np.float32)
    tot = np.ones((8, c), np.float32)
    mds, mes, lms = [], [], [np.eye(c, dtype=np.float32)]
    for lev in range(nl):
        w = 1 << lev
        blk = idx // w
        odd = (blk % 2 == 1)
        md = (odd[:, None] & (u >= (blk * w)[:, None]) & (u <= t)).astype(np.float32)
        me = ((~odd)[:, None] & (u > t) & (u <= ((blk + 1) * w - 1)[:, None])).astype(np.float32)
        lm = (odd[:, None] & (blk[None, :] == (blk - 1)[:, None])).astype(np.float32)
        mds.append(md)
        mes.append(me)
        lms.append(lm)
    mats = [incl, tot] + [md + me for md, me in zip(mds[:HG_MAT_LEVELS], mes[:HG_MAT_LEVELS])]
    if reverse:
        mats = [m[::-1, ::-1] for m in mats]
        lms = [m[::-1, ::-1] for m in lms]
    mall = np.concatenate(mats, axis=0)
    lmst = np.stack([np.tile(m, (1, BR_HEADS)) for m in lms])
    bd = np.kron(np.eye(BR_HEADS, dtype=np.float32), np.ones((HEAD_DIM, HEAD_DIM), np.float32))
    return mall, lmst, bd


def _hgrn_chunk(layout, chunk, reverse, q_ref, k_ref, v_ref, lf_ref, mall_ref, lm_ref, bd_ref, o_ref, st_ref):
    c = HG_C
    nl = int(math.log2(c))
    tok = chunk * c
    start, length = layout.seq_bounds(tok)
    fresh = (tok + c == start + length) if reverse else (tok == start)

    masks = _head_mask()
    lf = lf_ref[...]
    hi = lf.astype(BF16)
    mid = (lf - hi.astype(F32)).astype(BF16)
    mall = mall_ref[...]
    cums = _dot(mall, hi) + _dot(mall, mid)
    b = cums[0:c, :]
    b_tot = cums[c:c + 1, :]
    b_rest = b_tot - b
    base = c + 8

    def level_decay(lev):
        if lev < HG_MAT_LEVELS:
            return cums[base + lev * c:base + (lev + 1) * c, :]
        w = 1 << lev
        b3 = b.reshape(c // (2 * w), 2 * w, BR_W)
        r = w if reverse else w - 1
        d3 = b3 - b3[:, r:r + 1, :]
        second = lax.broadcasted_iota(jnp.int32, (1, 2 * w, 1), 1) >= w
        query_half = jnp.logical_not(second) if reverse else second
        return jnp.where(query_half, d3, -d3).reshape(c, BR_W)

    qf = q_ref[...].astype(F32)
    kf = k_ref[...].astype(F32)
    v = v_ref[...]
    st = jnp.where(fresh, 0.0, st_ref[...])

    o = _dot_nt((qf * jnp.exp(b)).astype(BF16), st.astype(BF16))
    dim_row = lax.broadcasted_iota(jnp.int32, (BR_W, 1), 0)
    head_rows = [(dim_row >= h * HEAD_DIM) & (dim_row < (h + 1) * HEAD_DIM) for h in range(BR_HEADS)]

    def scores(ql, kl):
        kt = kl.T.astype(BF16)
        rhs = jnp.concatenate([jnp.where(hr, kt, jnp.zeros_like(kt)) for hr in head_rows], axis=1)
        return _dot(ql, rhs)

    a_all = scores(q_ref[...], kf) * lm_ref[0]
    for lev in range(nl):
        dec = jnp.exp(level_decay(lev))
        a_all = a_all + scores((qf * dec).astype(BF16), kf * dec) * lm_ref[lev + 1]
    a_bf = a_all.astype(BF16)
    zero = jnp.zeros_like(v)
    for h, mk in enumerate(masks):
        o = o + _dot(a_bf[:, h * c:(h + 1) * c], jnp.where(mk, v, zero))
    o_ref[...] = o

    kst = (kf * jnp.exp(b_rest)).astype(BF16)
    vt = v.astype(F32).T.astype(BF16)
    st_ref[...] = (st * jnp.exp(b_tot) + _dot(vt, kst)) * bd_ref[...]


def _hgrn_body(layout, nct, qf_ref, kf_ref, vf_ref, lff_ref, qb_ref, kb_ref, vb_ref, lfb_ref,
               mallf_ref, lmf_ref, mallb_ref, lmb_ref, bd_ref, of_ref, ob_ref, stf_ref, stb_ref):
    i = pl.program_id(0)
    _hgrn_chunk(layout, i, False, qf_ref, kf_ref, vf_ref, lff_ref, mallf_ref, lmf_ref, bd_ref, of_ref, stf_ref)
    _hgrn_chunk(layout, nct - 1 - i, True, qb_ref, kb_ref, vb_ref, lfb_ref, mallb_ref, lmb_ref, bd_ref,
                ob_ref, stb_ref)


def _hgrn_norm_body(of_ref, ob_ref, g64_ref, gn_ref, o_ref):
    o_ref[...] = (_group_rms(of_ref[...] + ob_ref[...], g64_ref[...]) * gn_ref[...]).astype(BF16)


def _hgrn(layout, qs, v, kk, logf, consts_f, consts_b, g64, gn):
    t = qs.shape[0]
    c = HG_C
    nct = t // c
    mall_f, lm_f, bd = consts_f
    mall_b, lm_b, _ = consts_b

    def specs(reverse):
        cm = (lambda i: nct - 1 - i) if reverse else (lambda i: i)
        d = 1 if reverse else 0
        tile = pl.BlockSpec((c, BR_W), lambda i: (cm(i), 0))
        half = pl.BlockSpec((c, BR_W), lambda i: (cm(i), d))
        return tile, half

    tile_f, half_f = specs(False)
    tile_b, half_b = specs(True)
    o_f, o_b = pl.pallas_call(
        functools.partial(_hgrn_body, layout, nct),
        grid=(nct,),
        in_specs=[tile_f, half_f, tile_f, half_f, tile_b, half_b, tile_b, half_b,
                  _const_spec(mall_f.shape), _const_spec(lm_f.shape), _const_spec(mall_b.shape),
                  _const_spec(lm_b.shape), _const_spec(bd.shape)],
        out_specs=[tile_f, tile_b],
        out_shape=[jax.ShapeDtypeStruct((t, BR_W), F32)] * 2,
        scratch_shapes=[pltpu.VMEM((BR_W, BR_W), F32)] * 2,
        compiler_params=_params(("arbitrary",)),
        name="hgrn_scan",
    )(qs, kk, v, logf, qs, kk, v, logf, mall_f, lm_f, mall_b, lm_b, bd)

    tm = math.gcd(t, HG_NORM_TM)
    tok = pl.BlockSpec((tm, BR_W), lambda i: (i, 0))
    return pl.pallas_call(
        _hgrn_norm_body,
        grid=(t // tm,),
        in_specs=[tok, tok, _const_spec(g64.shape), _const_spec((1, BR_W))],
        out_specs=tok,
        out_shape=jax.ShapeDtypeStruct((t, BR_W), BF16),
        compiler_params=_params(("parallel",)),
        name="hgrn_norm",
    )(o_f, o_b, g64, gn)


def _diff_body(nkt, q_ref, k_ref, vt_ref, bias_ref, sc_ref, gcol_ref, o_ref, qm_ref, m_ref, acc_ref,
               s0_ref, s1_ref, mx0_ref, mx1_ref):
    tq = DF_T
    qi = pl.program_id(1)
    qt = q_ref[...].astype(F32).T.astype(BF16)
    rowid = lax.broadcasted_iota(jnp.int32, (DF_KH, tq), 0)
    for hc in range(2 * BR_HEADS):
        half, sub = divmod(hc * DF_DK, DF_KH)
        sel = (rowid >= sub) & (rowid < sub + DF_DK)
        qh = qt[half * DF_KH:(half + 1) * DF_KH, :]
        qm_ref[hc] = jnp.where(sel, qh, jnp.zeros_like(qh))
    m_ref[...] = jnp.full(m_ref.shape, NEG, F32)
    acc_ref[...] = jnp.zeros_like(acc_ref)

    nu = 2 * BR_HEADS
    rows = [(r, r + DF_RC) for r in range(0, tq, DF_RC)]

    def qk_unit(kj, hc, buf, near=None):
        sbuf, mxbuf = buf
        half = hc * DF_DK // DF_KH
        s = _dot(k_ref[kj, :, half * DF_KH:(half + 1) * DF_KH], qm_ref[hc])
        if near is not None:
            s = s + bias_ref[near, hc // 2]
        sbuf[hc] = s
        mxbuf[hc] = jnp.max(s.reshape(tq // 8, 8, tq), axis=0)

    def softmax_pv_unit(kj, hc, buf, const_bias):
        sbuf, mxbuf = buf
        c = const_bias(hc // 2)
        m_old = m_ref[hc]
        m_new = jnp.maximum(m_old, jnp.max(mxbuf[hc], axis=0, keepdims=True) + c)
        shift = m_new - c
        p = jnp.concatenate([jnp.exp2(sbuf[hc, r0:r1, :] - shift).astype(BF16) for r0, r1 in rows], axis=0)
        m_ref[hc] = m_new
        acc_ref[hc] = acc_ref[hc] * jnp.exp2(m_old - m_new) + _dot(vt_ref[kj, hc // 2], p)

    def step(kj, buf, const_bias, nxt):
        for hc in range(nu):
            softmax_pv_unit(kj, hc, buf, const_bias)
            qk_unit(nxt[0], hc, nxt[1], nxt[2])

    last = nkt - 1
    n_low = jnp.maximum(qi - 1, 0)
    n_far = n_low + jnp.maximum(nkt - qi - 2, 0)
    buf_a = (s0_ref, mx0_ref)
    buf_b = (s1_ref, mx1_ref)

    def far_tile(f):
        return jnp.minimum(jnp.where(f < n_low, f, f - n_low + qi + 2), last)

    def dead_if(cond):
        pen = jnp.where(cond, 2.0 * NEG, 0.0)
        return lambda h: pen

    near = [jnp.clip(qi + d - 1, 0, last) for d in range(3)]
    for hc in range(nu):
        qk_unit(near[0], hc, buf_a, 0)
    step(near[0], buf_a, dead_if(qi == 0), (near[1], buf_b, 1))
    step(near[1], buf_b, dead_if(False), (near[2], buf_a, 2))
    step(near[2], buf_a, dead_if(qi == last), (far_tile(0), buf_b, None))

    def far_step(f, cur, nxt):
        kj = far_tile(f)
        row = jnp.where(kj < qi, 0, 1)
        live = f < n_far
        step(kj, cur, lambda h: jnp.where(live, sc_ref[row, h], 2.0 * NEG), (far_tile(f + 1), nxt, None))

    far_step(0, buf_b, buf_a)

    def body(i, carry):
        f0 = 1 + DF_UNROLL * i
        for u in range(DF_UNROLL):
            cur, nxt = (buf_a, buf_b) if u % 2 == 0 else (buf_b, buf_a)
            far_step(f0 + u, cur, nxt)
        return carry
    lax.fori_loop(0, (n_far - 1 + DF_UNROLL - 1) // DF_UNROLL, body, 0)

    lmb = sc_ref[2, 0]
    outs = []
    for h in range(BR_HEADS):
        a0 = acc_ref[2 * h]
        a1 = acc_ref[2 * h + 1]
        o0 = a0[:HEAD_DIM, :] / a0[HEAD_DIM:HEAD_DIM + 1, :]
        o1 = a1[:HEAD_DIM, :] / a1[HEAD_DIM:HEAD_DIM + 1, :]
        o = o0 - lmb * o1
        ms = jnp.mean(o * o, axis=0, keepdims=True)
        outs.append(o * lax.rsqrt(ms + EPS))
    ot = jnp.concatenate(outs, axis=0) * gcol_ref[...]
    o_ref[...] = ot.T.astype(BF16)


def _diff_attn_class(off, nseq, n, qt, k3, vt4, bias, sc, gcol):
    t = DF_T
    nkt = n // t
    qb = off // t
    sb = off // n
    return pl.pallas_call(
        functools.partial(_diff_body, nkt),
        grid=(nseq, nkt),
        in_specs=[pl.BlockSpec((t, BR_W), lambda s, i: (qb + s * nkt + i, 0)),
                  pl.BlockSpec((nkt, t, BR_W), lambda s, i: (sb + s, 0, 0), pipeline_mode=pl.Buffered(1)),
                  pl.BlockSpec((nkt, BR_HEADS, DF_VROWS, t), lambda s, i: (sb + s, 0, 0, 0),
                               pipeline_mode=pl.Buffered(1)),
                  _const_spec(bias.shape),
                  pl.BlockSpec(memory_space=pltpu.SMEM),
                  _const_spec(gcol.shape)],
        out_specs=pl.BlockSpec((t, BR_W), lambda s, i: (s * nkt + i, 0)),
        out_shape=jax.ShapeDtypeStruct((nseq * n, BR_W), BF16),
        scratch_shapes=[pltpu.VMEM((2 * BR_HEADS, DF_KH, t), BF16),
                        pltpu.VMEM((2 * BR_HEADS, 1, t), F32),
                        pltpu.VMEM((2 * BR_HEADS, DF_VROWS, t), F32),
                        pltpu.VMEM((2 * BR_HEADS, t, t), F32),
                        pltpu.VMEM((2 * BR_HEADS, t, t), F32),
                        pltpu.VMEM((2 * BR_HEADS, 8, t), F32),
                        pltpu.VMEM((2 * BR_HEADS, 8, t), F32)],
        compiler_params=_params(("parallel", "parallel")),
        name="diff_attn",
    )(qt, k3, vt4, bias, sc, gcol)


def _diff_attn(layout, q, k, vt4, bias, sc, gcol):
    t = k.shape[0]
    k3 = k.reshape(t // DF_T, DF_T, BR_W)
    outs = [_diff_attn_class(off, nseq, n, q, k3, vt4, bias, sc, gcol) for off, nseq, n in layout.classes]
    return jnp.concatenate(outs, axis=0)


def _win_body(layout, q_ref, kp_ref, kc_ref, kn_ref, vp_ref, vc_ref, vn_ref, bias_ref, sink_ref, o_ref):
    blk = WIN_BLOCK
    i = pl.program_id(0)
    tok = i * (WIN_NB * blk)
    start, length = layout.seq_bounds(tok)
    masks = _head_mask()
    kcat = jnp.concatenate([kp_ref[...], kc_ref[...], kn_ref[...]], axis=0)
    vcat = jnp.concatenate([vp_ref[...], vc_ref[...], vn_ref[...]], axis=0)
    col = lax.broadcasted_iota(jnp.int32, (1, 3 * blk), 1)
    sink = sink_ref[...]
    bias = bias_ref[...]
    subs = range(WIN_NB)

    def dead_cols(j):
        has_prev = tok + j * blk > start
        has_next = tok + (j + 1) * blk < start + length
        return ((col < blk) & jnp.logical_not(has_prev)) | ((col >= 2 * blk) & jnp.logical_not(has_next))

    s = [_dot_nt(_stack_heads(q_ref[j * blk:(j + 1) * blk, :], masks), kcat[j * blk:(j + 3) * blk, :]) for j in subs]
    s = [s[j] + bias + jnp.where(dead_cols(j), NEG, 0.0) for j in subs]
    m = [jnp.maximum(jnp.max(s[j], axis=-1, keepdims=True), sink) for j in subs]
    p = [jnp.exp2(s[j] - m[j]) for j in subs]
    den = [jnp.sum(p[j], axis=-1, keepdims=True) + jnp.exp2(sink - m[j]) for j in subs]
    pn = [(p[j] * (1.0 / den[j])).astype(BF16) for j in subs]
    o_all = [_dot(pn[j], vcat[j * blk:(j + 3) * blk, :]) for j in subs]
    for j in subs:
        o_ref[j * blk:(j + 1) * blk, :] = _unstack_heads(o_all[j], masks, blk).astype(BF16)


def _win_attn(layout, q, k, v, bias, sink):
    t = q.shape[0]
    blk = WIN_BLOCK
    nb = t // blk
    cur = pl.BlockSpec((WIN_NB * blk, BR_W), lambda i: (i, 0))
    prev = pl.BlockSpec((blk, BR_W), lambda i: (jnp.maximum(i * WIN_NB - 1, 0), 0))
    nxt = pl.BlockSpec((blk, BR_W), lambda i: (jnp.minimum((i + 1) * WIN_NB, nb - 1), 0))
    return pl.pallas_call(
        functools.partial(_win_body, layout),
        grid=(nb // WIN_NB,),
        in_specs=[cur, prev, cur, nxt, prev, cur, nxt, _const_spec(bias.shape), _const_spec(sink.shape)],
        out_specs=cur,
        out_shape=jax.ShapeDtypeStruct((t, BR_W), BF16),
        compiler_params=_params(("parallel",)),
        name="win_attn",
    )(q, k, k, k, v, v, v, bias, sink)


def _na_body(rows, q_ref, k_ref, v_ref, bias_ref, o_ref):
    j = pl.program_id(1)
    masks = _head_mask()
    nk = NA_KH * GRID_W
    rr = range(NA_RB)
    r = [j * NA_RB + i for i in rr]
    rs = [jnp.clip(r[i] - NA_KH // 2, 0, rows - NA_KH) for i in rr]
    koff = [pl.multiple_of(rs[i] * GRID_W, GRID_W) for i in rr]
    s = [_dot_nt(_stack_heads(q_ref[i * GRID_W:(i + 1) * GRID_W, :], masks), k_ref[pl.ds(koff[i], nk), :])
         + bias_ref[r[i] - rs[i]] for i in rr]
    m = [jnp.max(s[i], axis=-1, keepdims=True) for i in rr]
    p = [jnp.exp2(s[i] - m[i]) for i in rr]
    pn = [(p[i] * (1.0 / jnp.sum(p[i], axis=-1, keepdims=True))).astype(BF16) for i in rr]
    o_all = [_dot(pn[i], v_ref[pl.ds(koff[i], nk), :]) for i in rr]
    for i in rr:
        o_ref[i * GRID_W:(i + 1) * GRID_W, :] = _unstack_heads(o_all[i], masks, GRID_W).astype(BF16)


def _na_class(off, nseq, n, q, k, v, bias):
    rows = n // GRID_W
    qt = NA_RB * GRID_W
    nj = n // qt
    qb = off // qt
    sb = off // n
    seq = pl.BlockSpec((n, BR_W), lambda s, j: (sb + s, 0), pipeline_mode=pl.Buffered(1))
    return pl.pallas_call(
        functools.partial(_na_body, rows),
        grid=(nseq, nj),
        in_specs=[pl.BlockSpec((qt, BR_W), lambda s, j: (qb + s * nj + j, 0)), seq, seq, _const_spec(bias.shape)],
        out_specs=pl.BlockSpec((qt, BR_W), lambda s, j: (s * nj + j, 0)),
        out_shape=jax.ShapeDtypeStruct((nseq * n, BR_W), BF16),
        compiler_params=_params(("parallel", "parallel")),
        name="na_attn",
    )(q, k, v, bias)


def _na_attn(layout, q, k, v, bias):
    return jnp.concatenate([_na_class(off, nseq, n, q, k, v, bias) for off, nseq, n in layout.classes], axis=0)


def _mem_body(q_ref, mk_ref, mv_ref, o_ref):
    masks = _head_mask()
    sub = MEM_TM // MEM_SPLIT
    parts = range(MEM_SPLIT)
    mk = mk_ref[0]
    mv = mv_ref[0]
    s = [_dot_nt(_stack_heads(q_ref[i * sub:(i + 1) * sub, :], masks), mk) for i in parts]
    m = [jnp.max(s[i], axis=-1, keepdims=True) for i in parts]
    p = [jnp.exp2(s[i] - m[i]) for i in parts]
    pn = [(p[i] * (1.0 / jnp.sum(p[i], axis=-1, keepdims=True))).astype(BF16) for i in parts]
    o_all = [_dot(pn[i], mv) for i in parts]
    for i in parts:
        o_ref[i * sub:(i + 1) * sub, :] = _unstack_heads(o_all[i], masks, sub).astype(BF16)


def _mem_attn(layout, q, mk, mv):
    t = q.shape[0]
    tm = MEM_TM
    mem_len = mk.shape[1]
    tile = pl.BlockSpec((tm, BR_W), lambda i: (i, 0))
    mem = pl.BlockSpec((1, mem_len, BR_W), lambda i: (layout.seq_index(i * tm), 0, 0))
    return pl.pallas_call(
        _mem_body,
        grid=(t // tm,),
        in_specs=[tile, mem, mem],
        out_specs=tile,
        out_shape=jax.ShapeDtypeStruct((t, BR_W), BF16),
        compiler_params=_params(("parallel",)),
        name="mem_attn",
    )(q, mk, mv)


def _merge_body(x_ref, ng_ref, oa_ref, ob_ref, oc_ref, od_ref, oe_ref, wg_ref, wm_ref, wb_ref, wo_ref, y_ref):
    x = x_ref[...]
    h = _rms_rows(x, ng_ref[...]).astype(BF16)
    merged = jnp.zeros((x.shape[0], D_MODEL), F32)
    for kb, o_ref in enumerate((oa_ref, ob_ref, oc_ref, od_ref, oe_ref)):
        g = _dot(h, wg_ref[:, kb * BR_W:(kb + 1) * BR_W])
        br = (o_ref[...].astype(F32) * (g * jax.nn.sigmoid(g))).astype(BF16)
        mg = jax.nn.sigmoid(_dot(h, wm_ref[:, kb * D_MODEL:(kb + 1) * D_MODEL]))
        merged = merged + mg * _dot(br, wb_ref[kb])
    y_ref[...] = x + _dot(merged.astype(BF16), wo_ref[...])


def _merge(x, ng, branches, wg, wm, wb, wo):
    t = x.shape[0]
    tm = MERGE_TM
    xt = pl.BlockSpec((tm, D_MODEL), lambda i: (i, 0))
    bt = pl.BlockSpec((tm, BR_W), lambda i: (i, 0))
    return pl.pallas_call(
        _merge_body,
        grid=(t // tm,),
        in_specs=[xt, _const_spec((1, D_MODEL))] + [bt] * N_BRANCH
        + [_const_spec(wg.shape), _const_spec(wm.shape), _const_spec(wb.shape), _const_spec(wo.shape)],
        out_specs=xt,
        out_shape=jax.ShapeDtypeStruct((t, D_MODEL), F32),
        input_output_aliases={0: 0},
        compiler_params=_params(("parallel",)),
        name="merge",
    )(x, ng, *branches, wg, wm, wb, wo)


def _t5_bucket(rel):
    half = N_BUCKETS // 2
    exact = half // 2
    n = jnp.abs(rel)
    nf = jnp.maximum(n, 1).astype(F32)
    large = exact + (jnp.log(nf / exact) / math.log(MAX_DIST / exact) * (half - exact)).astype(jnp.int32)
    large = jnp.clip(large, 0, half - 1)
    return jnp.where(rel > 0, half, 0) + jnp.where(n < exact, n, large)


def _group_matrix(group):
    return jnp.asarray(np.kron(np.eye(BR_W // group), np.full((group, group), 1.0 / group)), BF16)


def _lookup(table, idx):
    onehot = (idx[..., None] == jnp.arange(table.shape[0])).astype(F32)
    return jnp.dot(onehot, table, precision=lax.Precision.HIGHEST)


def _diff_bias_tables(rel_bias):
    t = DF_T
    table = rel_bias[:, :BR_HEADS].astype(F32) * LOG2E
    kl = jnp.arange(t)[:, None]
    ql = jnp.arange(t)[None, :]
    rel = jnp.stack([kl - ql + d * t for d in (-1, 0, 1)])
    tiles = _lookup(table, _t5_bucket(rel)).transpose(0, 3, 1, 2)
    far = _lookup(table, _t5_bucket(jnp.asarray([-2 * t, 2 * t], jnp.int32)))
    return tiles, far


def _win_bias_table(rel_bias):
    rel = jnp.arange(3 * WIN_BLOCK)[None, :] - WIN_BLOCK - jnp.arange(WIN_BLOCK)[:, None]
    bias = _lookup(rel_bias[:, BR_HEADS:].astype(F32) * LOG2E, _t5_bucket(rel)).transpose(2, 0, 1)
    bias = jnp.where((jnp.abs(rel) <= WIN)[None], bias, NEG)
    return bias.reshape(BR_HEADS * WIN_BLOCK, 3 * WIN_BLOCK)


def _na_bias_tables(rpb):
    depth = rpb.shape[0]
    col = np.arange(GRID_W)
    cs = np.clip(col - NA_KW // 2, 0, GRID_W - NA_KW)
    inwin = (col[None, :] >= cs[:, None]) & (col[None, :] < cs[:, None] + NA_KW)
    dc = col[None, :] - col[:, None] + (NA_KW - 1)
    onehot = jnp.asarray(dc[None] == np.arange(2 * NA_KW - 1)[:, None, None], F32)
    toep = jnp.einsum("lhrc,cqk->lhrqk", rpb.astype(F32) * LOG2E, onehot, precision=lax.Precision.HIGHEST)
    toep = jnp.where(jnp.asarray(inwin)[None, None, None], toep, NEG)
    out = []
    for d in range(NA_KH):
        lo = NA_KH - 1 - d
        b = toep[:, :, lo:lo + NA_KH].transpose(0, 1, 3, 2, 4)
        out.append(b.reshape(depth, BR_HEADS * GRID_W, NA_KH * GRID_W))
    return jnp.stack(out, axis=1)


def _tile_gain(g, reps, scale=1.0):
    return jnp.tile(g.astype(F32), reps)[None, :] * scale


def kernel(x_prompt, x_sample, mem_prompt, mem_sample, norm_g, mem_norm_g, w_in, w_mem_kv, rel_bias, hgrn_lb,
           hgrn_norm_g, diff_qk_g, diff_lambda, diff_subln_g, win_qk_g, win_sink, na_qk_g, na_rpb, mem_qk_g,
           w_branch, w_out):
    depth = w_in.shape[0]
    bp, lp, _ = x_prompt.shape
    bs, ls, _ = x_sample.shape
    mem_len = mem_prompt.shape[1]
    layout = _Layout(bp, lp, bs, ls)
    t = layout.total
    assert t % PROJ_TM == 0 and t % MERGE_TM == 0 and lp % DF_T == 0 and ls % DF_T == 0
    assert lp % (NA_RB * GRID_W) == 0 and ls % (NA_RB * GRID_W) == 0 and lp % MEM_TM == 0 and ls % MEM_TM == 0
    assert lp % (WIN_NB * WIN_BLOCK) == 0 and ls % (WIN_NB * WIN_BLOCK) == 0

    x = jnp.concatenate([x_prompt.reshape(bp * lp, D_MODEL), x_sample.reshape(bs * ls, D_MODEL)], axis=0)
    mem = jnp.concatenate([mem_prompt.reshape(bp * mem_len, D_MODEL), mem_sample.reshape(bs * mem_len, D_MODEL)], axis=0)

    sm = jax.nn.softmax(hgrn_lb.astype(F32), axis=1)
    lb_all = jnp.clip(jnp.cumsum(sm, axis=1) - sm[:, :1], 0.0, 1.0 - 1e-6)
    lam_init = jnp.asarray([0.8 - 0.6 * math.exp(-0.3 * l) for l in range(depth)], F32)
    lam = diff_lambda.astype(F32)
    lmb = jnp.exp(jnp.sum(lam[:, 0] * lam[:, 1], axis=-1)) - jnp.exp(jnp.sum(lam[:, 2] * lam[:, 3], axis=-1)) + lam_init

    w_in_b = w_in.astype(BF16)

    def expand_kv(w):
        w = w.reshape(depth, D_MODEL, WIN_KV_HEADS, HEAD_DIM)
        return jnp.repeat(w, BR_HEADS // WIN_KV_HEADS, axis=2).reshape(depth, D_MODEL, BR_W)

    w1 = jnp.concatenate([w_in_b[:, :, A_Q:A_G], w_in_b[:, :, B_Q:B_G], w_in_b[:, :, C_Q:C_K],
                          expand_kv(w_in_b[:, :, C_K:C_V]), expand_kv(w_in_b[:, :, C_V:C_G]),
                          w_in_b[:, :, D_Q:D_G], w_in_b[:, :, E_Q:E_G]], axis=-1)
    wvt = w_in_b[:, :, B_Q + 2 * BR_W:B_G].transpose(0, 2, 1)
    wg = jnp.concatenate([w_in_b[:, :, A_G:B_Q], w_in_b[:, :, B_G:C_Q], w_in_b[:, :, C_G:D_Q],
                          w_in_b[:, :, D_G:E_Q], w_in_b[:, :, E_G:M_G]], axis=-1)
    wm = w_in_b[:, :, M_G:]
    wb = w_branch.astype(BF16)
    wo = w_out.astype(BF16)
    wmem = w_mem_kv.astype(BF16)

    sc_b = DF_DK ** -0.5 * LOG2E
    sc_h = HEAD_DIM ** -0.5 * LOG2E
    zrow = jnp.zeros((depth, 1, BR_W), F32)

    def per_layer(fn):
        return jnp.stack([fn(l) for l in range(depth)])

    ep = jnp.concatenate([
        per_layer(lambda l: _tile_gain(diff_qk_g[l, 0], 8, sc_b)),
        per_layer(lambda l: _tile_gain(diff_qk_g[l, 1], 8)),
        per_layer(lambda l: _tile_gain(win_qk_g[l, 0], 4, sc_h)),
        per_layer(lambda l: _tile_gain(win_qk_g[l, 1], 4)),
        per_layer(lambda l: _tile_gain(na_qk_g[l, 0], 4, sc_h)),
        per_layer(lambda l: _tile_gain(na_qk_g[l, 1], 4)),
        per_layer(lambda l: _tile_gain(mem_qk_g[l, 0], 4, sc_h)),
        lb_all[0][:, None, :], lb_all[1][:, None, :]] + [zrow] * 7, axis=1)
    gk_mem = per_layer(lambda l: _tile_gain(mem_qk_g[l, 1], 4))
    gn_hg = hgrn_norm_g.astype(F32)[:, None, :]
    gcol = per_layer(lambda l: (jnp.tile(diff_subln_g[l].astype(F32), BR_HEADS) * (1.0 - lam_init[l]))[:, None])
    sink = per_layer(lambda l: jnp.repeat(win_sink[l].astype(F32) * LOG2E, WIN_BLOCK)[:, None])
    na_bias = _na_bias_tables(na_rpb)

    diff_bias, diff_far = _diff_bias_tables(rel_bias)
    win_bias = _win_bias_table(rel_bias)
    sc = jnp.concatenate([jnp.broadcast_to(diff_far[None], (depth, 2, BR_HEADS)),
                          jnp.broadcast_to(lmb[:, None, None], (depth, 1, BR_HEADS))], axis=1)

    g32 = _group_matrix(DF_DK)
    g64 = _group_matrix(HEAD_DIM)
    consts_f = tuple(jnp.asarray(a, dt) for a, dt in zip(_hgrn_constants(HG_C, False), (BF16, F32, F32)))
    consts_b = tuple(jnp.asarray(a, dt) for a, dt in zip(_hgrn_constants(HG_C, True), (BF16, F32, F32)))

    def layer(x, p):
        ng = p["ng"]
        (a_q, a_v, a_lf, a_kk, b_q, b_k, b_v, c_q, c_k, c_v, d_q, d_k, d_v, e_q) = _project(
            x, ng, p["w1"], p["wvt"], p["ep"], g32, g64)
        mk, mv = _mem_kv(mem, p["mng"], p["wmem"], p["gk_mem"], g64)
        o_a = _hgrn(layout, a_q, a_v, a_kk, a_lf, consts_f, consts_b, g64, p["gn_hg"])
        o_b = _diff_attn(layout, b_q, b_k, b_v, diff_bias, p["sc"], p["gcol"])
        o_c = _win_attn(layout, c_q, c_k, c_v, win_bias, p["sink"])
        o_d = _na_attn(layout, d_q, d_k, d_v, p["na_bias"])
        o_e = _mem_attn(layout, e_q, mk.reshape(layout.nseq, mem_len, BR_W), mv.reshape(layout.nseq, mem_len, BR_W))
        y = _merge(x, ng, (o_a, o_b, o_c, o_d, o_e), p["wg"], p["wm"], p["wb"], p["wo"])
        return y, None

    params = dict(ng=norm_g.astype(F32)[:, None, :], mng=mem_norm_g.astype(F32)[:, None, :], w1=w1, wvt=wvt, ep=ep,
                  wmem=wmem, gk_mem=gk_mem, gn_hg=gn_hg, sc=sc, gcol=gcol, sink=sink, na_bias=na_bias,
                  wg=wg, wm=wm, wb=wb, wo=wo)
    x, _ = lax.scan(layer, x, params)
    y_prompt = x[:layout.off1].reshape(bp, lp, D_MODEL)
    y_sample = x[layout.off1:].reshape(bs, ls, D_MODEL)
    return (y_prompt, y_sample)
```
